```python
import math, functools
import jax, jax.numpy as jnp
from jax import lax
import numpy as np

D_MODEL = 1024
BATCH = 8
SEQ = 4096
DEPTH = 2
DEC_BATCH = 128
DEC_SEQ = 8
PAST_LEN = 16384
PAGE_SIZE = 128

N_BRANCH = 4
BRANCH_DIM = D_MODEL // 2
FFN_DIM = 2816
N_MOD = 9
EPS = 1e-6

GM_CHUNK = 128
GM_HEADS = 4
GM_HEAD_DIM = BRANCH_DIM // GM_HEADS

SSM_HEADS = 8
SSM_HEAD_DIM = BRANCH_DIM // SSM_HEADS
SSM_GROUPS = 2
SSM_STATE = 128
SSM_CONV = 4
SSM_CHUNK = 128
SSM_CONV_CH = BRANCH_DIM + 2 * SSM_GROUPS * SSM_STATE

SC_DIM = BRANCH_DIM
SC_WIDTH = 3

MLA_HEADS = 8
Q_LORA = 384
KV_LORA = 256
NOPE_DIM = 64
ROPE_DIM = 32
QK_DIM = NOPE_DIM + ROPE_DIM
V_DIM = BRANCH_DIM // MLA_HEADS
CACHE_W = KV_LORA + ROPE_DIM
ROPE_THETA = 10000.0
ATTN_BLOCK = 128
QK_SCALE = 1.0 / math.sqrt(QK_DIM)

IN_SIZES = (BRANCH_DIM, BRANCH_DIM,
            BRANCH_DIM, SSM_CONV_CH, SSM_HEADS,
            SC_DIM, SC_DIM, SC_DIM,
            Q_LORA, KV_LORA, ROPE_DIM)
IN_TOTAL = sum(IN_SIZES)
IN_SPLITS = tuple(int(s) for s in np.cumsum(IN_SIZES)[:-1])

kernel_name = 'hybrid_gated_parallel_decoder_step'


def rmsnorm(x, g):
    xf = x.astype(jnp.float32)
    y = xf * lax.rsqrt(jnp.mean(xf * xf, axis=-1, keepdims=True) + EPS)
    return (y * g.astype(jnp.float32)).astype(x.dtype)


def modulate(x, shift, scale):
    return x * (1 + scale) + shift


def swiglu(x, w_in, w_out):
    g, u = jnp.split(x @ w_in, 2, axis=-1)
    return (jax.nn.silu(g) * u) @ w_out


def causal_dwconv(hist, w):
    k = w.shape[0]
    L = hist.shape[1] - k + 1
    return sum(hist[:, i:i + L] * w[i] for i in range(k))


def apply_rope(x, pos):
    half = ROPE_DIM // 2
    inv = ROPE_THETA ** (-jnp.arange(half, dtype=jnp.float32) / half)
    ang = pos.astype(jnp.float32)[:, None] * inv
    cos = jnp.cos(ang)[:, None, :]
    sin = jnp.sin(ang)[:, None, :]
    xr = x[..., NOPE_DIM:].astype(jnp.float32)
    x1, x2 = xr[..., :half], xr[..., half:]
    rot = jnp.concatenate([x1 * cos - x2 * sin, x2 * cos + x1 * sin], axis=-1).astype(x.dtype)
    return jnp.concatenate([x[..., :NOPE_DIM], rot], axis=-1)


def spatial_gating(u, v, w_s, b_s):
    b, L, _ = v.shape
    lp = -(-L // GM_CHUNK) * GM_CHUNK
    vc = jnp.pad(v, ((0, 0), (0, lp - L), (0, 0))).reshape(b, lp // GM_CHUNK, GM_CHUNK, GM_HEADS, GM_HEAD_DIM)
    causal = jnp.tril(jnp.ones((GM_CHUNK, GM_CHUNK), dtype=bool))
    w_m = jnp.where(causal, w_s, jnp.zeros_like(w_s)).astype(v.dtype)
    mixed = jnp.einsum('hij,bcjhd->bcihd', w_m, vc) + jnp.transpose(b_s)[None, None, :, :, None].astype(v.dtype)
    return u * mixed.reshape(b, lp, BRANCH_DIM)[:, :L]


def ssd(x, dt, a, bm, cm, s0):
    b, L, H, P = x.shape
    q = math.gcd(L, SSM_CHUNK)
    nc = L // q
    rep = H // SSM_GROUPS
    f32 = jnp.float32
    xf = x.astype(f32).reshape(b, nc, q, H, P)
    bh = jnp.repeat(bm.astype(f32), rep, axis=2).reshape(b, nc, q, H, SSM_STATE)
    ch = jnp.repeat(cm.astype(f32), rep, axis=2).reshape(b, nc, q, H, SSM_STATE)
    dtc = dt.reshape(b, nc, q, H)
    cum = jnp.cumsum(dtc * a, axis=2)
    seg = cum[:, :, :, None, :] - cum[:, :, None, :, :]
    causal = jnp.tril(jnp.ones((q, q), dtype=bool))[None, None, :, :, None]
    decay = jnp.where(causal, jnp.exp(jnp.where(causal, seg, 0.0)), 0.0)
    scores = jnp.einsum('bcihn,bcjhn->bcijh', ch, bh) * decay * dtc[:, :, None, :, :]
    y = jnp.einsum('bcijh,bcjhp->bcihp', scores, xf)
    to_end = jnp.exp(cum[:, :, -1:, :] - cum) * dtc
    chunk_states = jnp.einsum('bcjhn,bcjh,bcjhp->bchpn', bh, to_end, xf)
    chunk_decay = jnp.exp(cum[:, :, -1, :])

    def step(s, inp):
        cs, cd = inp
        return s * cd[:, :, None, None] + cs, s

    s_final, s_in = lax.scan(step, s0.astype(f32), (jnp.moveaxis(chunk_states, 1, 0), jnp.moveaxis(chunk_decay, 1, 0)))
    s_in = jnp.moveaxis(s_in, 0, 1)
    y = y + jnp.einsum('bcihn,bchpn,bcih->bcihp', ch, s_in, jnp.exp(cum))
    return y.reshape(b, L, H, P), s_final


def mamba_branch(z, xbc, dt_raw, conv_buf, s0, lw):
    b, L, _ = z.shape
    hist = jnp.concatenate([conv_buf.astype(xbc.dtype), xbc], axis=1)
    new_buf = hist[:, -(SSM_CONV - 1):]
    xbc = jax.nn.silu(causal_dwconv(hist, lw['w_ssm_conv']) + lw['b_ssm_conv'])
    xs, bm, cm = jnp.split(xbc, (BRANCH_DIM, BRANCH_DIM + SSM_GROUPS * SSM_STATE), axis=-1)
    xs = xs.reshape(b, L, SSM_HEADS, SSM_HEAD_DIM)
    dt = jax.nn.softplus(dt_raw.astype(jnp.float32) + lw['dt_bias'].astype(jnp.float32))
    a = -jnp.exp(lw['a_log'].astype(jnp.float32))
    y, s = ssd(xs, dt, a, bm.reshape(b, L, SSM_GROUPS, SSM_STATE), cm.reshape(b, L, SSM_GROUPS, SSM_STATE), s0)
    y = y + lw['d_skip'].astype(jnp.float32)[:, None] * xs.astype(jnp.float32)
    y = y.reshape(b, L, BRANCH_DIM).astype(z.dtype) * jax.nn.silu(z)
    gs = BRANCH_DIM // SSM_GROUPS
    y = rmsnorm(y.reshape(b, L, SSM_GROUPS, gs), lw['g_ssm_norm'].reshape(SSM_GROUPS, gs)).reshape(b, L, BRANCH_DIM)
    return y, new_buf, s.astype(s0.dtype)


def short_conv_branch(h, gb, gc, conv_buf, w):
    hist = jnp.concatenate([conv_buf.astype(h.dtype), gc * h], axis=1)
    return gb * causal_dwconv(hist, w), hist[:, -(SC_WIDTH - 1):]


def build_keys(rows, pos, w_uk, g_k):
    k_nope = jnp.einsum('...kc,chd->...khd', rows[..., :KV_LORA], w_uk)
    k_rope = jnp.broadcast_to(rows[..., None, KV_LORA:], k_nope.shape[:-1] + (ROPE_DIM,))
    return apply_rope(rmsnorm(jnp.concatenate([k_nope, k_rope], axis=-1), g_k), pos)


def attend(q, k, c, q_pos, k_pos, w_uv):
    s = jnp.einsum('qhd,khd->hqk', q, k).astype(jnp.float32) * QK_SCALE
    s = jnp.where(k_pos[None, None, :] <= q_pos[None, :, None], s, -jnp.inf)
    p = jax.nn.softmax(s, axis=-1).astype(c.dtype)
    ctx = jnp.einsum('hqk,kc->qhc', p, c)
    return jnp.einsum('qhc,chv->qhv', ctx, w_uv).reshape(q.shape[0], MLA_HEADS * V_DIM)


def prompt_attention(q, rows, lw, k_pos):
    b, S = q.shape[:2]
    k = build_keys(rows, k_pos, lw['w_uk'], lw['g_qk'][1])
    c = rows[..., :KV_LORA]
    nb = S // ATTN_BLOCK
    qb = jnp.moveaxis(q.reshape(b, nb, ATTN_BLOCK, MLA_HEADS, QK_DIM), 1, 0)
    starts = jnp.arange(nb, dtype=jnp.int32) * ATTN_BLOCK

    def block(args):
        q_blk, start = args
        q_pos = start + jnp.arange(ATTN_BLOCK, dtype=jnp.int32)
        return jax.vmap(attend, in_axes=(0, 0, 0, None, None, None))(q_blk, k, c, q_pos, k_pos, lw['w_uv'])

    out = lax.map(block, (qb, starts))
    return jnp.moveaxis(out, 0, 1).reshape(b, S, MLA_HEADS * V_DIM)


def sample_attention(q, rows, lw, pool, layer, page_table):
    L = q.shape[1]
    past = page_table.shape[1] * PAGE_SIZE
    k_pos = jnp.arange(past + L, dtype=jnp.int32)
    q_pos = past + jnp.arange(L, dtype=jnp.int32)

    def one(args):
        q_seq, rows_seq, pages = args
        all_rows = jnp.concatenate([pool[layer, pages].reshape(past, CACHE_W).astype(rows_seq.dtype), rows_seq], axis=0)
        k = build_keys(all_rows, k_pos, lw['w_uk'], lw['g_qk'][1])
        return attend(q_seq, k, all_rows[:, :KV_LORA], q_pos, k_pos, lw['w_uv'])

    return lax.map(one, (q, rows, page_table))


def trunk_layer(x, c, pos, ssm_buf, ssm_s0, sc_buf, attn_fn, lw):
    mod = (jax.nn.silu(c) @ lw['w_ada'] + lw['b_ada']).astype(x.dtype)[:, None, :]
    sh1, sc1, gt1, sh2, sc2, gt2, sh3, sc3, gt3 = jnp.split(mod, N_MOD, axis=-1)
    g_norm = lw['g_norm']
    h = x + 0.5 * gt1 * swiglu(modulate(rmsnorm(x, g_norm[0]), sh1, sc1), lw['w_ffn_in'][0], lw['w_ffn_out'][0])
    n = modulate(rmsnorm(h, g_norm[1]), sh2, sc2)
    (a_u, a_v, b_z, b_xbc, b_dt, c_h, c_b, c_c, d_q, d_kv, d_kr) = jnp.split(n @ lw['w_in'], IN_SPLITS, axis=-1)
    a_v = rmsnorm(jax.nn.gelu(a_v), lw['g_gm_v'])
    y_a = spatial_gating(jax.nn.gelu(a_u), a_v, lw['w_spatial'], lw['b_spatial'])
    y_b, ssm_buf_new, ssm_s = mamba_branch(b_z, b_xbc, b_dt, ssm_buf, ssm_s0, lw)
    y_c, sc_buf_new = short_conv_branch(c_h, c_b, c_c, sc_buf, lw['w_sc_conv'])
    q = jnp.einsum('blr,rhd->blhd', rmsnorm(d_q, lw['g_q_lat']), lw['w_uq'])
    q = apply_rope(rmsnorm(q, lw['g_qk'][0]), pos)
    rows = jnp.concatenate([rmsnorm(d_kv, lw['g_kv_lat']), d_kr], axis=-1)
    y_d = attn_fn(q, rows, lw)
    merged = 0
    for r, y_r in enumerate((y_a, y_b, y_c, y_d)):
        gate = jax.nn.sigmoid(n @ lw['w_gate'][r] + lw['b_gate'][r])
        merged = merged + gate * (y_r @ lw['w_branch_out'][r])
    h = h + gt2 * (merged @ lw['w_out'])
    h = h + 0.5 * gt3 * swiglu(modulate(rmsnorm(h, g_norm[2]), sh3, sc3), lw['w_ffn_in'][1], lw['w_ffn_out'][1])
    return h, rows, ssm_buf_new, ssm_s, sc_buf_new, a_v


def setup_inputs(seed: int = 0) -> dict:
    key = jax.random.key(seed)
    f32 = jnp.float32
    counter = [0]

    def nk():
        counter[0] += 1
        return jax.random.fold_in(key, counter[0])

    def normal(shape, scale=1.0):
        return jax.random.normal(nk(), shape, f32) * scale

    def gain(shape):
        return 1.0 + normal(shape, 0.05)

    n_pages = PAST_LEN // PAGE_SIZE
    n_pool = (5 * DEC_BATCH * n_pages) // 4
    page_table = jax.random.permutation(nk(), n_pool)[:DEC_BATCH * n_pages].reshape(DEC_BATCH, n_pages).astype(jnp.int32)
    dt0 = jnp.exp(jax.random.uniform(nk(), (DEPTH, SSM_HEADS), f32, math.log(1e-3), math.log(1e-1)))
    dt_bias = dt0 + jnp.log(-jnp.expm1(-dt0))
    a_log = jnp.log(jax.random.uniform(nk(), (DEPTH, SSM_HEADS), f32, 1.0, 16.0))
    return {
        'x_prompt': normal((BATCH, SEQ, D_MODEL)),
        'x_sample': normal((DEC_BATCH, DEC_SEQ, D_MODEL)),
        'c_prompt': normal((BATCH, D_MODEL)),
        'c_sample': normal((DEC_BATCH, D_MODEL)),
        'cache_mla': normal((DEPTH, n_pool, PAGE_SIZE, CACHE_W)),
        'page_table': page_table,
        'state_ssm': normal((DEPTH, DEC_BATCH, SSM_HEADS, SSM_HEAD_DIM, SSM_STATE), 0.5),
        'state_ssm_conv': normal((DEPTH, DEC_BATCH, SSM_CONV - 1, SSM_CONV_CH)),
        'state_short_conv': normal((DEPTH, DEC_BATCH, SC_WIDTH - 1, SC_DIM)),
        'w_ada': normal((DEPTH, D_MODEL, N_MOD * D_MODEL), 0.5 * D_MODEL ** -0.5),
        'b_ada': normal((DEPTH, N_MOD * D_MODEL), 0.01),
        'g_norm': gain((DEPTH, 3, D_MODEL)),
        'w_ffn_in': normal((DEPTH, 2, D_MODEL, 2 * FFN_DIM), D_MODEL ** -0.5),
        'w_ffn_out': normal((DEPTH, 2, FFN_DIM, D_MODEL), FFN_DIM ** -0.5),
        'w_in': normal((DEPTH, D_MODEL, IN_TOTAL), D_MODEL ** -0.5),
        'g_gm_v': gain((DEPTH, BRANCH_DIM)),
        'w_spatial': normal((DEPTH, GM_HEADS, GM_CHUNK, GM_CHUNK), GM_CHUNK ** -0.5),
        'b_spatial': 1.0 + normal((DEPTH, GM_HEADS, GM_CHUNK), 0.1),
        'w_ssm_conv': normal((DEPTH, SSM_CONV, SSM_CONV_CH), SSM_CONV ** -0.5),
        'b_ssm_conv': normal((DEPTH, SSM_CONV_CH), 0.01),
        'dt_bias': dt_bias,
        'a_log': a_log,
        'd_skip': 1.0 + normal((DEPTH, SSM_HEADS), 0.1),
        'g_ssm_norm': gain((DEPTH, BRANCH_DIM)),
        'w_sc_conv': normal((DEPTH, SC_WIDTH, SC_DIM), SC_WIDTH ** -0.5),
        'g_q_lat': gain((DEPTH, Q_LORA)),
        'w_uq': normal((DEPTH, Q_LORA, MLA_HEADS, QK_DIM), Q_LORA ** -0.5),
        'g_kv_lat': gain((DEPTH, KV_LORA)),
        'w_uk': normal((DEPTH, KV_LORA, MLA_HEADS, NOPE_DIM), KV_LORA ** -0.5),
        'w_uv': normal((DEPTH, KV_LORA, MLA_HEADS, V_DIM), KV_LORA ** -0.5),
        'g_qk': gain((DEPTH, 2, QK_DIM)),
        'w_branch_out': normal((DEPTH, N_BRANCH, BRANCH_DIM, D_MODEL), BRANCH_DIM ** -0.5),
        'w_gate': normal((DEPTH, N_BRANCH, D_MODEL, D_MODEL), D_MODEL ** -0.5),
        'b_gate': normal((DEPTH, N_BRANCH, D_MODEL), 0.01),
        'w_out': normal((DEPTH, D_MODEL, D_MODEL), D_MODEL ** -0.5),
    }


def reference(x_prompt, x_sample, c_prompt, c_sample, cache_mla, page_table, state_ssm, state_ssm_conv,
              state_short_conv, w_ada, b_ada, g_norm, w_ffn_in, w_ffn_out, w_in, g_gm_v, w_spatial, b_spatial,
              w_ssm_conv, b_ssm_conv, dt_bias, a_log, d_skip, g_ssm_norm, w_sc_conv, g_q_lat, w_uq, g_kv_lat,
              w_uk, w_uv, g_qk, w_branch_out, w_gate, b_gate, w_out):
    bp, sp, _ = x_prompt.shape
    past = page_table.shape[1] * PAGE_SIZE
    pos_p = jnp.arange(sp, dtype=jnp.int32)
    pos_s = past + jnp.arange(x_sample.shape[1], dtype=jnp.int32)
    dtype = x_prompt.dtype
    yp, ys = x_prompt, x_sample
    rows_p, rows_s, ssm_p, ssm_s, cv_p, cv_s, sc_p, sc_s, gv_s = ([] for _ in range(9))
    for l in range(DEPTH):
        lw = dict(w_ada=w_ada[l], b_ada=b_ada[l], g_norm=g_norm[l], w_ffn_in=w_ffn_in[l], w_ffn_out=w_ffn_out[l],
                  w_in=w_in[l], g_gm_v=g_gm_v[l], w_spatial=w_spatial[l], b_spatial=b_spatial[l],
                  w_ssm_conv=w_ssm_conv[l], b_ssm_conv=b_ssm_conv[l], dt_bias=dt_bias[l], a_log=a_log[l],
                  d_skip=d_skip[l], g_ssm_norm=g_ssm_norm[l], w_sc_conv=w_sc_conv[l], g_q_lat=g_q_lat[l],
                  w_uq=w_uq[l], g_kv_lat=g_kv_lat[l], w_uk=w_uk[l], w_uv=w_uv[l], g_qk=g_qk[l],
                  w_branch_out=w_branch_out[l], w_gate=w_gate[l], b_gate=b_gate[l], w_out=w_out[l])
        yp, r, cb, s, scb, _ = trunk_layer(
            yp, c_prompt, pos_p,
            jnp.zeros((bp, SSM_CONV - 1, SSM_CONV_CH), dtype),
            jnp.zeros((bp, SSM_HEADS, SSM_HEAD_DIM, SSM_STATE), dtype),
            jnp.zeros((bp, SC_WIDTH - 1, SC_DIM), dtype),
            functools.partial(prompt_attention, k_pos=pos_p), lw)
        rows_p.append(r); cv_p.append(cb); ssm_p.append(s); sc_p.append(scb)
        ys, r, cb, s, scb, v = trunk_layer(
            ys, c_sample, pos_s, state_ssm_conv[l], state_ssm[l], state_short_conv[l],
            functools.partial(sample_attention, pool=cache_mla, layer=l, page_table=page_table), lw)
        rows_s.append(r); cv_s.append(cb); ssm_s.append(s); sc_s.append(scb); gv_s.append(v)
    return (yp, ys, jnp.stack(rows_p), jnp.stack(rows_s), jnp.stack(ssm_p), jnp.stack(ssm_s),
            jnp.stack(cv_p), jnp.stack(cv_s), jnp.stack(sc_p), jnp.stack(sc_s), jnp.stack(gv_s))
```

```python
import functools
import math

import jax
import jax.numpy as jnp
from jax import lax
from jax.experimental import pallas as pl
from jax.experimental.pallas import tpu as pltpu

F32 = jnp.float32
BF16 = jnp.bfloat16
HIGHEST = lax.Precision.HIGHEST

EPS = 1e-6
LANES = 128
N_MOD = 9
GM_CHUNK = 128
GM_HEADS = 4
SSM_HEADS = 8
SSM_GROUPS = 2
SSM_STATE = 128
SSM_CHUNK = 128
SSM_CONV = 4
SC_WIDTH = 3
MLA_HEADS = 8
NOPE_DIM = 64
ROPE_DIM = 32
QK_DIM = NOPE_DIM + ROPE_DIM
ROPE_THETA = 10000.0
QK_SCALE = 1.0 / math.sqrt(QK_DIM)
ROW_BLOCK = 512
ATTN_BLOCK = 512
DECODE_PAGES = 16
VMEM_LIMIT = 56 * 1024 * 1024


def _cparams(sem):
    return pltpu.CompilerParams(dimension_semantics=sem, vmem_limit_bytes=VMEM_LIMIT)


def _resident(shape):
    nd = len(shape)
    return pl.BlockSpec(shape, lambda *_: (0,) * nd, pipeline_mode=pl.Buffered(1))


def _tok_blocks(b, l):
    tl = min(l, ROW_BLOCK)
    bb = min(b, max(1, ROW_BLOCK // tl))
    assert l % tl == 0 and b % bb == 0 and tl % 8 == 0
    return bb, tl


def _silu(x):
    return x * jax.nn.sigmoid(x)


def _rms(x):
    return x * lax.rsqrt(jnp.mean(x * x, axis=-1, keepdims=True) + EPS)


def _dot(a, b):
    return jnp.dot(a, b, preferred_element_type=F32)


def _dot_nt(a, b, precision=None):
    return lax.dot_general(a, b, (((1,), (1,)), ((), ())), preferred_element_type=F32, precision=precision)


def _ada_kernel(c_ref, w_ref, b_ref, o_ref):
    o_ref[...] = _dot(_silu(c_ref[...]).astype(BF16), w_ref[...]) + b_ref[...]


def _ada(c, w, b):
    bt, d = c.shape
    n = w.shape[1]
    return pl.pallas_call(
        _ada_kernel,
        grid=(n // d,),
        in_specs=[pl.BlockSpec((bt, d), lambda j: (0, 0)),
                  pl.BlockSpec((d, d), lambda j: (0, j)),
                  pl.BlockSpec((1, d), lambda j: (0, j))],
        out_specs=pl.BlockSpec((bt, d), lambda j: (0, j)),
        out_shape=jax.ShapeDtypeStruct((bt, n), F32),
        compiler_params=_cparams(("arbitrary",)),
        name="ada",
    )(c, w, b)


def _ffn_kernel(x_ref, sh_ref, sc_ref, gt_ref, g_ref, win_ref, wout_ref, o_ref, acc_ref, *, tf):
    bb, tl, d = x_ref.shape
    f = wout_ref.shape[0]
    x = x_ref[...]
    xn = (_rms(x) * g_ref[...]) * (1.0 + sc_ref[...]) + sh_ref[...]
    xn = xn.reshape(bb * tl, d).astype(BF16)
    for i in range(f // tf):
        g = _dot(xn, win_ref[:, i * tf:(i + 1) * tf])
        u = _dot(xn, win_ref[:, f + i * tf:f + (i + 1) * tf])
        a = (_silu(g) * u).astype(BF16)
        part = _dot(a, wout_ref[i * tf:(i + 1) * tf, :])
        if i == 0:
            acc_ref[...] = part
        else:
            acc_ref[...] += part
    o_ref[...] = x + 0.5 * gt_ref[...] * acc_ref[...].reshape(bb, tl, d)


def _ffn(x, sh, sc, gt, g, w_in, w_out):
    b, l, d = x.shape
    bb, tl = _tok_blocks(b, l)
    f = w_out.shape[0]
    tf = 256 if f % 256 == 0 else LANES
    tok = pl.BlockSpec((bb, tl, d), lambda i, j: (i, j, 0))
    mod = pl.BlockSpec((bb, 1, d), lambda i, j: (i, 0, 0))
    return pl.pallas_call(
        functools.partial(_ffn_kernel, tf=tf),
        grid=(b // bb, l // tl),
        in_specs=[tok, mod, mod, mod, _resident((1, d)), _resident(w_in.shape), _resident(w_out.shape)],
        out_specs=tok,
        out_shape=jax.ShapeDtypeStruct(x.shape, F32),
        scratch_shapes=[pltpu.VMEM((bb * tl, d), F32)],
        compiler_params=_cparams(("parallel", "parallel")),
        name="ffn",
    )(x, sh, sc, gt, g, w_in, w_out)


_PROJ_GROUPS = ("a_u", "a_v", "b_z", "b_xbc", "c_h", "c_b", "c_c", "d_q", "d_kv", "kr0", "krp", "dt")


def _proj_layout(bd, q_lora, kv_lora):
    widths = dict(a_u=bd, a_v=bd, b_z=bd, b_xbc=2 * bd, c_h=bd, c_b=bd, c_c=bd, d_q=q_lora, d_kv=kv_lora,
                  kr0=LANES, krp=LANES, dt=LANES)
    off, layout = 0, {}
    for name in _PROJ_GROUPS:
        layout[name] = (off, widths[name])
        off += widths[name]
    return layout, off


def _proj_weight(w_in, bd, q_lora, kv_lora):
    d = w_in.shape[0]
    sizes = (bd, bd, bd, 2 * bd, SSM_HEADS, bd, bd, bd, q_lora, kv_lora, ROPE_DIM)
    a_u, a_v, b_z, b_xbc, b_dt, c_h, c_b, c_c, d_q, d_kv, d_kr = jnp.split(w_in, _cumsum(sizes)[:-1], axis=1)
    z = lambda n: jnp.zeros((d, n), w_in.dtype)
    kr0 = jnp.concatenate([d_kr, z(LANES - ROPE_DIM)], axis=1)
    krp = jnp.concatenate([z(NOPE_DIM), d_kr, z(LANES - QK_DIM)], axis=1)
    dt = jnp.concatenate([b_dt, z(LANES - SSM_HEADS)], axis=1)
    return jnp.concatenate([a_u, a_v, b_z, b_xbc, c_h, c_b, c_c, d_q, d_kv, kr0, krp, dt], axis=1).astype(BF16)


def _cumsum(sizes):
    out, s = [], 0
    for v in sizes:
        s += v
        out.append(s)
    return out


def _proj_kernel(h_ref, sh_ref, sc_ref, g_ref, w_ref, ggm_ref, gq_ref, gkv_ref,
                 au_ref, av_ref, z_ref, xbc_ref, p_ref, gb_ref, ql_ref, rows_ref, cbf_ref, krp_ref, dt_ref,
                 *, layout):
    bb, tl, d = h_ref.shape
    n = (_rms(h_ref[...]) * g_ref[...]) * (1.0 + sc_ref[...]) + sh_ref[...]
    n = n.reshape(bb * tl, d).astype(BF16)

    def grp(name):
        off, w = layout[name]
        return _dot(n, w_ref[:, off:off + w])

    def put(ref, val):
        ref[...] = val.reshape(bb, tl, val.shape[-1]).astype(ref.dtype)

    put(au_ref, jax.nn.gelu(grp("a_u")))
    put(av_ref, _rms(jax.nn.gelu(grp("a_v"))) * ggm_ref[...])
    put(z_ref, grp("b_z"))
    put(xbc_ref, grp("b_xbc"))
    put(p_ref, grp("c_c") * grp("c_h"))
    put(gb_ref, grp("c_b"))
    put(ql_ref, _rms(grp("d_q")) * gq_ref[...])
    kv = _rms(grp("d_kv")) * gkv_ref[...]
    put(cbf_ref, kv)
    kvw = kv.shape[-1]
    rows_ref[:, :, 0:kvw] = kv.reshape(bb, tl, kvw)
    rows_ref[:, :, kvw:kvw + ROPE_DIM] = grp("kr0")[:, 0:ROPE_DIM].reshape(bb, tl, ROPE_DIM)
    put(krp_ref, grp("krp"))
    put(dt_ref, grp("dt"))


def _proj(h, sh, sc, g, w, ggm, gq, gkv, layout):
    b, l, d = h.shape
    bb, tl = _tok_blocks(b, l)
    bd, q_lora, kv_lora = ggm.shape[1], gq.shape[1], gkv.shape[1]
    tok = lambda w_: pl.BlockSpec((bb, tl, w_), lambda i, j: (i, j, 0))
    mod = pl.BlockSpec((bb, 1, d), lambda i, j: (i, 0, 0))
    outs = [(bd, F32), (bd, F32), (bd, F32), (2 * bd, F32), (bd, F32), (bd, F32), (q_lora, BF16),
            (kv_lora + ROPE_DIM, F32), (kv_lora, BF16), (LANES, F32), (LANES, F32)]
    return pl.pallas_call(
        functools.partial(_proj_kernel, layout=layout),
        grid=(b // bb, l // tl),
        in_specs=[tok(d), mod, mod, _resident((1, d)), _resident(w.shape),
                  _resident(ggm.shape), _resident(gq.shape), _resident(gkv.shape)],
        out_specs=[tok(w_) for w_, _ in outs],
        out_shape=[jax.ShapeDtypeStruct((b, l, w_), dt) for w_, dt in outs],
        compiler_params=_cparams(("parallel", "parallel")),
        name="proj",
    )(h, sh, sc, g, w, ggm, gq, gkv)


def _gmlp_kernel(u_ref, v_ref, w_ref, bt_ref, o_ref, *, chunk):
    bb, tl, bd = u_ref.shape
    hd = bd // GM_HEADS
    row = lax.broadcasted_iota(jnp.int32, (chunk, chunk), 0)
    col = lax.broadcasted_iota(jnp.int32, (chunk, chunk), 1)
    for hh in range(GM_HEADS):
        wm = jnp.where(col <= row, w_ref[hh, 0:chunk, 0:chunk], 0.0)
        bias = bt_ref[0:chunk, hh:hh + 1]
        cols = slice(hh * hd, (hh + 1) * hd)
        if chunk == GM_CHUNK:
            wmb = wm.astype(BF16)
            for bi in range(bb):
                for ci in range(tl // chunk):
                    rows = slice(ci * chunk, (ci + 1) * chunk)
                    mixed = _dot(wmb, v_ref[bi, rows, cols].astype(BF16)) + bias
                    o_ref[bi, rows, cols] = u_ref[bi, rows, cols] * mixed
        else:
            v = v_ref[:, :, cols]
            mixed = jnp.zeros((bb, chunk, hd), F32) + bias
            for j in range(chunk):
                mixed = mixed + wm[:, j:j + 1] * v[:, j:j + 1, :]
            o_ref[:, :, cols] = u_ref[:, :, cols] * mixed


def _gmlp(u, v, w_s, b_t):
    b, l, bd = u.shape
    bb, tl = _tok_blocks(b, l)
    chunk = min(l, GM_CHUNK)
    assert chunk == GM_CHUNK or chunk == tl
    tok = pl.BlockSpec((bb, tl, bd), lambda i, j: (i, j, 0))
    return pl.pallas_call(
        functools.partial(_gmlp_kernel, chunk=chunk),
        grid=(b // bb, l // tl),
        in_specs=[tok, tok, _resident(w_s.shape), _resident(b_t.shape)],
        out_specs=tok,
        out_shape=jax.ShapeDtypeStruct(u.shape, F32),
        compiler_params=_cparams(("parallel", "parallel")),
        name="gmlp",
    )(u, v, w_s, b_t)


def _sconv_kernel(p_ref, gb_ref, buf_ref, w_ref, o_ref, hist_ref):
    bb, tl, c = p_ref.shape
    k = w_ref.shape[0]

    @pl.when(pl.program_id(1) == 0)
    def _():
        hist_ref[:, 0:8, :] = jnp.zeros((bb, 8, c), F32)
        hist_ref[:, 8 - (k - 1):8, :] = buf_ref[...]

    @pl.when(pl.program_id(1) > 0)
    def _():
        hist_ref[:, 0:8, :] = hist_ref[:, tl:tl + 8, :]

    hist_ref[:, 8:8 + tl, :] = p_ref[...]
    acc = jnp.zeros((bb, tl, c), F32)
    for i in range(k):
        s = 8 - (k - 1) + i
        acc = acc + w_ref[i:i + 1, :] * hist_ref[:, s:s + tl, :]
    o_ref[...] = gb_ref[...] * acc


def _sconv(p, gb, buf, w):
    b, l, c = p.shape
    bb, tl = _tok_blocks(b, l)
    tok = pl.BlockSpec((bb, tl, c), lambda i, j: (i, j, 0))
    return pl.pallas_call(
        _sconv_kernel,
        grid=(b // bb, l // tl),
        in_specs=[tok, tok, pl.BlockSpec((bb, buf.shape[1], c), lambda i, j: (i, 0, 0)), _resident(w.shape)],
        out_specs=tok,
        out_shape=jax.ShapeDtypeStruct(p.shape, F32),
        scratch_shapes=[pltpu.VMEM((bb, 8 + tl, c), F32)],
        compiler_params=_cparams(("parallel", "arbitrary")),
        name="sconv",
    )(p, gb, buf, w)


def _softplus(x):
    return jnp.maximum(x, 0.0) + jnp.log1p(jnp.exp(-jnp.abs(x)))


def _ssm_kernel(xbc_ref, z_ref, dtm_ref, dtt_ref, buf_ref, s0_ref, wc_ref, bc_ref, dtb_ref, dtbt_ref,
                alog_ref, alogt_ref, dsk_ref, gn_ref, y_ref, sout_ref, hist_ref, s_ref, *, valid):
    q = SSM_CHUNK
    c = xbc_ref.shape[-1]
    bd = z_ref.shape[-1]
    k = wc_ref.shape[0]
    ci = pl.program_id(1)

    @pl.when(ci == 0)
    def _():
        hist_ref[0:8, :] = jnp.zeros((8, c), F32)
        hist_ref[8 - (k - 1):8, :] = buf_ref[0]
        s_ref[...] = s0_ref[0]

    @pl.when(ci > 0)
    def _():
        hist_ref[0:8, :] = hist_ref[q:q + 8, :]

    hist_ref[8:8 + q, :] = xbc_ref[0]
    acc = jnp.zeros((q, c), F32) + bc_ref[...]
    for i in range(k):
        s = 8 - (k - 1) + i
        acc = acc + wc_ref[i:i + 1, :] * hist_ref[s:s + q, :]
    xc = _silu(acc)
    xs = xc[:, 0:bd]
    gw = SSM_STATE
    bm = [xc[:, bd + g * gw:bd + (g + 1) * gw].astype(BF16) for g in range(SSM_GROUPS)]
    cm = [xc[:, bd + (SSM_GROUPS + g) * gw:bd + (SSM_GROUPS + g + 1) * gw].astype(BF16) for g in range(SSM_GROUPS)]

    row = lax.broadcasted_iota(jnp.int32, (q, q), 0)
    col = lax.broadcasted_iota(jnp.int32, (q, q), 1)
    causal = col <= row
    dt = _softplus(dtm_ref[0][:, 0:SSM_HEADS] + dtb_ref[...])
    dtt = _softplus(dtt_ref[0] + dtbt_ref[...])
    if valid < q:
        dt = jnp.where(lax.broadcasted_iota(jnp.int32, dt.shape, 0) < valid, dt, 0.0)
        dtt = jnp.where(lax.broadcasted_iota(jnp.int32, dtt.shape, 1) < valid, dtt, 0.0)
    da = dt * (-jnp.exp(alog_ref[...]))
    dat = dtt * (-jnp.exp(alogt_ref[...]))
    cum = jnp.dot(causal.astype(F32), da, preferred_element_type=F32, precision=HIGHEST)
    cumt = jnp.dot(dat, (row <= col).astype(F32), preferred_element_type=F32, precision=HIGHEST)
    last = cum[q - 1:q, :]

    lane = lax.broadcasted_iota(jnp.int32, (q, LANES), 1)
    srow = lax.broadcasted_iota(jnp.int32, (LANES, gw), 0)
    p_dim = bd // SSM_HEADS
    rep = SSM_HEADS // SSM_GROUPS
    for pair in range(SSM_HEADS // 2):
        h0, h1 = 2 * pair, 2 * pair + 1
        g = h0 // rep
        cb = _dot_nt(cm[g], bm[g])
        x_pair = xs[:, pair * LANES:(pair + 1) * LANES]
        xb = x_pair.astype(BF16)
        ys = []
        for hh in (h0, h1):
            seg = cum[:, hh:hh + 1] - cumt[hh:hh + 1, :]
            dec = jnp.where(causal, jnp.exp(jnp.where(causal, seg, 0.0)), 0.0) * dtt[hh:hh + 1, :]
            ys.append(_dot((cb * dec).astype(BF16), xb))
        first = lane < p_dim
        cum_pair = jnp.where(first, cum[:, h0:h0 + 1], cum[:, h1:h1 + 1])
        last_pair = jnp.where(first, last[:, h0:h0 + 1], last[:, h1:h1 + 1])
        dt_pair = jnp.where(first, dt[:, h0:h0 + 1], dt[:, h1:h1 + 1])
        s_in = s_ref[pair]
        y_pair = jnp.where(first, ys[0], ys[1]) + jnp.exp(cum_pair) * _dot_nt(cm[g], s_in.astype(BF16))
        y_ref[0, :, pair * LANES:(pair + 1) * LANES] = y_pair
        xw = x_pair * (jnp.exp(last_pair - cum_pair) * dt_pair)
        cs = _dot(xw.T.astype(BF16), bm[g])
        cd = jnp.where(srow < p_dim, jnp.exp(last[:, h0:h0 + 1]), jnp.exp(last[:, h1:h1 + 1]))
        s_ref[pair] = s_in * cd + cs

    y = (y_ref[0] + dsk_ref[...] * xs) * _silu(z_ref[0])
    gs = bd // SSM_GROUPS
    for g in range(SSM_GROUPS):
        y_ref[0, :, g * gs:(g + 1) * gs] = _rms(y[:, g * gs:(g + 1) * gs]) * gn_ref[:, g * gs:(g + 1) * gs]
    sout_ref[0] = s_ref[...]


def _ssm(xbc, z, dtm, dtt, buf, s0, wc, bc, dtb, dtbt, alog, alogt, dsk, gn, valid):
    b, l, c = xbc.shape
    bd = z.shape[-1]
    q = SSM_CHUNK
    assert l % q == 0 and SSM_HEADS % 2 == 0 and bd // SSM_HEADS * 2 == LANES and (SSM_HEADS // SSM_GROUPS) % 2 == 0
    npair = SSM_HEADS // 2
    tok = lambda w_: pl.BlockSpec((1, q, w_), lambda i, j: (i, j, 0))
    per_b = lambda shp: pl.BlockSpec((1,) + shp, lambda i, j: (i,) + (0,) * len(shp))
    small = [wc, bc, dtb, dtbt, alog, alogt, dsk, gn]
    return pl.pallas_call(
        functools.partial(_ssm_kernel, valid=valid),
        grid=(b, l // q),
        in_specs=[tok(c), tok(bd), tok(LANES), pl.BlockSpec((1, SSM_HEADS, q), lambda i, j: (i, 0, j)),
                  per_b(buf.shape[1:]), per_b(s0.shape[1:])] + [_resident(a.shape) for a in small],
        out_specs=[tok(bd), per_b(s0.shape[1:])],
        out_shape=[jax.ShapeDtypeStruct((b, l, bd), F32), jax.ShapeDtypeStruct(s0.shape, F32)],
        scratch_shapes=[pltpu.VMEM((8 + q, c), F32), pltpu.VMEM((npair, LANES, SSM_STATE), F32)],
        compiler_params=_cparams(("parallel", "arbitrary")),
        name="ssm",
    )(xbc, z, dtm, dtt, buf, s0, *small)


def _rope_swap(x, lane, sign):
    half = ROPE_DIM // 2
    return jnp.where(lane < NOPE_DIM + half, pltpu.roll(x, LANES - half, 1), sign * pltpu.roll(x, half, 1))


def _qk_kernel(*refs, decode):
    if decode:
        (ql_ref, cos_ref, sin_ref, wq_ref, gq_ref, gk_ref, wukt_ref, qt_ref, qa_ref, qb_ref) = refs
    else:
        (ql_ref, cbf_ref, krp_ref, cos_ref, sin_ref, wq_ref, wk_ref, gq_ref, gk_ref, q_ref, k_ref) = refs
    bb, tl, _ = ql_ref.shape
    m = bb * tl
    ql = ql_ref[...].reshape(m, ql_ref.shape[-1])
    lane = lax.broadcasted_iota(jnp.int32, (m, LANES), 1)
    cos = jnp.broadcast_to(cos_ref[...][None], (bb, tl, LANES)).reshape(m, LANES)
    sin = jnp.broadcast_to(sin_ref[...][None], (bb, tl, LANES)).reshape(m, LANES)

    def norm_rope(x, g):
        xn = x * lax.rsqrt(jnp.sum(x * x, axis=-1, keepdims=True) * (1.0 / QK_DIM) + EPS) * g
        return xn * cos + _rope_swap(xn, lane, 1.0) * sin

    if not decode:
        cbf = cbf_ref[...].reshape(m, cbf_ref.shape[-1])
        krp = krp_ref[...].reshape(m, LANES)
    for hh in range(MLA_HEADS):
        qf = norm_rope(_dot(ql, wq_ref[hh]), gq_ref[...]) * QK_SCALE
        if decode:
            gk = gk_ref[...]
            nope = lane < NOPE_DIM
            qn = jnp.where(nope, qf * gk, 0.0).astype(BF16)
            qt_ref[:, hh] = _dot(qn, wukt_ref[hh]).reshape(bb, tl, wukt_ref.shape[-1])
            qa = qf * gk
            qb = _rope_swap(qf, lane, -1.0) * gk
            qa_ref[:, hh] = qa[:, NOPE_DIM:QK_DIM].reshape(bb, tl, ROPE_DIM)
            qb_ref[:, hh] = qb[:, NOPE_DIM:QK_DIM].reshape(bb, tl, ROPE_DIM)
        else:
            q_ref[:, hh] = qf.reshape(bb, tl, LANES).astype(BF16)
            kf = norm_rope(_dot(cbf, wk_ref[hh]) + krp, gk_ref[...])
            k_ref[:, hh] = kf.reshape(bb, tl, LANES).astype(BF16)


def _qk_prompt(ql, cbf, krp, cos, sin, wq, wk, gq, gk):
    b, l, _ = ql.shape
    bb, tl = _tok_blocks(b, l)
    tok = lambda w_: pl.BlockSpec((bb, tl, w_), lambda i, j: (i, j, 0))
    tab = pl.BlockSpec((tl, LANES), lambda i, j: (j, 0))
    head = pl.BlockSpec((bb, MLA_HEADS, tl, LANES), lambda i, j: (i, 0, j, 0))
    shp = jax.ShapeDtypeStruct((b, MLA_HEADS, l, LANES), BF16)
    return pl.pallas_call(
        functools.partial(_qk_kernel, decode=False),
        grid=(b // bb, l // tl),
        in_specs=[tok(ql.shape[-1]), tok(cbf.shape[-1]), tok(LANES), tab, tab,
                  _resident(wq.shape), _resident(wk.shape), _resident(gq.shape), _resident(gk.shape)],
        out_specs=[head, head],
        out_shape=[shp, shp],
        compiler_params=_cparams(("parallel", "parallel")),
        name="qk_prompt",
    )(ql, cbf, krp, cos, sin, wq, wk, gq, gk)


def _q_decode(ql, cos, sin, wq, gq, gk, wukt):
    b, l, _ = ql.shape
    bb, tl = _tok_blocks(b, l)
    assert tl == l
    kv = wukt.shape[-1]
    tok = lambda w_: pl.BlockSpec((bb, tl, w_), lambda i: (i, 0, 0))
    head = lambda w_: pl.BlockSpec((bb, MLA_HEADS, tl, w_), lambda i: (i, 0, 0, 0))
    shp = lambda w_: jax.ShapeDtypeStruct((b, MLA_HEADS, l, w_), F32)
    return pl.pallas_call(
        functools.partial(_qk_kernel, decode=True),
        grid=(b // bb,),
        in_specs=[tok(ql.shape[-1]), _resident(cos.shape), _resident(sin.shape),
                  _resident(wq.shape), _resident(gq.shape), _resident(gk.shape), _resident(wukt.shape)],
        out_specs=[head(kv), head(ROPE_DIM), head(ROPE_DIM)],
        out_shape=[shp(kv), shp(ROPE_DIM), shp(ROPE_DIM)],
        compiler_params=_cparams(("parallel",)),
        name="q_decode",
    )(ql, cos, sin, wq, gq, gk, wukt)


def _flash_kernel(q_ref, k_ref, c_ref, wuv_ref, o_ref, m_ref, l_ref, acc_ref):
    tq = q_ref.shape[2]
    tk = k_ref.shape[2]
    qi, ki = pl.program_id(1), pl.program_id(2)

    @pl.when(ki == 0)
    def _():
        m_ref[...] = jnp.full(m_ref.shape, -jnp.inf, F32)
        l_ref[...] = jnp.zeros(l_ref.shape, F32)
        acc_ref[...] = jnp.zeros(acc_ref.shape, F32)

    def step(masked):
        c = c_ref[0]
        if masked:
            keep = (lax.broadcasted_iota(jnp.int32, (tq, tk), 1) <= lax.broadcasted_iota(jnp.int32, (tq, tk), 0))
        for hh in range(MLA_HEADS):
            s = _dot_nt(q_ref[0, hh], k_ref[0, hh])
            if masked:
                s = jnp.where(keep, s, -jnp.inf)
            m_prev = m_ref[hh]
            m_new = jnp.maximum(m_prev, jnp.max(s, axis=-1, keepdims=True))
            alpha = jnp.exp(m_prev - m_new)
            p = jnp.exp(s - m_new)
            l_ref[hh] = alpha * l_ref[hh] + jnp.sum(p, axis=-1, keepdims=True)
            acc_ref[hh] = alpha * acc_ref[hh] + _dot(p.astype(BF16), c)
            m_ref[hh] = m_new

    @pl.when(ki < qi)
    def _():
        step(False)

    @pl.when(ki == qi)
    def _():
        step(True)
        vd = wuv_ref.shape[-1]
        for hh in range(MLA_HEADS):
            ctx = (acc_ref[hh] / l_ref[hh]).astype(BF16)
            o_ref[0, :, hh * vd:(hh + 1) * vd] = _dot(ctx, wuv_ref[hh])


def _flash(q, k, cbf, wuv):
    b, h, l, _ = q.shape
    t = min(l, ATTN_BLOCK)
    assert l % t == 0
    kv = cbf.shape[-1]
    vd = wuv.shape[-1]
    return pl.pallas_call(
        _flash_kernel,
        grid=(b, l // t, l // t),
        in_specs=[pl.BlockSpec((1, h, t, LANES), lambda i, qi, ki: (i, 0, qi, 0)),
                  pl.BlockSpec((1, h, t, LANES), lambda i, qi, ki: (i, 0, jnp.minimum(ki, qi), 0)),
                  pl.BlockSpec((1, t, kv), lambda i, qi, ki: (i, jnp.minimum(ki, qi), 0)),
                  _resident(wuv.shape)],
        out_specs=pl.BlockSpec((1, t, h * vd), lambda i, qi, ki: (i, qi, 0)),
        out_shape=jax.ShapeDtypeStruct((b, l, h * vd), F32),
        scratch_shapes=[pltpu.VMEM((h, t, 1), F32), pltpu.VMEM((h, t, 1), F32), pltpu.VMEM((h, t, kv), F32)],
        compiler_params=_cparams(("parallel", "parallel", "arbitrary")),
        name="flash",
    )(q, k, cbf, wuv)


def _decode_kernel(pt_ref, qt_ref, qa_ref, qb_ref, cc_ref, sn_ref, ccn_ref, snn_ref, new_ref, wukt_ref, wuv_ref,
                   *rest, pages):
    page_refs = rest[:pages]
    o_ref, m_ref, l_ref, acc_ref = rest[pages:]
    j = pl.program_id(1)
    nrow = qt_ref.shape[1]
    lq = nrow // MLA_HEADS
    kvw = qt_ref.shape[2]

    @pl.when(j == 0)
    def _():
        m_ref[...] = jnp.full(m_ref.shape, -jnp.inf, F32)
        l_ref[...] = jnp.zeros(l_ref.shape, F32)
        acc_ref[...] = jnp.zeros(acc_ref.shape, F32)

    qt = qt_ref[0].astype(BF16)
    qa = qa_ref[0].astype(BF16)
    qb = qb_ref[0].astype(BF16)
    wukt = wukt_ref[...]
    ones = jnp.ones((MLA_HEADS, ROPE_DIM), F32)

    def attend(rows, cc, sn, keep):
        nk = rows.shape[0]
        c = rows[:, 0:kvw].astype(BF16)
        kr = rows[:, kvw:kvw + ROPE_DIM]
        kt = _dot_nt(wukt, c)
        ss = jnp.sum((kt * kt).reshape(MLA_HEADS, NOPE_DIM, nk), axis=1)
        ss = ss + _dot_nt(ones, kr * kr, precision=HIGHEST)
        inv = lax.rsqrt(ss * (1.0 / QK_DIM) + EPS)
        s = _dot_nt(qt, c) + _dot_nt(qa, (kr * cc).astype(BF16)) + _dot_nt(qb, (kr * sn).astype(BF16))
        s = (s.reshape(MLA_HEADS, lq, nk) * inv[:, None, :]).reshape(nrow, nk)
        if keep is not None:
            s = jnp.where(keep, s, -jnp.inf)
        m_prev = m_ref[...]
        m_new = jnp.maximum(m_prev, jnp.max(s, axis=-1, keepdims=True))
        alpha = jnp.exp(m_prev - m_new)
        p = jnp.exp(s - m_new)
        l_ref[...] = alpha * l_ref[...] + jnp.sum(p, axis=-1, keepdims=True)
        acc_ref[...] = alpha * acc_ref[...] + _dot(p.astype(BF16), c)
        m_ref[...] = m_new

    psz = page_refs[0].shape[0]
    for pi in range(pages):
        attend(page_refs[pi][...], cc_ref[pi * psz:(pi + 1) * psz, :], sn_ref[pi * psz:(pi + 1) * psz, :], None)

    @pl.when(j == pl.num_programs(1) - 1)
    def _():
        tq = lax.broadcasted_iota(jnp.int32, (nrow, lq), 0) % lq
        tk = lax.broadcasted_iota(jnp.int32, (nrow, lq), 1)
        attend(new_ref[0], ccn_ref[...], snn_ref[...], tk <= tq)
        ctx = (acc_ref[...] / l_ref[...]).astype(BF16)
        vd = wuv_ref.shape[-1]
        for hh in range(MLA_HEADS):
            o_ref[0, :, hh * vd:(hh + 1) * vd] = _dot(ctx[hh * lq:(hh + 1) * lq, :], wuv_ref[hh])


def _decode(page_table, qt, qa, qb, cc, sn, ccn, snn, rows_new, wukt, wuv, pool, layer):
    bs, nrow, kvw = qt.shape
    n_pages = page_table.shape[1]
    psz, cw = pool.shape[2], pool.shape[3]
    pages = math.gcd(n_pages, DECODE_PAGES)
    lq = rows_new.shape[1]
    vd = wuv.shape[-1]
    per_s = lambda shp: pl.BlockSpec((1,) + shp, lambda s, j, pt: (s,) + (0,) * len(shp))
    tab = pl.BlockSpec((pages * psz, ROPE_DIM), lambda s, j, pt: (j, 0))

    def page_spec(pi):
        return pl.BlockSpec((None, None, psz, cw), lambda s, j, pt: (layer, pt[s, j * pages + pi], 0, 0))

    grid_spec = pltpu.PrefetchScalarGridSpec(
        num_scalar_prefetch=1,
        grid=(bs, n_pages // pages),
        in_specs=[per_s((nrow, kvw)), per_s((nrow, ROPE_DIM)), per_s((nrow, ROPE_DIM)), tab, tab,
                  _resident(ccn.shape), _resident(snn.shape), per_s((lq, cw)),
                  _resident(wukt.shape), _resident(wuv.shape)] + [page_spec(pi) for pi in range(pages)],
        out_specs=per_s((lq, MLA_HEADS * vd)),
        scratch_shapes=[pltpu.VMEM((nrow, 1), F32), pltpu.VMEM((nrow, 1), F32), pltpu.VMEM((nrow, kvw), F32)],
    )
    return pl.pallas_call(
        functools.partial(_decode_kernel, pages=pages),
        grid_spec=grid_spec,
        out_shape=jax.ShapeDtypeStruct((bs, lq, MLA_HEADS * vd), F32),
        compiler_params=_cparams(("parallel", "arbitrary")),
        name="decode",
    )(page_table, qt, qa, qb, cc, sn, ccn, snn, rows_new, wukt, wuv, *([pool] * pages))


def _merge_kernel(h_ref, sh_ref, sc_ref, gt_ref, g_ref, ya_ref, yb_ref, yc_ref, yd_ref,
                  wg_ref, bg_ref, wb_ref, wo_ref, o_ref):
    bb, tl, d = h_ref.shape
    m = bb * tl
    h = h_ref[...]
    n = (_rms(h) * g_ref[...]) * (1.0 + sc_ref[...]) + sh_ref[...]
    n = n.reshape(m, d).astype(BF16)
    merged = None
    for r, y_ref in enumerate((ya_ref, yb_ref, yc_ref, yd_ref)):
        gate = jax.nn.sigmoid(_dot(n, wg_ref[r]) + bg_ref[r])
        term = gate * _dot(y_ref[...].reshape(m, y_ref.shape[-1]).astype(BF16), wb_ref[r])
        merged = term if merged is None else merged + term
    out = _dot(merged.astype(BF16), wo_ref[...])
    o_ref[...] = h + gt_ref[...] * out.reshape(bb, tl, d)


def _merge(h, sh, sc, gt, g, ys, wg, bg, wb, wo):
    b, l, d = h.shape
    bb, tl = _tok_blocks(b, l)
    tok = lambda w_: pl.BlockSpec((bb, tl, w_), lambda i, j: (i, j, 0))
    mod = pl.BlockSpec((bb, 1, d), lambda i, j: (i, 0, 0))
    return pl.pallas_call(
        _merge_kernel,
        grid=(b // bb, l // tl),
        in_specs=[tok(d), mod, mod, mod, _resident((1, d))] + [tok(y.shape[-1]) for y in ys]
                 + [_resident(wg.shape), _resident(bg.shape), _resident(wb.shape), _resident(wo.shape)],
        out_specs=tok(d),
        out_shape=jax.ShapeDtypeStruct(h.shape, F32),
        compiler_params=_cparams(("parallel", "parallel")),
        name="merge",
    )(h, sh, sc, gt, g, *ys, wg, bg, wb, wo)


def _rope_tables(pos):
    half = ROPE_DIM // 2
    inv = ROPE_THETA ** (-jnp.arange(half, dtype=F32) / half)
    ang = pos.astype(F32)[:, None] * inv
    return jnp.cos(ang), jnp.sin(ang)


def _lane_tables(pos):
    cos, sin = _rope_tables(pos)
    n = pos.shape[0]
    cos_t = jnp.concatenate([jnp.ones((n, NOPE_DIM), F32), cos, cos, jnp.zeros((n, LANES - QK_DIM), F32)], axis=1)
    sin_t = jnp.concatenate([jnp.zeros((n, NOPE_DIM), F32), -sin, sin, jnp.zeros((n, LANES - QK_DIM), F32)], axis=1)
    return cos_t, sin_t


def _pad_lanes(x, before, total):
    pad = [(0, 0)] * (x.ndim - 1) + [(before, total - before - x.shape[-1])]
    return jnp.pad(x, pad)


def _layer_weights(lw):
    d = lw["w_in"].shape[0]
    bd = lw["g_gm_v"].shape[0]
    q_lora, kv_lora = lw["g_q_lat"].shape[0], lw["g_kv_lat"].shape[0]
    layout, _ = _proj_layout(bd, q_lora, kv_lora)
    w = {}
    w["layout"] = layout
    w["w_ada"] = lw["w_ada"].astype(BF16)
    w["b_ada"] = lw["b_ada"][None, :]
    w["g_norm"] = [lw["g_norm"][i][None, :] for i in range(3)]
    w["w_ffn_in"] = [lw["w_ffn_in"][i].astype(BF16) for i in range(2)]
    w["w_ffn_out"] = [lw["w_ffn_out"][i].astype(BF16) for i in range(2)]
    w["w_proj"] = _proj_weight(lw["w_in"], bd, q_lora, kv_lora)
    w["g_gm_v"] = lw["g_gm_v"][None, :]
    w["g_q_lat"] = lw["g_q_lat"][None, :]
    w["g_kv_lat"] = lw["g_kv_lat"][None, :]
    w["w_spatial"] = lw["w_spatial"]
    w["b_spatial_t"] = lw["b_spatial"].T
    w["w_ssm_conv"] = lw["w_ssm_conv"]
    w["b_ssm_conv"] = lw["b_ssm_conv"][None, :]
    w["dt_bias"] = lw["dt_bias"][None, :]
    w["dt_bias_t"] = lw["dt_bias"][:, None]
    w["a_log"] = lw["a_log"][None, :]
    w["a_log_t"] = lw["a_log"][:, None]
    w["d_skip"] = jnp.repeat(lw["d_skip"], bd // SSM_HEADS)[None, :]
    w["g_ssm_norm"] = lw["g_ssm_norm"][None, :]
    w["w_sc_conv"] = lw["w_sc_conv"]
    w["wq"] = _pad_lanes(jnp.moveaxis(lw["w_uq"], 1, 0), 0, LANES).astype(BF16)
    wuk = jnp.moveaxis(lw["w_uk"], 1, 0)
    w["wk"] = _pad_lanes(wuk, 0, LANES).astype(BF16)
    wukt = jnp.swapaxes(wuk, 1, 2)
    w["wukt_pad"] = jnp.pad(wukt, ((0, 0), (0, LANES - NOPE_DIM), (0, 0))).astype(BF16)
    w["wukt_flat"] = wukt.reshape(MLA_HEADS * NOPE_DIM, kv_lora).astype(BF16)
    w["wuv"] = jnp.moveaxis(lw["w_uv"], 1, 0).astype(BF16)
    w["gq"] = _pad_lanes(lw["g_qk"][0][None, :], 0, LANES)
    w["gk"] = _pad_lanes(lw["g_qk"][1][None, :], 0, LANES)
    w["w_gate"] = lw["w_gate"].astype(BF16)
    w["b_gate"] = lw["b_gate"][:, None, :]
    w["w_branch_out"] = lw["w_branch_out"].astype(BF16)
    w["w_out"] = lw["w_out"].astype(BF16)
    return w


def _trunk_layer(x, mod, w, ssm_buf, ssm_s0, sc_buf, attn_fn):
    b, l, d = x.shape
    sh1, sc1, gt1, sh2, sc2, gt2, sh3, sc3, gt3 = [m[:, None, :] for m in jnp.split(mod, N_MOD, axis=-1)]
    h = _ffn(x, sh1, sc1, gt1, w["g_norm"][0], w["w_ffn_in"][0], w["w_ffn_out"][0])
    (a_u, a_v, b_z, b_xbc, c_p, c_b, ql, rows, cbf, krp, dtm) = _proj(
        h, sh2, sc2, w["g_norm"][1], w["w_proj"], w["g_gm_v"], w["g_q_lat"], w["g_kv_lat"], w["layout"])
    y_a = _gmlp(a_u, a_v, w["w_spatial"], w["b_spatial_t"])
    lp = -(-l // SSM_CHUNK) * SSM_CHUNK
    padl = lambda t: jnp.pad(t, ((0, 0), (0, lp - l), (0, 0)))
    dtt = jnp.swapaxes(padl(dtm)[:, :, 0:SSM_HEADS], 1, 2)
    npair = SSM_HEADS // 2
    s0 = ssm_s0.reshape(b, npair, LANES, SSM_STATE)
    y_b, s_new = _ssm(padl(b_xbc), padl(b_z), padl(dtm), dtt, ssm_buf, s0, w["w_ssm_conv"], w["b_ssm_conv"],
                      w["dt_bias"], w["dt_bias_t"], w["a_log"], w["a_log_t"], w["d_skip"], w["g_ssm_norm"],
                      valid=min(l, SSM_CHUNK) if lp != l else SSM_CHUNK)
    y_b = y_b[:, :l]
    s_new = s_new.reshape(ssm_s0.shape)
    ssm_buf_new = jnp.concatenate([ssm_buf, b_xbc], axis=1)[:, -(SSM_CONV - 1):]
    y_c = _sconv(c_p, c_b, sc_buf, w["w_sc_conv"])
    sc_buf_new = jnp.concatenate([sc_buf, c_p], axis=1)[:, -(SC_WIDTH - 1):]
    y_d = attn_fn(ql, rows, cbf, krp)
    h = _merge(h, sh2, sc2, gt2, w["g_norm"][1], (y_a, y_b, y_c, y_d), w["w_gate"], w["b_gate"],
               w["w_branch_out"], w["w_out"])
    h = _ffn(h, sh3, sc3, gt3, w["g_norm"][2], w["w_ffn_in"][1], w["w_ffn_out"][1])
    return h, rows, ssm_buf_new, s_new, sc_buf_new, a_v


def kernel(x_prompt, x_sample, c_prompt, c_sample, cache_mla, page_table, state_ssm, state_ssm_conv,
           state_short_conv, w_ada, b_ada, g_norm, w_ffn_in, w_ffn_out, w_in, g_gm_v, w_spatial, b_spatial,
           w_ssm_conv, b_ssm_conv, dt_bias, a_log, d_skip, g_ssm_norm, w_sc_conv, g_q_lat, w_uq, g_kv_lat,
           w_uk, w_uv, g_qk, w_branch_out, w_gate, b_gate, w_out):
    params = dict(w_ada=w_ada, b_ada=b_ada, g_norm=g_norm, w_ffn_in=w_ffn_in, w_ffn_out=w_ffn_out, w_in=w_in,
                  g_gm_v=g_gm_v, w_spatial=w_spatial, b_spatial=b_spatial, w_ssm_conv=w_ssm_conv,
                  b_ssm_conv=b_ssm_conv, dt_bias=dt_bias, a_log=a_log, d_skip=d_skip, g_ssm_norm=g_ssm_norm,
                  w_sc_conv=w_sc_conv, g_q_lat=g_q_lat, w_uq=w_uq, g_kv_lat=g_kv_lat, w_uk=w_uk, w_uv=w_uv,
                  g_qk=g_qk, w_branch_out=w_branch_out, w_gate=w_gate, b_gate=b_gate, w_out=w_out)
    depth = w_ada.shape[0]
    bp, lp, d = x_prompt.shape
    bs, ls, _ = x_sample.shape
    page = cache_mla.shape[2]
    past = page_table.shape[1] * page
    bd = g_gm_v.shape[1]
    conv_ch = w_ssm_conv.shape[-1]

    pos_p = jnp.arange(lp, dtype=jnp.int32)
    pos_s = past + jnp.arange(ls, dtype=jnp.int32)
    cos_p, sin_p = _lane_tables(pos_p)
    cos_s, sin_s = _lane_tables(pos_s)
    cos_k, sin_k = _rope_tables(jnp.arange(past, dtype=jnp.int32))
    cc_k, sn_k = jnp.concatenate([cos_k, cos_k], axis=1), jnp.concatenate([sin_k, sin_k], axis=1)
    cos_n, sin_n = _rope_tables(pos_s)
    cc_n, sn_n = jnp.concatenate([cos_n, cos_n], axis=1), jnp.concatenate([sin_n, sin_n], axis=1)
    c_all = jnp.concatenate([c_prompt, c_sample], axis=0)

    yp, ys = x_prompt, x_sample
    outs = [[] for _ in range(9)]
    for layer in range(depth):
        w = _layer_weights({k: v[layer] for k, v in params.items()})
        mod = _ada(c_all, w["w_ada"], w["b_ada"])

        def attn_prompt(ql, rows, cbf, krp, w=w):
            q, k = _qk_prompt(ql, cbf, krp, cos_p, sin_p, w["wq"], w["wk"], w["gq"], w["gk"])
            return _flash(q, k, cbf, w["wuv"])

        def attn_sample(ql, rows, cbf, krp, w=w, layer=layer):
            qt, qa, qb = _q_decode(ql, cos_s, sin_s, w["wq"], w["gq"], w["gk"], w["wukt_pad"])
            flat = lambda t: t.reshape(bs, MLA_HEADS * ls, t.shape[-1])
            return _decode(page_table, flat(qt), flat(qa), flat(qb), cc_k, sn_k, cc_n, sn_n, rows,
                           w["wukt_flat"], w["wuv"], cache_mla, layer)

        yp, r, cb, s, scb, _ = _trunk_layer(
            yp, mod[:bp], w,
            jnp.zeros((bp, SSM_CONV - 1, conv_ch), F32),
            jnp.zeros((bp, SSM_HEADS, bd // SSM_HEADS, SSM_STATE), F32),
            jnp.zeros((bp, SC_WIDTH - 1, bd), F32), attn_prompt)
        for lst, v in zip((outs[0], outs[2], outs[4], outs[6]), (r, s, cb, scb)):
            lst.append(v)
        ys, r, cb, s, scb, v = _trunk_layer(
            ys, mod[bp:], w, state_ssm_conv[layer], state_ssm[layer], state_short_conv[layer], attn_sample)
        for lst, val in zip((outs[1], outs[3], outs[5], outs[7], outs[8]), (r, s, cb, scb, v)):
            lst.append(val)
    st = jnp.stack
    return (yp, ys, st(outs[0]), st(outs[1]), st(outs[2]), st(outs[3]), st(outs[4]), st(outs[5]),
            st(outs[6]), st(outs[7]), st(outs[8]))
```

```python
import functools
import math

import jax
import jax.numpy as jnp
from jax import lax
from jax.experimental import pallas as pl
from jax.experimental.pallas import tpu as pltpu

F32 = jnp.float32
BF16 = jnp.bfloat16
HIGHEST = lax.Precision.HIGHEST

EPS = 1e-6
LANES = 128
N_MOD = 9
GM_CHUNK = 128
GM_HEADS = 4
SSM_HEADS = 8
SSM_GROUPS = 2
SSM_STATE = 128
SSM_CHUNK = 128
SSM_CONV = 4
SC_WIDTH = 3
MLA_HEADS = 8
NOPE_DIM = 64
ROPE_DIM = 32
QK_DIM = NOPE_DIM + ROPE_DIM
ROPE_THETA = 10000.0
QK_SCALE = 1.0 / math.sqrt(QK_DIM)
ROW_BLOCK = 512
ATTN_BLOCK = 512
DECODE_PAGES = 16
DECODE_GROUP = 2
VMEM_LIMIT = 56 * 1024 * 1024


def _cparams(sem):
    return pltpu.CompilerParams(dimension_semantics=sem, vmem_limit_bytes=VMEM_LIMIT)


def _resident(shape):
    nd = len(shape)
    return pl.BlockSpec(shape, lambda *_: (0,) * nd, pipeline_mode=pl.Buffered(1))


def _tok_blocks(b, l):
    tl = min(l, ROW_BLOCK)
    bb = min(b, max(1, ROW_BLOCK // tl))
    assert l % tl == 0 and b % bb == 0 and tl % 8 == 0
    return bb, tl


def _silu(x):
    return x * jax.nn.sigmoid(x)


def _rms(x):
    return x * lax.rsqrt(jnp.mean(x * x, axis=-1, keepdims=True) + EPS)


def _dot(a, b):
    return jnp.dot(a, b, preferred_element_type=F32)


def _dot_nt(a, b, precision=None):
    return lax.dot_general(a, b, (((1,), (1,)), ((), ())), preferred_element_type=F32, precision=precision)


def _ada_kernel(c_ref, w_ref, b_ref, o_ref):
    o_ref[...] = _dot(_silu(c_ref[...]).astype(BF16), w_ref[...]) + b_ref[...]


def _ada(c, w, b):
    bt, d = c.shape
    n = w.shape[1]
    return pl.pallas_call(
        _ada_kernel,
        grid=(n // d,),
        in_specs=[pl.BlockSpec((bt, d), lambda j: (0, 0)),
                  pl.BlockSpec((d, d), lambda j: (0, j)),
                  pl.BlockSpec((1, d), lambda j: (0, j))],
        out_specs=pl.BlockSpec((bt, d), lambda j: (0, j)),
        out_shape=jax.ShapeDtypeStruct((bt, n), F32),
        compiler_params=_cparams(("arbitrary",)),
        name="ada",
    )(c, w, b)


def _ffn_kernel(x_ref, sh_ref, sc_ref, gt_ref, g_ref, win_ref, wout_ref, o_ref, acc_ref, *, tf):
    bb, tl, d = x_ref.shape
    f = wout_ref.shape[0]
    x = x_ref[...]
    xn = (_rms(x) * g_ref[...]) * (1.0 + sc_ref[...]) + sh_ref[...]
    xn = xn.reshape(bb * tl, d).astype(BF16)
    for i in range(f // tf):
        g = _dot(xn, win_ref[:, i * tf:(i + 1) * tf])
        u = _dot(xn, win_ref[:, f + i * tf:f + (i + 1) * tf])
        a = (_silu(g) * u).astype(BF16)
        part = _dot(a, wout_ref[i * tf:(i + 1) * tf, :])
        if i == 0:
            acc_ref[...] = part
        else:
            acc_ref[...] += part
    o_ref[...] = x + 0.5 * gt_ref[...] * acc_ref[...].reshape(bb, tl, d)


def _ffn(x, sh, sc, gt, g, w_in, w_out):
    b, l, d = x.shape
    bb, tl = _tok_blocks(b, l)
    f = w_out.shape[0]
    tf = 256 if f % 256 == 0 else LANES
    tok = pl.BlockSpec((bb, tl, d), lambda i, j: (i, j, 0))
    mod = pl.BlockSpec((bb, 1, d), lambda i, j: (i, 0, 0))
    return pl.pallas_call(
        functools.partial(_ffn_kernel, tf=tf),
        grid=(b // bb, l // tl),
        in_specs=[tok, mod, mod, mod, _resident((1, d)), _resident(w_in.shape), _resident(w_out.shape)],
        out_specs=tok,
        out_shape=jax.ShapeDtypeStruct(x.shape, F32),
        scratch_shapes=[pltpu.VMEM((bb * tl, d), F32)],
        compiler_params=_cparams(("parallel", "parallel")),
        name="ffn",
    )(x, sh, sc, gt, g, w_in, w_out)


_PROJ_GROUPS = ("a_u", "a_v", "b_z", "b_xbc", "c_h", "c_b", "c_c", "d_q", "d_kv", "kr0", "krp", "dt")


def _proj_layout(bd, q_lora, kv_lora):
    widths = dict(a_u=bd, a_v=bd, b_z=bd, b_xbc=2 * bd, c_h=bd, c_b=bd, c_c=bd, d_q=q_lora, d_kv=kv_lora,
                  kr0=LANES, krp=LANES, dt=LANES)
    off, layout = 0, {}
    for name in _PROJ_GROUPS:
        layout[name] = (off, widths[name])
        off += widths[name]
    return layout, off


def _proj_weight(w_in, bd, q_lora, kv_lora):
    d = w_in.shape[0]
    sizes = (bd, bd, bd, 2 * bd, SSM_HEADS, bd, bd, bd, q_lora, kv_lora, ROPE_DIM)
    a_u, a_v, b_z, b_xbc, b_dt, c_h, c_b, c_c, d_q, d_kv, d_kr = jnp.split(w_in, _cumsum(sizes)[:-1], axis=1)
    z = lambda n: jnp.zeros((d, n), w_in.dtype)
    kr0 = jnp.concatenate([d_kr, z(LANES - ROPE_DIM)], axis=1)
    krp = jnp.concatenate([z(NOPE_DIM), d_kr, z(LANES - QK_DIM)], axis=1)
    dt = jnp.concatenate([b_dt, z(LANES - SSM_HEADS)], axis=1)
    return jnp.concatenate([a_u, a_v, b_z, b_xbc, c_h, c_b, c_c, d_q, d_kv, kr0, krp, dt], axis=1).astype(BF16)


def _cumsum(sizes):
    out, s = [], 0
    for v in sizes:
        s += v
        out.append(s)
    return out


def _proj_kernel(h_ref, sh_ref, sc_ref, g_ref, w_ref, ggm_ref, gq_ref, gkv_ref,
                 au_ref, av_ref, z_ref, xbc_ref, p_ref, gb_ref, ql_ref, rows_ref, cbf_ref, krp_ref, dt_ref,
                 *maybe_ct_ref, layout):
    bb, tl, d = h_ref.shape
    n = (_rms(h_ref[...]) * g_ref[...]) * (1.0 + sc_ref[...]) + sh_ref[...]
    n = n.reshape(bb * tl, d).astype(BF16)

    def grp(name):
        off, w = layout[name]
        return _dot(n, w_ref[:, off:off + w])

    def put(ref, val):
        ref[...] = val.reshape(bb, tl, val.shape[-1]).astype(ref.dtype)

    put(au_ref, jax.nn.gelu(grp("a_u")))
    put(av_ref, _rms(jax.nn.gelu(grp("a_v"))) * ggm_ref[...])
    put(z_ref, grp("b_z"))
    put(xbc_ref, grp("b_xbc"))
    put(p_ref, grp("c_c") * grp("c_h"))
    put(gb_ref, grp("c_b"))
    put(ql_ref, _rms(grp("d_q")) * gq_ref[...])
    kv = _rms(grp("d_kv")) * gkv_ref[...]
    put(cbf_ref, kv)
    if maybe_ct_ref:
        maybe_ct_ref[0][0] = kv.T.astype(BF16)
    kvw = kv.shape[-1]
    rows_ref[:, :, 0:kvw] = kv.reshape(bb, tl, kvw)
    rows_ref[:, :, kvw:kvw + ROPE_DIM] = grp("kr0")[:, 0:ROPE_DIM].reshape(bb, tl, ROPE_DIM)
    put(krp_ref, grp("krp"))
    put(dt_ref, grp("dt"))


def _proj(h, sh, sc, g, w, ggm, gq, gkv, layout):
    b, l, d = h.shape
    bb, tl = _tok_blocks(b, l)
    bd, q_lora, kv_lora = ggm.shape[1], gq.shape[1], gkv.shape[1]
    tok = lambda w_: pl.BlockSpec((bb, tl, w_), lambda i, j: (i, j, 0))
    mod = pl.BlockSpec((bb, 1, d), lambda i, j: (i, 0, 0))
    outs = [(bd, F32), (bd, F32), (bd, F32), (2 * bd, F32), (bd, F32), (bd, F32), (q_lora, BF16),
            (kv_lora + ROPE_DIM, F32), (kv_lora, BF16), (LANES, F32), (LANES, F32)]
    out_specs = [tok(w_) for w_, _ in outs]
    out_shape = [jax.ShapeDtypeStruct((b, l, w_), dt) for w_, dt in outs]
    if bb == 1 and tl % LANES == 0:
        out_specs.append(pl.BlockSpec((1, kv_lora, tl), lambda i, j: (i, 0, j)))
        out_shape.append(jax.ShapeDtypeStruct((b, kv_lora, l), BF16))
    return pl.pallas_call(
        functools.partial(_proj_kernel, layout=layout),
        grid=(b // bb, l // tl),
        in_specs=[tok(d), mod, mod, _resident((1, d)), _resident(w.shape),
                  _resident(ggm.shape), _resident(gq.shape), _resident(gkv.shape)],
        out_specs=out_specs,
        out_shape=out_shape,
        compiler_params=_cparams(("parallel", "parallel")),
        name="proj",
    )(h, sh, sc, g, w, ggm, gq, gkv)


def _gmlp_kernel(u_ref, v_ref, w_ref, bt_ref, o_ref, *, chunk):
    bb, tl, bd = u_ref.shape
    hd = bd // GM_HEADS
    row = lax.broadcasted_iota(jnp.int32, (chunk, chunk), 0)
    col = lax.broadcasted_iota(jnp.int32, (chunk, chunk), 1)
    for hh in range(GM_HEADS):
        wm = jnp.where(col <= row, w_ref[hh, 0:chunk, 0:chunk], 0.0)
        bias = bt_ref[0:chunk, hh:hh + 1]
        cols = slice(hh * hd, (hh + 1) * hd)
        if chunk == GM_CHUNK:
            wmb = wm.astype(BF16)
            for bi in range(bb):
                for ci in range(tl // chunk):
                    rows = slice(ci * chunk, (ci + 1) * chunk)
                    mixed = _dot(wmb, v_ref[bi, rows, cols].astype(BF16)) + bias
                    o_ref[bi, rows, cols] = u_ref[bi, rows, cols] * mixed
        else:
            v = v_ref[:, :, cols]
            mixed = jnp.zeros((bb, chunk, hd), F32) + bias
            for j in range(chunk):
                mixed = mixed + wm[:, j:j + 1] * v[:, j:j + 1, :]
            o_ref[:, :, cols] = u_ref[:, :, cols] * mixed


def _gmlp(u, v, w_s, b_t):
    b, l, bd = u.shape
    bb, tl = _tok_blocks(b, l)
    chunk = min(l, GM_CHUNK)
    assert chunk == GM_CHUNK or chunk == tl
    tok = pl.BlockSpec((bb, tl, bd), lambda i, j: (i, j, 0))
    return pl.pallas_call(
        functools.partial(_gmlp_kernel, chunk=chunk),
        grid=(b // bb, l // tl),
        in_specs=[tok, tok, _resident(w_s.shape), _resident(b_t.shape)],
        out_specs=tok,
        out_shape=jax.ShapeDtypeStruct(u.shape, F32),
        compiler_params=_cparams(("parallel", "parallel")),
        name="gmlp",
    )(u, v, w_s, b_t)


def _sconv_kernel(p_ref, gb_ref, buf_ref, w_ref, o_ref, hist_ref):
    bb, tl, c = p_ref.shape
    k = w_ref.shape[0]

    @pl.when(pl.program_id(1) == 0)
    def _():
        hist_ref[:, 0:8, :] = jnp.zeros((bb, 8, c), F32)
        hist_ref[:, 8 - (k - 1):8, :] = buf_ref[...]

    @pl.when(pl.program_id(1) > 0)
    def _():
        hist_ref[:, 0:8, :] = hist_ref[:, tl:tl + 8, :]

    hist_ref[:, 8:8 + tl, :] = p_ref[...]
    acc = jnp.zeros((bb, tl, c), F32)
    for i in range(k):
        s = 8 - (k - 1) + i
        acc = acc + w_ref[i:i + 1, :] * hist_ref[:, s:s + tl, :]
    o_ref[...] = gb_ref[...] * acc


def _sconv(p, gb, buf, w):
    b, l, c = p.shape
    bb, tl = _tok_blocks(b, l)
    tok = pl.BlockSpec((bb, tl, c), lambda i, j: (i, j, 0))
    return pl.pallas_call(
        _sconv_kernel,
        grid=(b // bb, l // tl),
        in_specs=[tok, tok, pl.BlockSpec((bb, buf.shape[1], c), lambda i, j: (i, 0, 0)), _resident(w.shape)],
        out_specs=tok,
        out_shape=jax.ShapeDtypeStruct(p.shape, F32),
        scratch_shapes=[pltpu.VMEM((bb, 8 + tl, c), F32)],
        compiler_params=_cparams(("parallel", "arbitrary")),
        name="sconv",
    )(p, gb, buf, w)


def _softplus(x):
    return jnp.maximum(x, 0.0) + jnp.log1p(jnp.exp(-jnp.abs(x)))


def _ssm_kernel(xbc_ref, z_ref, dtm_ref, dtt_ref, buf_ref, s0_ref, wc_ref, bc_ref, dtb_ref, dtbt_ref,
                alog_ref, alogt_ref, dsk_ref, gn_ref, y_ref, sout_ref, hist_ref, s_ref, *, valid):
    q = SSM_CHUNK
    c = xbc_ref.shape[-1]
    bd = z_ref.shape[-1]
    k = wc_ref.shape[0]
    ci = pl.program_id(1)

    @pl.when(ci == 0)
    def _():
        hist_ref[0:8, :] = jnp.zeros((8, c), F32)
        hist_ref[8 - (k - 1):8, :] = buf_ref[0]
        s_ref[...] = s0_ref[0]

    @pl.when(ci > 0)
    def _():
        hist_ref[0:8, :] = hist_ref[q:q + 8, :]

    hist_ref[8:8 + q, :] = xbc_ref[0]
    acc = jnp.zeros((q, c), F32) + bc_ref[...]
    for i in range(k):
        s = 8 - (k - 1) + i
        acc = acc + wc_ref[i:i + 1, :] * hist_ref[s:s + q, :]
    xc = _silu(acc)
    xs = xc[:, 0:bd]
    gw = SSM_STATE
    bm = [xc[:, bd + g * gw:bd + (g + 1) * gw].astype(BF16) for g in range(SSM_GROUPS)]
    cm = [xc[:, bd + (SSM_GROUPS + g) * gw:bd + (SSM_GROUPS + g + 1) * gw].astype(BF16) for g in range(SSM_GROUPS)]

    row = lax.broadcasted_iota(jnp.int32, (q, q), 0)
    col = lax.broadcasted_iota(jnp.int32, (q, q), 1)
    causal = col <= row
    dt = _softplus(dtm_ref[0][:, 0:SSM_HEADS] + dtb_ref[...])
    dtt = _softplus(dtt_ref[0] + dtbt_ref[...])
    if valid < q:
        dt = jnp.where(lax.broadcasted_iota(jnp.int32, dt.shape, 0) < valid, dt, 0.0)
        dtt = jnp.where(lax.broadcasted_iota(jnp.int32, dtt.shape, 1) < valid, dtt, 0.0)
    da = dt * (-jnp.exp(alog_ref[...]))
    dat = dtt * (-jnp.exp(alogt_ref[...]))
    cum = jnp.dot(causal.astype(F32), da, preferred_element_type=F32, precision=HIGHEST)
    cumt = jnp.dot(dat, (row <= col).astype(F32), preferred_element_type=F32, precision=HIGHEST)
    last = cum[q - 1:q, :]

    lane = lax.broadcasted_iota(jnp.int32, (q, LANES), 1)
    srow = lax.broadcasted_iota(jnp.int32, (LANES, gw), 0)
    p_dim = bd // SSM_HEADS
    rep = SSM_HEADS // SSM_GROUPS
    for pair in range(SSM_HEADS // 2):
        h0, h1 = 2 * pair, 2 * pair + 1
        g = h0 // rep
        cb = _dot_nt(cm[g], bm[g])
        x_pair = xs[:, pair * LANES:(pair + 1) * LANES]
        xb = x_pair.astype(BF16)
        ys = []
        for hh in (h0, h1):
            seg = cum[:, hh:hh + 1] - cumt[hh:hh + 1, :]
            dec = jnp.where(causal, jnp.exp(jnp.where(causal, seg, 0.0)), 0.0) * dtt[hh:hh + 1, :]
            ys.append(_dot((cb * dec).astype(BF16), xb))
        first = lane < p_dim
        cum_pair = jnp.where(first, cum[:, h0:h0 + 1], cum[:, h1:h1 + 1])
        last_pair = jnp.where(first, last[:, h0:h0 + 1], last[:, h1:h1 + 1])
        dt_pair = jnp.where(first, dt[:, h0:h0 + 1], dt[:, h1:h1 + 1])
        s_in = s_ref[pair]
        y_pair = jnp.where(first, ys[0], ys[1]) + jnp.exp(cum_pair) * _dot_nt(cm[g], s_in.astype(BF16))
        y_ref[0, :, pair * LANES:(pair + 1) * LANES] = y_pair
        xw = x_pair * (jnp.exp(last_pair - cum_pair) * dt_pair)
        cs = _dot(xw.T.astype(BF16), bm[g])
        cd = jnp.where(srow < p_dim, jnp.exp(last[:, h0:h0 + 1]), jnp.exp(last[:, h1:h1 + 1]))
        s_ref[pair] = s_in * cd + cs

    y = (y_ref[0] + dsk_ref[...] * xs) * _silu(z_ref[0])
    gs = bd // SSM_GROUPS
    for g in range(SSM_GROUPS):
        y_ref[0, :, g * gs:(g + 1) * gs] = _rms(y[:, g * gs:(g + 1) * gs]) * gn_ref[:, g * gs:(g + 1) * gs]
    sout_ref[0] = s_ref[...]


def _ssm(xbc, z, dtm, dtt, buf, s0, wc, bc, dtb, dtbt, alog, alogt, dsk, gn, valid):
    b, l, c = xbc.shape
    bd = z.shape[-1]
    q = SSM_CHUNK
    assert l % q == 0 and SSM_HEADS % 2 == 0 and bd // SSM_HEADS * 2 == LANES and (SSM_HEADS // SSM_GROUPS) % 2 == 0
    npair = SSM_HEADS // 2
    tok = lambda w_: pl.BlockSpec((1, q, w_), lambda i, j: (i, j, 0))
    per_b = lambda shp: pl.BlockSpec((1,) + shp, lambda i, j: (i,) + (0,) * len(shp))
    small = [wc, bc, dtb, dtbt, alog, alogt, dsk, gn]
    return pl.pallas_call(
        functools.partial(_ssm_kernel, valid=valid),
        grid=(b, l // q),
        in_specs=[tok(c), tok(bd), tok(LANES), pl.BlockSpec((1, SSM_HEADS, q), lambda i, j: (i, 0, j)),
                  per_b(buf.shape[1:]), per_b(s0.shape[1:])] + [_resident(a.shape) for a in small],
        out_specs=[tok(bd), per_b(s0.shape[1:])],
        out_shape=[jax.ShapeDtypeStruct((b, l, bd), F32), jax.ShapeDtypeStruct(s0.shape, F32)],
        scratch_shapes=[pltpu.VMEM((8 + q, c), F32), pltpu.VMEM((npair, LANES, SSM_STATE), F32)],
        compiler_params=_cparams(("parallel", "arbitrary")),
        name="ssm",
    )(xbc, z, dtm, dtt, buf, s0, *small)


def _rope_swap(x, lane, sign):
    half = ROPE_DIM // 2
    return jnp.where(lane < NOPE_DIM + half, pltpu.roll(x, LANES - half, 1), sign * pltpu.roll(x, half, 1))


def _qk_kernel(*refs, decode):
    if decode:
        (ql_ref, cos_ref, sin_ref, wq_ref, gq_ref, gk_ref, wukt_ref, qt_ref, qa_ref, qb_ref) = refs
    else:
        (ql_ref, cbf_ref, krp_ref, cos_ref, sin_ref, wq_ref, wk_ref, gq_ref, gk_ref, q_ref, k_ref) = refs
    bb, tl, _ = ql_ref.shape
    m = bb * tl
    ql = ql_ref[...].reshape(m, ql_ref.shape[-1])
    lane = lax.broadcasted_iota(jnp.int32, (m, LANES), 1)
    cos = jnp.broadcast_to(cos_ref[...][None], (bb, tl, LANES)).reshape(m, LANES)
    sin = jnp.broadcast_to(sin_ref[...][None], (bb, tl, LANES)).reshape(m, LANES)

    def norm_rope(x, g):
        xn = x * lax.rsqrt(jnp.sum(x * x, axis=-1, keepdims=True) * (1.0 / QK_DIM) + EPS) * g
        return xn * cos + _rope_swap(xn, lane, 1.0) * sin

    if not decode:
        cbf = cbf_ref[...].reshape(m, cbf_ref.shape[-1])
        krp = krp_ref[...].reshape(m, LANES)
    for hh in range(MLA_HEADS):
        qf = norm_rope(_dot(ql, wq_ref[hh]), gq_ref[...]) * QK_SCALE
        if decode:
            gk = gk_ref[...]
            nope = lane < NOPE_DIM
            qn = jnp.where(nope, qf * gk, 0.0).astype(BF16)
            qt_ref[:, hh] = _dot(qn, wukt_ref[hh]).reshape(bb, tl, wukt_ref.shape[-1])
            qa = qf * gk
            qb = _rope_swap(qf, lane, -1.0) * gk
            qa_ref[:, hh] = qa[:, NOPE_DIM:QK_DIM].reshape(bb, tl, ROPE_DIM)
            qb_ref[:, hh] = qb[:, NOPE_DIM:QK_DIM].reshape(bb, tl, ROPE_DIM)
        else:
            q_ref[:, hh] = qf.reshape(bb, tl, LANES).astype(BF16)
            kf = norm_rope(_dot(cbf, wk_ref[hh]) + krp, gk_ref[...])
            k_ref[:, hh] = kf.reshape(bb, tl, LANES).astype(BF16)


def _qk_prompt(ql, cbf, krp, cos, sin, wq, wk, gq, gk):
    b, l, _ = ql.shape
    bb, tl = _tok_blocks(b, l)
    tok = lambda w_: pl.BlockSpec((bb, tl, w_), lambda i, j: (i, j, 0))
    tab = pl.BlockSpec((tl, LANES), lambda i, j: (j, 0))
    head = pl.BlockSpec((bb, MLA_HEADS, tl, LANES), lambda i, j: (i, 0, j, 0))
    shp = jax.ShapeDtypeStruct((b, MLA_HEADS, l, LANES), BF16)
    return pl.pallas_call(
        functools.partial(_qk_kernel, decode=False),
        grid=(b // bb, l // tl),
        in_specs=[tok(ql.shape[-1]), tok(cbf.shape[-1]), tok(LANES), tab, tab,
                  _resident(wq.shape), _resident(wk.shape), _resident(gq.shape), _resident(gk.shape)],
        out_specs=[head, head],
        out_shape=[shp, shp],
        compiler_params=_cparams(("parallel", "parallel")),
        name="qk_prompt",
    )(ql, cbf, krp, cos, sin, wq, wk, gq, gk)


def _q_decode(ql, cos, sin, wq, gq, gk, wukt):
    b, l, _ = ql.shape
    bb, tl = _tok_blocks(b, l)
    assert tl == l
    kv = wukt.shape[-1]
    tok = lambda w_: pl.BlockSpec((bb, tl, w_), lambda i: (i, 0, 0))
    head = lambda w_: pl.BlockSpec((bb, MLA_HEADS, tl, w_), lambda i: (i, 0, 0, 0))
    shp = lambda w_: jax.ShapeDtypeStruct((b, MLA_HEADS, l, w_), F32)
    return pl.pallas_call(
        functools.partial(_qk_kernel, decode=True),
        grid=(b // bb,),
        in_specs=[tok(ql.shape[-1]), _resident(cos.shape), _resident(sin.shape),
                  _resident(wq.shape), _resident(gq.shape), _resident(gk.shape), _resident(wukt.shape)],
        out_specs=[head(kv), head(ROPE_DIM), head(ROPE_DIM)],
        out_shape=[shp(kv), shp(ROPE_DIM), shp(ROPE_DIM)],
        compiler_params=_cparams(("parallel",)),
        name="q_decode",
    )(ql, cos, sin, wq, gq, gk, wukt)


def _flash_kernel(qi_ref, ki_ref, q_ref, k_ref, ct_ref, wuvt_ref, o_ref, m_ref, l_ref, acc_ref, ot_ref):
    tq = q_ref.shape[2]
    tk = k_ref.shape[2]
    step_i = pl.program_id(1)
    qi, ki = qi_ref[step_i], ki_ref[step_i]

    @pl.when(ki == 0)
    def _():
        m_ref[...] = jnp.full(m_ref.shape, -jnp.inf, F32)
        l_ref[...] = jnp.zeros(l_ref.shape, F32)
        acc_ref[...] = jnp.zeros(acc_ref.shape, F32)

    def step(masked):
        ct = ct_ref[0]
        if masked:
            keep = (ki * tk + lax.broadcasted_iota(jnp.int32, (tk, tq), 0)
                    <= qi * tq + lax.broadcasted_iota(jnp.int32, (tk, tq), 1))
        for hh in range(MLA_HEADS):
            st = _dot_nt(k_ref[0, hh], q_ref[0, hh])
            if masked:
                st = jnp.where(keep, st, -jnp.inf)
            m_prev = m_ref[hh:hh + 1, :]
            m_new = jnp.maximum(m_prev, jnp.max(st, axis=0, keepdims=True))
            alpha = jnp.exp(m_prev - m_new)
            p = jnp.exp(st - m_new)
            l_ref[hh:hh + 1, :] = alpha * l_ref[hh:hh + 1, :] + jnp.sum(p, axis=0, keepdims=True)
            acc_ref[hh] = alpha * acc_ref[hh] + _dot(ct, p.astype(BF16))
            m_ref[hh:hh + 1, :] = m_new

    straddles = (ki + 1) * tk - 1 > qi * tq

    @pl.when(jnp.logical_not(straddles))
    def _():
        step(False)

    @pl.when(straddles)
    def _():
        step(True)

    @pl.when((ki + 1) * tk >= (qi + 1) * tq)
    def _():
        vd = wuvt_ref.shape[1]
        for hh in range(MLA_HEADS):
            ctx_t = (acc_ref[hh] / l_ref[hh:hh + 1, :]).astype(BF16)
            ot_ref[hh * vd:(hh + 1) * vd, :] = _dot(wuvt_ref[hh], ctx_t)
        o_ref[0] = ot_ref[...].T


def _flash(q, k, ct, wuvt):
    b, h, l, _ = q.shape
    t = min(l, ATTN_BLOCK)
    assert l % t == 0
    kv = ct.shape[1]
    vd = wuvt.shape[1]
    n = l // t
    pairs = [(qi, ki) for qi in range(n) for ki in range(qi + 1)]
    qi_of = jnp.asarray([p[0] for p in pairs], jnp.int32)
    ki_of = jnp.asarray([p[1] for p in pairs], jnp.int32)
    grid_spec = pltpu.PrefetchScalarGridSpec(
        num_scalar_prefetch=2,
        grid=(b, len(pairs)),
        in_specs=[pl.BlockSpec((1, h, t, LANES), lambda i, s, qo, ko: (i, 0, qo[s], 0)),
                  pl.BlockSpec((1, h, t, LANES), lambda i, s, qo, ko: (i, 0, ko[s], 0)),
                  pl.BlockSpec((1, kv, t), lambda i, s, qo, ko: (i, 0, ko[s])),
                  _resident(wuvt.shape)],
        out_specs=pl.BlockSpec((1, t, h * vd), lambda i, s, qo, ko: (i, qo[s], 0)),
        scratch_shapes=[pltpu.VMEM((h, t), F32), pltpu.VMEM((h, t), F32), pltpu.VMEM((h, kv, t), F32),
                        pltpu.VMEM((h * vd, t), F32)],
    )
    return pl.pallas_call(
        _flash_kernel,
        grid_spec=grid_spec,
        out_shape=jax.ShapeDtypeStruct((b, l, h * vd), F32),
        compiler_params=_cparams(("parallel", "arbitrary")),
        name="flash",
    )(qi_of, ki_of, q, k, ct, wuvt)


def _decode_kernel(pt_ref, qt_ref, qab_ref, tab_ref, tabn_ref, new_ref, wukt_ref, wuv_ref, *rest, pages, group):
    page_refs = rest[:pages]
    o_ref, lhs_ref, ctb_ref, s_ref, m_ref, l_ref, acc_ref = rest[pages:]
    j = pl.program_id(1)
    nrow = qt_ref.shape[1]
    lq = nrow // MLA_HEADS
    kvw = qt_ref.shape[2]
    nkt = wukt_ref.shape[0]
    psz = page_refs[0].shape[1]

    @pl.when(j == 0)
    def _():
        m_ref[...] = jnp.full(m_ref.shape, -jnp.inf, F32)
        l_ref[...] = jnp.zeros(l_ref.shape, F32)
        acc_ref[...] = jnp.zeros(acc_ref.shape, F32)
        lhs_ref[0:nkt, :] = wukt_ref[...]
        lhs_ref[nkt:nkt + nrow, :] = qt_ref[0].astype(BF16)

    qab = qab_ref[0].astype(BF16)

    def scores(ct, krt, cct, snt):
        nk = ct.shape[1]
        ctb = ct.astype(BF16)
        a = _dot(lhs_ref[...], ctb)
        kt = a[0:nkt]
        ss = jnp.sum((kt * kt).reshape(MLA_HEADS, nkt // MLA_HEADS, nk), axis=1)
        ss = ss + jnp.sum(krt * krt, axis=0, keepdims=True)
        inv = lax.rsqrt(ss * (1.0 / QK_DIM) + EPS)
        feats = jnp.concatenate([krt * cct, krt * snt], axis=0).astype(BF16)
        s = a[nkt:nkt + nrow] + _dot(qab, feats)
        return (s.reshape(MLA_HEADS, lq, nk) * inv[:, None, :]).reshape(nrow, nk), ctb

    def update(s, ctb):
        m_prev = m_ref[...]
        m_new = jnp.maximum(m_prev, jnp.max(s, axis=-1, keepdims=True))
        alpha = jnp.exp(m_prev - m_new)
        p = jnp.exp(s - m_new)
        l_ref[...] = alpha * l_ref[...] + jnp.sum(p, axis=-1, keepdims=True)
        acc_ref[...] = alpha * acc_ref[...] + _dot_nt(p.astype(BF16), ctb)
        m_ref[...] = m_new

    for g0 in range(0, pages, group):
        refs = page_refs[g0:g0 + group]
        cat = lambda parts: parts[0] if len(parts) == 1 else jnp.concatenate(parts, axis=1)
        lanes = slice(g0 * psz, (g0 + group) * psz)
        s, ctb = scores(cat([r[0:kvw, :] for r in refs]), cat([r[kvw:kvw + ROPE_DIM, :] for r in refs]),
                        tab_ref[0, :, lanes], tab_ref[1, :, lanes])
        s_ref[:, lanes] = s
        ctb_ref[:, lanes] = ctb
    update(s_ref[...], ctb_ref[...])

    @pl.when(j == pl.num_programs(1) - 1)
    def _():
        nk = new_ref.shape[2]
        tq = lax.broadcasted_iota(jnp.int32, (nrow, nk), 0) % lq
        tk = lax.broadcasted_iota(jnp.int32, (nrow, nk), 1)
        s, ctb = scores(new_ref[0, 0:kvw, :], new_ref[0, kvw:kvw + ROPE_DIM, :], tabn_ref[0], tabn_ref[1])
        update(jnp.where(tk <= tq, s, -jnp.inf), ctb)
        ctx = (acc_ref[...] / l_ref[...]).astype(BF16)
        vd = wuv_ref.shape[-1]
        for hh in range(MLA_HEADS):
            o_ref[0, :, hh * vd:(hh + 1) * vd] = _dot(ctx[hh * lq:(hh + 1) * lq, :], wuv_ref[hh])


def _decode(page_table, qt, qab, tab, tabn, rows_new_t, wukt, wuv, pool_t, layer):
    bs, nrow, kvw = qt.shape
    n_pages = page_table.shape[1]
    cw, psz = pool_t.shape[2], pool_t.shape[3]
    pages = math.gcd(n_pages, DECODE_PAGES)
    group = math.gcd(pages, DECODE_GROUP)
    lq = nrow // MLA_HEADS
    vd = wuv.shape[-1]
    per_s = lambda shp: pl.BlockSpec((1,) + shp, lambda s, j, pt: (s,) + (0,) * len(shp))

    def page_spec(pi):
        return pl.BlockSpec((None, None, cw, psz), lambda s, j, pt: (layer, pt[s, j * pages + pi], 0, 0))

    grid_spec = pltpu.PrefetchScalarGridSpec(
        num_scalar_prefetch=1,
        grid=(bs, n_pages // pages),
        in_specs=[per_s((nrow, kvw)), per_s((nrow, 2 * ROPE_DIM)),
                  pl.BlockSpec((2, ROPE_DIM, pages * psz), lambda s, j, pt: (0, 0, j)),
                  _resident(tabn.shape), per_s(rows_new_t.shape[1:]),
                  _resident(wukt.shape), _resident(wuv.shape)] + [page_spec(pi) for pi in range(pages)],
        out_specs=per_s((lq, MLA_HEADS * vd)),
        scratch_shapes=[pltpu.VMEM((wukt.shape[0] + nrow, kvw), BF16),
                        pltpu.VMEM((kvw, pages * psz), BF16), pltpu.VMEM((nrow, pages * psz), F32),
                        pltpu.VMEM((nrow, 1), F32), pltpu.VMEM((nrow, 1), F32), pltpu.VMEM((nrow, kvw), F32)],
    )
    return pl.pallas_call(
        functools.partial(_decode_kernel, pages=pages, group=group),
        grid_spec=grid_spec,
        out_shape=jax.ShapeDtypeStruct((bs, lq, MLA_HEADS * vd), F32),
        compiler_params=_cparams(("parallel", "arbitrary")),
        name="decode",
    )(page_table, qt, qab, tab, tabn, rows_new_t, wukt, wuv, *([pool_t] * pages))


def _merge_kernel(h_ref, sh_ref, sc_ref, gt_ref, g_ref, ya_ref, yb_ref, yc_ref, yd_ref,
                  wg_ref, bg_ref, wb_ref, wo_ref, o_ref):
    bb, tl, d = h_ref.shape
    m = bb * tl
    h = h_ref[...]
    n = (_rms(h) * g_ref[...]) * (1.0 + sc_ref[...]) + sh_ref[...]
    n = n.reshape(m, d).astype(BF16)
    merged = None
    for r, y_ref in enumerate((ya_ref, yb_ref, yc_ref, yd_ref)):
        gate = jax.nn.sigmoid(_dot(n, wg_ref[r]) + bg_ref[r])
        term = gate * _dot(y_ref[...].reshape(m, y_ref.shape[-1]).astype(BF16), wb_ref[r])
        merged = term if merged is None else merged + term
    out = _dot(merged.astype(BF16), wo_ref[...])
    o_ref[...] = h + gt_ref[...] * out.reshape(bb, tl, d)


def _merge(h, sh, sc, gt, g, ys, wg, bg, wb, wo):
    b, l, d = h.shape
    bb, tl = _tok_blocks(b, l)
    tok = lambda w_: pl.BlockSpec((bb, tl, w_), lambda i, j: (i, j, 0))
    mod = pl.BlockSpec((bb, 1, d), lambda i, j: (i, 0, 0))
    return pl.pallas_call(
        _merge_kernel,
        grid=(b // bb, l // tl),
        in_specs=[tok(d), mod, mod, mod, _resident((1, d))] + [tok(y.shape[-1]) for y in ys]
                 + [_resident(wg.shape), _resident(bg.shape), _resident(wb.shape), _resident(wo.shape)],
        out_specs=tok(d),
        out_shape=jax.ShapeDtypeStruct(h.shape, F32),
        compiler_params=_cparams(("parallel", "parallel")),
        name="merge",
    )(h, sh, sc, gt, g, *ys, wg, bg, wb, wo)


def _rope_tables(pos):
    half = ROPE_DIM // 2
    inv = ROPE_THETA ** (-jnp.arange(half, dtype=F32) / half)
    ang = pos.astype(F32)[:, None] * inv
    return jnp.cos(ang), jnp.sin(ang)


def _lane_tables(pos):
    cos, sin = _rope_tables(pos)
    n = pos.shape[0]
    cos_t = jnp.concatenate([jnp.ones((n, NOPE_DIM), F32), cos, cos, jnp.zeros((n, LANES - QK_DIM), F32)], axis=1)
    sin_t = jnp.concatenate([jnp.zeros((n, NOPE_DIM), F32), -sin, sin, jnp.zeros((n, LANES - QK_DIM), F32)], axis=1)
    return cos_t, sin_t


def _pad_lanes(x, before, total):
    pad = [(0, 0)] * (x.ndim - 1) + [(before, total - before - x.shape[-1])]
    return jnp.pad(x, pad)


def _layer_weights(lw):
    d = lw["w_in"].shape[0]
    bd = lw["g_gm_v"].shape[0]
    q_lora, kv_lora = lw["g_q_lat"].shape[0], lw["g_kv_lat"].shape[0]
    layout, _ = _proj_layout(bd, q_lora, kv_lora)
    w = {}
    w["layout"] = layout
    w["w_ada"] = lw["w_ada"].astype(BF16)
    w["b_ada"] = lw["b_ada"][None, :]
    w["g_norm"] = [lw["g_norm"][i][None, :] for i in range(3)]
    w["w_ffn_in"] = [lw["w_ffn_in"][i].astype(BF16) for i in range(2)]
    w["w_ffn_out"] = [lw["w_ffn_out"][i].astype(BF16) for i in range(2)]
    w["w_proj"] = _proj_weight(lw["w_in"], bd, q_lora, kv_lora)
    w["g_gm_v"] = lw["g_gm_v"][None, :]
    w["g_q_lat"] = lw["g_q_lat"][None, :]
    w["g_kv_lat"] = lw["g_kv_lat"][None, :]
    w["w_spatial"] = lw["w_spatial"]
    w["b_spatial_t"] = lw["b_spatial"].T
    w["w_ssm_conv"] = lw["w_ssm_conv"]
    w["b_ssm_conv"] = lw["b_ssm_conv"][None, :]
    w["dt_bias"] = lw["dt_bias"][None, :]
    w["dt_bias_t"] = lw["dt_bias"][:, None]
    w["a_log"] = lw["a_log"][None, :]
    w["a_log_t"] = lw["a_log"][:, None]
    w["d_skip"] = jnp.repeat(lw["d_skip"], bd // SSM_HEADS)[None, :]
    w["g_ssm_norm"] = lw["g_ssm_norm"][None, :]
    w["w_sc_conv"] = lw["w_sc_conv"]
    w["wq"] = _pad_lanes(jnp.moveaxis(lw["w_uq"], 1, 0), 0, LANES).astype(BF16)
    wuk = jnp.moveaxis(lw["w_uk"], 1, 0)
    w["wk"] = _pad_lanes(wuk, 0, LANES).astype(BF16)
    wukt = jnp.swapaxes(wuk, 1, 2)
    w["wukt_pad"] = jnp.pad(wukt, ((0, 0), (0, LANES - NOPE_DIM), (0, 0))).astype(BF16)
    w["wukt_flat"] = wukt.reshape(MLA_HEADS * NOPE_DIM, kv_lora).astype(BF16)
    w["wuv"] = jnp.moveaxis(lw["w_uv"], 1, 0).astype(BF16)
    w["wuvt"] = jnp.swapaxes(w["wuv"], 1, 2)
    w["gq"] = _pad_lanes(lw["g_qk"][0][None, :], 0, LANES)
    w["gk"] = _pad_lanes(lw["g_qk"][1][None, :], 0, LANES)
    w["w_gate"] = lw["w_gate"].astype(BF16)
    w["b_gate"] = lw["b_gate"][:, None, :]
    w["w_branch_out"] = lw["w_branch_out"].astype(BF16)
    w["w_out"] = lw["w_out"].astype(BF16)
    return w


def _trunk_layer(x, mod, w, ssm_buf, ssm_s0, sc_buf, attn_fn):
    b, l, d = x.shape
    sh1, sc1, gt1, sh2, sc2, gt2, sh3, sc3, gt3 = [m[:, None, :] for m in jnp.split(mod, N_MOD, axis=-1)]
    h = _ffn(x, sh1, sc1, gt1, w["g_norm"][0], w["w_ffn_in"][0], w["w_ffn_out"][0])
    (a_u, a_v, b_z, b_xbc, c_p, c_b, ql, rows, cbf, krp, dtm, *ct) = _proj(
        h, sh2, sc2, w["g_norm"][1], w["w_proj"], w["g_gm_v"], w["g_q_lat"], w["g_kv_lat"], w["layout"])
    y_a = _gmlp(a_u, a_v, w["w_spatial"], w["b_spatial_t"])
    lp = -(-l // SSM_CHUNK) * SSM_CHUNK
    padl = lambda t: jnp.pad(t, ((0, 0), (0, lp - l), (0, 0)))
    dtt = jnp.swapaxes(padl(dtm)[:, :, 0:SSM_HEADS], 1, 2)
    npair = SSM_HEADS // 2
    s0 = ssm_s0.reshape(b, npair, LANES, SSM_STATE)
    y_b, s_new = _ssm(padl(b_xbc), padl(b_z), padl(dtm), dtt, ssm_buf, s0, w["w_ssm_conv"], w["b_ssm_conv"],
                      w["dt_bias"], w["dt_bias_t"], w["a_log"], w["a_log_t"], w["d_skip"], w["g_ssm_norm"],
                      valid=min(l, SSM_CHUNK) if lp != l else SSM_CHUNK)
    y_b = y_b[:, :l]
    s_new = s_new.reshape(ssm_s0.shape)
    ssm_buf_new = jnp.concatenate([ssm_buf, b_xbc], axis=1)[:, -(SSM_CONV - 1):]
    y_c = _sconv(c_p, c_b, sc_buf, w["w_sc_conv"])
    sc_buf_new = jnp.concatenate([sc_buf, c_p], axis=1)[:, -(SC_WIDTH - 1):]
    y_d = attn_fn(ql, rows, cbf, krp, *ct)
    h = _merge(h, sh2, sc2, gt2, w["g_norm"][1], (y_a, y_b, y_c, y_d), w["w_gate"], w["b_gate"],
               w["w_branch_out"], w["w_out"])
    h = _ffn(h, sh3, sc3, gt3, w["g_norm"][2], w["w_ffn_in"][1], w["w_ffn_out"][1])
    return h, rows, ssm_buf_new, s_new, sc_buf_new, a_v


def kernel(x_prompt, x_sample, c_prompt, c_sample, cache_mla, page_table, state_ssm, state_ssm_conv,
           state_short_conv, w_ada, b_ada, g_norm, w_ffn_in, w_ffn_out, w_in, g_gm_v, w_spatial, b_spatial,
           w_ssm_conv, b_ssm_conv, dt_bias, a_log, d_skip, g_ssm_norm, w_sc_conv, g_q_lat, w_uq, g_kv_lat,
           w_uk, w_uv, g_qk, w_branch_out, w_gate, b_gate, w_out):
    params = dict(w_ada=w_ada, b_ada=b_ada, g_norm=g_norm, w_ffn_in=w_ffn_in, w_ffn_out=w_ffn_out, w_in=w_in,
                  g_gm_v=g_gm_v, w_spatial=w_spatial, b_spatial=b_spatial, w_ssm_conv=w_ssm_conv,
                  b_ssm_conv=b_ssm_conv, dt_bias=dt_bias, a_log=a_log, d_skip=d_skip, g_ssm_norm=g_ssm_norm,
                  w_sc_conv=w_sc_conv, g_q_lat=g_q_lat, w_uq=w_uq, g_kv_lat=g_kv_lat, w_uk=w_uk, w_uv=w_uv,
                  g_qk=g_qk, w_branch_out=w_branch_out, w_gate=w_gate, b_gate=b_gate, w_out=w_out)
    depth = w_ada.shape[0]
    bp, lp, d = x_prompt.shape
    bs, ls, _ = x_sample.shape
    page = cache_mla.shape[2]
    past = page_table.shape[1] * page
    bd = g_gm_v.shape[1]
    conv_ch = w_ssm_conv.shape[-1]

    pos_p = jnp.arange(lp, dtype=jnp.int32)
    pos_s = past + jnp.arange(ls, dtype=jnp.int32)
    cos_p, sin_p = _lane_tables(pos_p)
    cos_s, sin_s = _lane_tables(pos_s)
    def key_tables(pos, width):
        cos, sin = _rope_tables(pos)
        t = jnp.stack([jnp.concatenate([cos, cos], axis=1).T, jnp.concatenate([sin, sin], axis=1).T])
        return jnp.pad(t, ((0, 0), (0, 0), (0, width - pos.shape[0])))

    tab_k = key_tables(jnp.arange(past, dtype=jnp.int32), past)
    new_w = -(-ls // LANES) * LANES
    tab_n = key_tables(pos_s, new_w)
    pool_t = jnp.swapaxes(cache_mla, 2, 3)
    c_all = jnp.concatenate([c_prompt, c_sample], axis=0)

    yp, ys = x_prompt, x_sample
    outs = [[] for _ in range(9)]
    for layer in range(depth):
        w = _layer_weights({k: v[layer] for k, v in params.items()})
        mod = _ada(c_all, w["w_ada"], w["b_ada"])

        def attn_prompt(ql, rows, cbf, krp, ct, w=w):
            q, k = _qk_prompt(ql, cbf, krp, cos_p, sin_p, w["wq"], w["wk"], w["gq"], w["gk"])
            return _flash(q, k, ct, w["wuvt"])

        def attn_sample(ql, rows, cbf, krp, *unused, w=w, layer=layer):
            qt, qa, qb = _q_decode(ql, cos_s, sin_s, w["wq"], w["gq"], w["gk"], w["wukt_pad"])
            flat = lambda t: t.reshape(bs, MLA_HEADS * ls, t.shape[-1])
            rows_t = jnp.pad(jnp.swapaxes(rows, 1, 2), ((0, 0), (0, 0), (0, new_w - ls)))
            return _decode(page_table, flat(qt), flat(jnp.concatenate([qa, qb], axis=-1)), tab_k, tab_n, rows_t,
                           w["wukt_flat"], w["wuv"], pool_t, layer)

        yp, r, cb, s, scb, _ = _trunk_layer(
            yp, mod[:bp], w,
            jnp.zeros((bp, SSM_CONV - 1, conv_ch), F32),
            jnp.zeros((bp, SSM_HEADS, bd // SSM_HEADS, SSM_STATE), F32),
            jnp.zeros((bp, SC_WIDTH - 1, bd), F32), attn_prompt)
        for lst, v in zip((outs[0], outs[2], outs[4], outs[6]), (r, s, cb, scb)):
            lst.append(v)
        ys, r, cb, s, scb, v = _trunk_layer(
            ys, mod[bp:], w, state_ssm_conv[layer], state_ssm[layer], state_short_conv[layer], attn_sample)
        for lst, val in zip((outs[1], outs[3], outs[5], outs[7], outs[8]), (r, s, cb, scb, v)):
            lst.append(val)
    st = jnp.stack
    return (yp, ys, st(outs[0]), st(outs[1]), st(outs[2]), st(outs[3]), st(outs[4]), st(outs[5]),
            st(outs[6]), st(outs[7]), st(outs[8]))
```

```python
import functools
import math

import jax
import jax.numpy as jnp
from jax import lax
from jax.experimental import pallas as pl
from jax.experimental.pallas import tpu as pltpu

F32 = jnp.float32
BF16 = jnp.bfloat16
HIGHEST = lax.Precision.HIGHEST

EPS = 1e-6
LANES = 128
N_MOD = 9
GM_CHUNK = 128
GM_HEADS = 4
SSM_HEADS = 8
SSM_GROUPS = 2
SSM_STATE = 128
SSM_CHUNK = 128
SSM_CONV = 4
SC_WIDTH = 3
MLA_HEADS = 8
NOPE_DIM = 64
ROPE_DIM = 32
QK_DIM = NOPE_DIM + ROPE_DIM
ROPE_THETA = 10000.0
QK_SCALE = 1.0 / math.sqrt(QK_DIM)
LOG2E = math.log2(math.e)
ROW_BLOCK = 512
ATTN_BLOCK = 512
DECODE_PAGES = 16
DECODE_GROUP = 4
VMEM_LIMIT = 56 * 1024 * 1024


def _cparams(sem):
    return pltpu.CompilerParams(dimension_semantics=sem, vmem_limit_bytes=VMEM_LIMIT)


def _resident(shape):
    nd = len(shape)
    return pl.BlockSpec(shape, lambda *_: (0,) * nd, pipeline_mode=pl.Buffered(1))


def _tok_blocks(b, l):
    tl = min(l, ROW_BLOCK)
    bb = min(b, max(1, ROW_BLOCK // tl))
    assert l % tl == 0 and b % bb == 0 and tl % 8 == 0
    return bb, tl


def _silu(x):
    return x * jax.nn.sigmoid(x)


def _rms(x):
    return x * lax.rsqrt(jnp.mean(x * x, axis=-1, keepdims=True) + EPS)


def _dot(a, b):
    return jnp.dot(a, b, preferred_element_type=F32)


def _dot_nt(a, b, precision=None):
    return lax.dot_general(a, b, (((1,), (1,)), ((), ())), preferred_element_type=F32, precision=precision)


def _ada_kernel(c_ref, w_ref, b_ref, o_ref):
    o_ref[...] = _dot(_silu(c_ref[...]).astype(BF16), w_ref[...]) + b_ref[...]


def _ada(c, w, b):
    bt, d = c.shape
    n = w.shape[1]
    return pl.pallas_call(
        _ada_kernel,
        grid=(n // d,),
        in_specs=[pl.BlockSpec((bt, d), lambda j: (0, 0)),
                  pl.BlockSpec((d, d), lambda j: (0, j)),
                  pl.BlockSpec((1, d), lambda j: (0, j))],
        out_specs=pl.BlockSpec((bt, d), lambda j: (0, j)),
        out_shape=jax.ShapeDtypeStruct((bt, n), F32),
        compiler_params=_cparams(("arbitrary",)),
        name="ada",
    )(c, w, b)


def _ffn_kernel(x_ref, sh_ref, sc_ref, gt_ref, g_ref, win_ref, wout_ref, o_ref, acc_ref, *, tf):
    bb, tl, d = x_ref.shape
    f = wout_ref.shape[0]
    x = x_ref[...]
    xn = (_rms(x) * g_ref[...]) * (1.0 + sc_ref[...]) + sh_ref[...]
    xn = xn.reshape(bb * tl, d).astype(BF16)
    for i in range(f // tf):
        g = _dot(xn, win_ref[:, i * tf:(i + 1) * tf])
        u = _dot(xn, win_ref[:, f + i * tf:f + (i + 1) * tf])
        a = (_silu(g) * u).astype(BF16)
        part = _dot(a, wout_ref[i * tf:(i + 1) * tf, :])
        if i == 0:
            acc_ref[...] = part
        else:
            acc_ref[...] += part
    o_ref[...] = x + 0.5 * gt_ref[...] * acc_ref[...].reshape(bb, tl, d)


def _ffn(x, sh, sc, gt, g, w_in, w_out):
    b, l, d = x.shape
    bb, tl = _tok_blocks(b, l)
    f = w_out.shape[0]
    tf = 256 if f % 256 == 0 else LANES
    tok = pl.BlockSpec((bb, tl, d), lambda i, j: (i, j, 0))
    mod = pl.BlockSpec((bb, 1, d), lambda i, j: (i, 0, 0))
    return pl.pallas_call(
        functools.partial(_ffn_kernel, tf=tf),
        grid=(b // bb, l // tl),
        in_specs=[tok, mod, mod, mod, _resident((1, d)), _resident(w_in.shape), _resident(w_out.shape)],
        out_specs=tok,
        out_shape=jax.ShapeDtypeStruct(x.shape, F32),
        scratch_shapes=[pltpu.VMEM((bb * tl, d), F32)],
        compiler_params=_cparams(("parallel", "parallel")),
        name="ffn",
    )(x, sh, sc, gt, g, w_in, w_out)


_PROJ_GROUPS = ("a_u", "a_v", "b_z", "b_xbc", "c_h", "c_b", "c_c", "d_q", "d_kv", "kr0", "krp", "dt")


def _proj_layout(bd, q_lora, kv_lora):
    widths = dict(a_u=bd, a_v=bd, b_z=bd, b_xbc=2 * bd, c_h=bd, c_b=bd, c_c=bd, d_q=q_lora, d_kv=kv_lora,
                  kr0=LANES, krp=LANES, dt=LANES)
    off, layout = 0, {}
    for name in _PROJ_GROUPS:
        layout[name] = (off, widths[name])
        off += widths[name]
    return layout, off


def _proj_weight(w_in, bd, q_lora, kv_lora):
    d = w_in.shape[0]
    sizes = (bd, bd, bd, 2 * bd, SSM_HEADS, bd, bd, bd, q_lora, kv_lora, ROPE_DIM)
    a_u, a_v, b_z, b_xbc, b_dt, c_h, c_b, c_c, d_q, d_kv, d_kr = jnp.split(w_in, _cumsum(sizes)[:-1], axis=1)
    z = lambda n: jnp.zeros((d, n), w_in.dtype)
    kr0 = jnp.concatenate([d_kr, z(LANES - ROPE_DIM)], axis=1)
    krp = jnp.concatenate([z(NOPE_DIM), d_kr, z(LANES - QK_DIM)], axis=1)
    dt = jnp.concatenate([b_dt, z(LANES - SSM_HEADS)], axis=1)
    return jnp.concatenate([a_u, a_v, b_z, b_xbc, c_h, c_b, c_c, d_q, d_kv, kr0, krp, dt], axis=1).astype(BF16)


def _cumsum(sizes):
    out, s = [], 0
    for v in sizes:
        s += v
        out.append(s)
    return out


def _proj_kernel(h_ref, sh_ref, sc_ref, g_ref, w_ref, ggm_ref, gq_ref, gkv_ref,
                 au_ref, av_ref, z_ref, xbc_ref, p_ref, gb_ref, ql_ref, rows_ref, cbf_ref, krp_ref, dt_ref,
                 *maybe_ct_ref, layout):
    bb, tl, d = h_ref.shape
    n = (_rms(h_ref[...]) * g_ref[...]) * (1.0 + sc_ref[...]) + sh_ref[...]
    n = n.reshape(bb * tl, d).astype(BF16)

    def grp(name):
        off, w = layout[name]
        return _dot(n, w_ref[:, off:off + w])

    def put(ref, val):
        ref[...] = val.reshape(bb, tl, val.shape[-1]).astype(ref.dtype)

    put(au_ref, jax.nn.gelu(grp("a_u")))
    put(av_ref, _rms(jax.nn.gelu(grp("a_v"))) * ggm_ref[...])
    put(z_ref, grp("b_z"))
    put(xbc_ref, grp("b_xbc"))
    put(p_ref, grp("c_c") * grp("c_h"))
    put(gb_ref, grp("c_b"))
    put(ql_ref, _rms(grp("d_q")) * gq_ref[...])
    kv = _rms(grp("d_kv")) * gkv_ref[...]
    put(cbf_ref, kv)
    if maybe_ct_ref:
        maybe_ct_ref[0][0] = kv.T.astype(BF16)
    kvw = kv.shape[-1]
    rows_ref[:, :, 0:kvw] = kv.reshape(bb, tl, kvw)
    rows_ref[:, :, kvw:kvw + ROPE_DIM] = grp("kr0")[:, 0:ROPE_DIM].reshape(bb, tl, ROPE_DIM)
    put(krp_ref, grp("krp"))
    put(dt_ref, grp("dt"))


def _proj(h, sh, sc, g, w, ggm, gq, gkv, layout):
    b, l, d = h.shape
    bb, tl = _tok_blocks(b, l)
    bd, q_lora, kv_lora = ggm.shape[1], gq.shape[1], gkv.shape[1]
    tok = lambda w_: pl.BlockSpec((bb, tl, w_), lambda i, j: (i, j, 0))
    mod = pl.BlockSpec((bb, 1, d), lambda i, j: (i, 0, 0))
    outs = [(bd, F32), (bd, F32), (bd, F32), (2 * bd, F32), (bd, F32), (bd, F32), (q_lora, BF16),
            (kv_lora + ROPE_DIM, F32), (kv_lora, BF16), (LANES, F32), (LANES, F32)]
    out_specs = [tok(w_) for w_, _ in outs]
    out_shape = [jax.ShapeDtypeStruct((b, l, w_), dt) for w_, dt in outs]
    if bb == 1 and tl % LANES == 0:
        out_specs.append(pl.BlockSpec((1, kv_lora, tl), lambda i, j: (i, 0, j)))
        out_shape.append(jax.ShapeDtypeStruct((b, kv_lora, l), BF16))
    return pl.pallas_call(
        functools.partial(_proj_kernel, layout=layout),
        grid=(b // bb, l // tl),
        in_specs=[tok(d), mod, mod, _resident((1, d)), _resident(w.shape),
                  _resident(ggm.shape), _resident(gq.shape), _resident(gkv.shape)],
        out_specs=out_specs,
        out_shape=out_shape,
        compiler_params=_cparams(("parallel", "parallel")),
        name="proj",
    )(h, sh, sc, g, w, ggm, gq, gkv)


def _gmlp_kernel(u_ref, v_ref, w_ref, bt_ref, o_ref, *, chunk):
    bb, tl, bd = u_ref.shape
    hd = bd // GM_HEADS
    row = lax.broadcasted_iota(jnp.int32, (chunk, chunk), 0)
    col = lax.broadcasted_iota(jnp.int32, (chunk, chunk), 1)
    for hh in range(GM_HEADS):
        wm = jnp.where(col <= row, w_ref[hh, 0:chunk, 0:chunk], 0.0)
        bias = bt_ref[0:chunk, hh:hh + 1]
        cols = slice(hh * hd, (hh + 1) * hd)
        if chunk == GM_CHUNK:
            wmb = wm.astype(BF16)
            for bi in range(bb):
                for ci in range(tl // chunk):
                    rows = slice(ci * chunk, (ci + 1) * chunk)
                    mixed = _dot(wmb, v_ref[bi, rows, cols].astype(BF16)) + bias
                    o_ref[bi, rows, cols] = u_ref[bi, rows, cols] * mixed
        else:
            v = v_ref[:, :, cols]
            mixed = jnp.zeros((bb, chunk, hd), F32) + bias
            for j in range(chunk):
                mixed = mixed + wm[:, j:j + 1] * v[:, j:j + 1, :]
            o_ref[:, :, cols] = u_ref[:, :, cols] * mixed


def _gmlp(u, v, w_s, b_t):
    b, l, bd = u.shape
    bb, tl = _tok_blocks(b, l)
    chunk = min(l, GM_CHUNK)
    assert chunk == GM_CHUNK or chunk == tl
    tok = pl.BlockSpec((bb, tl, bd), lambda i, j: (i, j, 0))
    return pl.pallas_call(
        functools.partial(_gmlp_kernel, chunk=chunk),
        grid=(b // bb, l // tl),
        in_specs=[tok, tok, _resident(w_s.shape), _resident(b_t.shape)],
        out_specs=tok,
        out_shape=jax.ShapeDtypeStruct(u.shape, F32),
        compiler_params=_cparams(("parallel", "parallel")),
        name="gmlp",
    )(u, v, w_s, b_t)


def _sconv_kernel(p_ref, gb_ref, buf_ref, w_ref, o_ref, hist_ref):
    bb, tl, c = p_ref.shape
    k = w_ref.shape[0]

    @pl.when(pl.program_id(1) == 0)
    def _():
        hist_ref[:, 0:8, :] = jnp.zeros((bb, 8, c), F32)
        hist_ref[:, 8 - (k - 1):8, :] = buf_ref[...]

    @pl.when(pl.program_id(1) > 0)
    def _():
        hist_ref[:, 0:8, :] = hist_ref[:, tl:tl + 8, :]

    hist_ref[:, 8:8 + tl, :] = p_ref[...]
    acc = jnp.zeros((bb, tl, c), F32)
    for i in range(k):
        s = 8 - (k - 1) + i
        acc = acc + w_ref[i:i + 1, :] * hist_ref[:, s:s + tl, :]
    o_ref[...] = gb_ref[...] * acc


def _sconv(p, gb, buf, w):
    b, l, c = p.shape
    bb, tl = _tok_blocks(b, l)
    tok = pl.BlockSpec((bb, tl, c), lambda i, j: (i, j, 0))
    return pl.pallas_call(
        _sconv_kernel,
        grid=(b // bb, l // tl),
        in_specs=[tok, tok, pl.BlockSpec((bb, buf.shape[1], c), lambda i, j: (i, 0, 0)), _resident(w.shape)],
        out_specs=tok,
        out_shape=jax.ShapeDtypeStruct(p.shape, F32),
        scratch_shapes=[pltpu.VMEM((bb, 8 + tl, c), F32)],
        compiler_params=_cparams(("parallel", "arbitrary")),
        name="sconv",
    )(p, gb, buf, w)


def _softplus(x):
    return jnp.maximum(x, 0.0) + jnp.log1p(jnp.exp(-jnp.abs(x)))


def _ssm_kernel(xbc_ref, z_ref, dtm_ref, dtt_ref, buf_ref, s0_ref, wc_ref, bc_ref, dtb_ref, dtbt_ref,
                alog_ref, alogt_ref, dsk_ref, gn_ref, y_ref, sout_ref, hist_ref, s_ref, *, valid):
    q = SSM_CHUNK
    c = xbc_ref.shape[-1]
    bd = z_ref.shape[-1]
    k = wc_ref.shape[0]
    ci = pl.program_id(1)

    @pl.when(ci == 0)
    def _():
        hist_ref[0:8, :] = jnp.zeros((8, c), F32)
        hist_ref[8 - (k - 1):8, :] = buf_ref[0]
        s_ref[...] = s0_ref[0]

    @pl.when(ci > 0)
    def _():
        hist_ref[0:8, :] = hist_ref[q:q + 8, :]

    hist_ref[8:8 + q, :] = xbc_ref[0]
    acc = jnp.zeros((q, c), F32) + bc_ref[...]
    for i in range(k):
        s = 8 - (k - 1) + i
        acc = acc + wc_ref[i:i + 1, :] * hist_ref[s:s + q, :]
    xc = _silu(acc)
    xs = xc[:, 0:bd]
    gw = SSM_STATE
    bm = [xc[:, bd + g * gw:bd + (g + 1) * gw].astype(BF16) for g in range(SSM_GROUPS)]
    cm = [xc[:, bd + (SSM_GROUPS + g) * gw:bd + (SSM_GROUPS + g + 1) * gw].astype(BF16) for g in range(SSM_GROUPS)]

    row = lax.broadcasted_iota(jnp.int32, (q, q), 0)
    col = lax.broadcasted_iota(jnp.int32, (q, q), 1)
    causal = col <= row
    dt = _softplus(dtm_ref[0][:, 0:SSM_HEADS] + dtb_ref[...])
    dtt = _softplus(dtt_ref[0] + dtbt_ref[...])
    if valid < q:
        dt = jnp.where(lax.broadcasted_iota(jnp.int32, dt.shape, 0) < valid, dt, 0.0)
        dtt = jnp.where(lax.broadcasted_iota(jnp.int32, dtt.shape, 1) < valid, dtt, 0.0)
    da = dt * (-jnp.exp(alog_ref[...]))
    dat = dtt * (-jnp.exp(alogt_ref[...]))
    cum = jnp.dot(causal.astype(F32), da, preferred_element_type=F32, precision=HIGHEST)
    cumt = jnp.dot(dat, (row <= col).astype(F32), preferred_element_type=F32, precision=HIGHEST)
    last = cum[q - 1:q, :]

    lane = lax.broadcasted_iota(jnp.int32, (q, LANES), 1)
    srow = lax.broadcasted_iota(jnp.int32, (LANES, gw), 0)
    p_dim = bd // SSM_HEADS
    rep = SSM_HEADS // SSM_GROUPS
    for pair in range(SSM_HEADS // 2):
        h0, h1 = 2 * pair, 2 * pair + 1
        g = h0 // rep
        cb = _dot_nt(cm[g], bm[g])
        x_pair = xs[:, pair * LANES:(pair + 1) * LANES]
        xb = x_pair.astype(BF16)
        ys = []
        for hh in (h0, h1):
            seg = cum[:, hh:hh + 1] - cumt[hh:hh + 1, :]
            dec = jnp.where(causal, jnp.exp(jnp.where(causal, seg, 0.0)), 0.0) * dtt[hh:hh + 1, :]
            ys.append(_dot((cb * dec).astype(BF16), xb))
        first = lane < p_dim
        cum_pair = jnp.where(first, cum[:, h0:h0 + 1], cum[:, h1:h1 + 1])
        last_pair = jnp.where(first, last[:, h0:h0 + 1], last[:, h1:h1 + 1])
        dt_pair = jnp.where(first, dt[:, h0:h0 + 1], dt[:, h1:h1 + 1])
        s_in = s_ref[pair]
        y_pair = jnp.where(first, ys[0], ys[1]) + jnp.exp(cum_pair) * _dot_nt(cm[g], s_in.astype(BF16))
        y_ref[0, :, pair * LANES:(pair + 1) * LANES] = y_pair
        xw = x_pair * (jnp.exp(last_pair - cum_pair) * dt_pair)
        cs = _dot(xw.T.astype(BF16), bm[g])
        cd = jnp.where(srow < p_dim, jnp.exp(last[:, h0:h0 + 1]), jnp.exp(last[:, h1:h1 + 1]))
        s_ref[pair] = s_in * cd + cs

    y = (y_ref[0] + dsk_ref[...] * xs) * _silu(z_ref[0])
    gs = bd // SSM_GROUPS
    for g in range(SSM_GROUPS):
        y_ref[0, :, g * gs:(g + 1) * gs] = _rms(y[:, g * gs:(g + 1) * gs]) * gn_ref[:, g * gs:(g + 1) * gs]
    sout_ref[0] = s_ref[...]


def _ssm(xbc, z, dtm, dtt, buf, s0, wc, bc, dtb, dtbt, alog, alogt, dsk, gn, valid):
    b, l, c = xbc.shape
    bd = z.shape[-1]
    q = SSM_CHUNK
    assert l % q == 0 and SSM_HEADS % 2 == 0 and bd // SSM_HEADS * 2 == LANES and (SSM_HEADS // SSM_GROUPS) % 2 == 0
    npair = SSM_HEADS // 2
    tok = lambda w_: pl.BlockSpec((1, q, w_), lambda i, j: (i, j, 0))
    per_b = lambda shp: pl.BlockSpec((1,) + shp, lambda i, j: (i,) + (0,) * len(shp))
    small = [wc, bc, dtb, dtbt, alog, alogt, dsk, gn]
    return pl.pallas_call(
        functools.partial(_ssm_kernel, valid=valid),
        grid=(b, l // q),
        in_specs=[tok(c), tok(bd), tok(LANES), pl.BlockSpec((1, SSM_HEADS, q), lambda i, j: (i, 0, j)),
                  per_b(buf.shape[1:]), per_b(s0.shape[1:])] + [_resident(a.shape) for a in small],
        out_specs=[tok(bd), per_b(s0.shape[1:])],
        out_shape=[jax.ShapeDtypeStruct((b, l, bd), F32), jax.ShapeDtypeStruct(s0.shape, F32)],
        scratch_shapes=[pltpu.VMEM((8 + q, c), F32), pltpu.VMEM((npair, LANES, SSM_STATE), F32)],
        compiler_params=_cparams(("parallel", "arbitrary")),
        name="ssm",
    )(xbc, z, dtm, dtt, buf, s0, *small)


def _rope_swap(x, lane, sign):
    half = ROPE_DIM // 2
    return jnp.where(lane < NOPE_DIM + half, pltpu.roll(x, LANES - half, 1), sign * pltpu.roll(x, half, 1))


def _qk_kernel(*refs, decode):
    if decode:
        (ql_ref, cos_ref, sin_ref, wq_ref, gq_ref, gk_ref, wukt_ref, qt_ref, qa_ref, qb_ref) = refs
    else:
        (ql_ref, cbf_ref, krp_ref, ct_ref, cos_ref, sin_ref, wq_ref, wk_ref, wuvt_ref, gq_ref, gk_ref,
         q_ref, k_ref, vt_ref) = refs
    bb, tl, _ = ql_ref.shape
    m = bb * tl
    ql = ql_ref[...].reshape(m, ql_ref.shape[-1])
    lane = lax.broadcasted_iota(jnp.int32, (m, LANES), 1)
    cos = jnp.broadcast_to(cos_ref[...][None], (bb, tl, LANES)).reshape(m, LANES)
    sin = jnp.broadcast_to(sin_ref[...][None], (bb, tl, LANES)).reshape(m, LANES)

    def norm_rope(x, g):
        xn = x * lax.rsqrt(jnp.sum(x * x, axis=-1, keepdims=True) * (1.0 / QK_DIM) + EPS) * g
        return xn * cos + _rope_swap(xn, lane, 1.0) * sin

    if not decode:
        cbf = cbf_ref[...].reshape(m, cbf_ref.shape[-1])
        krp = krp_ref[...].reshape(m, LANES)
    for hh in range(MLA_HEADS):
        qf = norm_rope(_dot(ql, wq_ref[hh]), gq_ref[...]) * (QK_SCALE * LOG2E)
        if decode:
            gk = gk_ref[...]
            nope = lane < NOPE_DIM
            qn = jnp.where(nope, qf * gk, 0.0).astype(BF16)
            qt_ref[:, hh] = _dot(qn, wukt_ref[hh]).reshape(bb, tl, wukt_ref.shape[-1])
            qa = qf * gk
            qb = _rope_swap(qf, lane, -1.0) * gk
            qa_ref[:, hh] = qa[:, NOPE_DIM:QK_DIM].reshape(bb, tl, ROPE_DIM)
            qb_ref[:, hh] = qb[:, NOPE_DIM:QK_DIM].reshape(bb, tl, ROPE_DIM)
        else:
            q_ref[:, hh] = qf.reshape(bb, tl, LANES).astype(BF16)
            kf = norm_rope(_dot(cbf, wk_ref[hh]) + krp, gk_ref[...])
            k_ref[:, hh] = kf.reshape(bb, tl, LANES).astype(BF16)
            vt_ref[0, hh] = _dot(wuvt_ref[hh], ct_ref[0]).astype(BF16)


def _qk_prompt(ql, cbf, krp, ct, cos, sin, wq, wk, wuvt, gq, gk):
    b, l, _ = ql.shape
    bb, tl = _tok_blocks(b, l)
    assert bb == 1
    vd = wuvt.shape[1]
    tok = lambda w_: pl.BlockSpec((bb, tl, w_), lambda i, j: (i, j, 0))
    tab = pl.BlockSpec((tl, LANES), lambda i, j: (j, 0))
    head = pl.BlockSpec((bb, MLA_HEADS, tl, LANES), lambda i, j: (i, 0, j, 0))
    shp = jax.ShapeDtypeStruct((b, MLA_HEADS, l, LANES), BF16)
    return pl.pallas_call(
        functools.partial(_qk_kernel, decode=False),
        grid=(b // bb, l // tl),
        in_specs=[tok(ql.shape[-1]), tok(cbf.shape[-1]), tok(LANES),
                  pl.BlockSpec((1, ct.shape[1], tl), lambda i, j: (i, 0, j)), tab, tab,
                  _resident(wq.shape), _resident(wk.shape), _resident(wuvt.shape),
                  _resident(gq.shape), _resident(gk.shape)],
        out_specs=[head, head, pl.BlockSpec((1, MLA_HEADS, vd, tl), lambda i, j: (i, 0, 0, j))],
        out_shape=[shp, shp, jax.ShapeDtypeStruct((b, MLA_HEADS, vd, l), BF16)],
        compiler_params=_cparams(("parallel", "parallel")),
        name="qk_prompt",
    )(ql, cbf, krp, ct, cos, sin, wq, wk, wuvt, gq, gk)


def _q_decode(ql, cos, sin, wq, gq, gk, wukt):
    b, l, _ = ql.shape
    bb, tl = _tok_blocks(b, l)
    assert tl == l
    kv = wukt.shape[-1]
    tok = lambda w_: pl.BlockSpec((bb, tl, w_), lambda i: (i, 0, 0))
    head = lambda w_: pl.BlockSpec((bb, MLA_HEADS, tl, w_), lambda i: (i, 0, 0, 0))
    shp = lambda w_: jax.ShapeDtypeStruct((b, MLA_HEADS, l, w_), F32)
    return pl.pallas_call(
        functools.partial(_qk_kernel, decode=True),
        grid=(b // bb,),
        in_specs=[tok(ql.shape[-1]), _resident(cos.shape), _resident(sin.shape),
                  _resident(wq.shape), _resident(gq.shape), _resident(gk.shape), _resident(wukt.shape)],
        out_specs=[head(kv), head(ROPE_DIM), head(ROPE_DIM)],
        out_shape=[shp(kv), shp(ROPE_DIM), shp(ROPE_DIM)],
        compiler_params=_cparams(("parallel",)),
        name="q_decode",
    )(ql, cos, sin, wq, gq, gk, wukt)


def _flash_kernel(qi_ref, ki_ref, q_ref, k_ref, vt_ref, o_ref, m_ref, l_ref, acc_ref):
    tq = q_ref.shape[2]
    tk = k_ref.shape[2]
    step_i = pl.program_id(1)
    qi, ki = qi_ref[step_i], ki_ref[step_i]

    @pl.when(ki == 0)
    def _():
        m_ref[...] = jnp.full(m_ref.shape, -jnp.inf, F32)
        l_ref[...] = jnp.zeros(l_ref.shape, F32)
        acc_ref[...] = jnp.zeros(acc_ref.shape, F32)

    vd = vt_ref.shape[2]

    def step(masked):
        if masked:
            keep = (ki * tk + lax.broadcasted_iota(jnp.int32, (tk, tq), 0)
                    <= qi * tq + lax.broadcasted_iota(jnp.int32, (tk, tq), 1))
        for hh in range(MLA_HEADS):
            st = _dot_nt(k_ref[0, hh], q_ref[0, hh])
            if masked:
                st = jnp.where(keep, st, -jnp.inf)
            m_prev = m_ref[hh:hh + 1, :]
            m_new = jnp.maximum(m_prev, jnp.max(st, axis=0, keepdims=True))
            alpha = jnp.exp2(m_prev - m_new)
            p = jnp.exp2(st - m_new)
            l_ref[hh:hh + 1, :] = alpha * l_ref[hh:hh + 1, :] + jnp.sum(p, axis=0, keepdims=True)
            rows = slice(hh * vd, (hh + 1) * vd)
            acc_ref[rows, :] = alpha * acc_ref[rows, :] + _dot(vt_ref[0, hh], p.astype(BF16))
            m_ref[hh:hh + 1, :] = m_new

    straddles = (ki + 1) * tk - 1 > qi * tq

    @pl.when(jnp.logical_not(straddles))
    def _():
        step(False)

    @pl.when(straddles)
    def _():
        step(True)

    @pl.when((ki + 1) * tk >= (qi + 1) * tq)
    def _():
        for hh in range(MLA_HEADS):
            rows = slice(hh * vd, (hh + 1) * vd)
            acc_ref[rows, :] = acc_ref[rows, :] / l_ref[hh:hh + 1, :]
        o_ref[0] = acc_ref[...].T


def _flash(q, k, vt):
    b, h, l, _ = q.shape
    t = min(l, ATTN_BLOCK)
    assert l % t == 0
    vd = vt.shape[2]
    n = l // t
    pairs = [(qi, ki) for qi in range(n) for ki in range(qi + 1)]
    qi_of = jnp.asarray([p[0] for p in pairs], jnp.int32)
    ki_of = jnp.asarray([p[1] for p in pairs], jnp.int32)
    grid_spec = pltpu.PrefetchScalarGridSpec(
        num_scalar_prefetch=2,
        grid=(b, len(pairs)),
        in_specs=[pl.BlockSpec((1, h, t, LANES), lambda i, s, qo, ko: (i, 0, qo[s], 0)),
                  pl.BlockSpec((1, h, t, LANES), lambda i, s, qo, ko: (i, 0, ko[s], 0)),
                  pl.BlockSpec((1, h, vd, t), lambda i, s, qo, ko: (i, 0, 0, ko[s]))],
        out_specs=pl.BlockSpec((1, t, h * vd), lambda i, s, qo, ko: (i, qo[s], 0)),
        scratch_shapes=[pltpu.VMEM((h, t), F32), pltpu.VMEM((h, t), F32), pltpu.VMEM((h * vd, t), F32)],
    )
    return pl.pallas_call(
        _flash_kernel,
        grid_spec=grid_spec,
        out_shape=jax.ShapeDtypeStruct((b, l, h * vd), F32),
        compiler_params=_cparams(("parallel", "arbitrary")),
        name="flash",
    )(qi_of, ki_of, q, k, vt)


def _decode_kernel(pt_ref, qt_ref, qab_ref, tab_ref, tabn_ref, new_ref, wukt_ref, wuv_ref, *rest, pages, group):
    page_refs = rest[:pages]
    o_ref, lhs_ref, ctb_ref, a_ref, s_ref, m_ref, l_ref, acc_ref = rest[pages:]
    j = pl.program_id(1)
    nrow = qt_ref.shape[1]
    lq = nrow // MLA_HEADS
    kvw = qt_ref.shape[2]
    nkt = wukt_ref.shape[0]
    psz = page_refs[0].shape[1]

    @pl.when(j == 0)
    def _():
        m_ref[...] = jnp.full(m_ref.shape, -jnp.inf, F32)
        l_ref[...] = jnp.zeros(l_ref.shape, F32)
        acc_ref[...] = jnp.zeros(acc_ref.shape, F32)
        lhs_ref[0:nkt, :] = wukt_ref[...]
        lhs_ref[nkt:nkt + nrow, :] = qt_ref[0].astype(BF16)

    qab = qab_ref[0].astype(BF16)

    def scores(ct, krt, cct, snt):
        nk = ct.shape[1]
        ctb = ct.astype(BF16)
        a = _dot(lhs_ref[...], ctb)
        kt = a[0:nkt]
        ss = jnp.sum((kt * kt).reshape(MLA_HEADS, nkt // MLA_HEADS, nk), axis=1)
        ss = ss + jnp.sum(krt * krt, axis=0, keepdims=True)
        inv = lax.rsqrt(ss * (1.0 / QK_DIM) + EPS)
        feats = jnp.concatenate([krt * cct, krt * snt], axis=0).astype(BF16)
        s = a[nkt:nkt + nrow] + _dot(qab, feats)
        return (s.reshape(MLA_HEADS, lq, nk) * inv[:, None, :]).reshape(nrow, nk), ctb

    def update(s, ctb):
        m_prev = m_ref[...]
        m_new = jnp.maximum(m_prev, jnp.max(s, axis=-1, keepdims=True))
        alpha = jnp.exp2(m_prev - m_new)
        p = jnp.exp2(s - m_new)
        l_ref[...] = alpha * l_ref[...] + jnp.sum(p, axis=-1, keepdims=True)
        acc_ref[...] = alpha * acc_ref[...] + _dot_nt(p.astype(BF16), ctb)
        m_ref[...] = m_new

    for pi in range(pages):
        ctb_ref[:, pi * psz:(pi + 1) * psz] = page_refs[pi][0:kvw, :].astype(BF16)
    a_ref[...] = _dot(lhs_ref[...], ctb_ref[...])
    for g0 in range(0, pages, group):
        refs = page_refs[g0:g0 + group]
        lanes = slice(g0 * psz, (g0 + group) * psz)
        nk = group * psz
        kt = a_ref[0:nkt, lanes]
        krt = jnp.concatenate([r[kvw:kvw + ROPE_DIM, :] for r in refs], axis=1) if group > 1 else refs[0][kvw:kvw + ROPE_DIM, :]
        ss = jnp.sum((kt * kt).reshape(MLA_HEADS, nkt // MLA_HEADS, nk), axis=1)
        ss = ss + jnp.sum(krt * krt, axis=0, keepdims=True)
        inv = lax.rsqrt(ss * (1.0 / QK_DIM) + EPS)
        feats = jnp.concatenate([krt * tab_ref[0, :, lanes], krt * tab_ref[1, :, lanes]], axis=0).astype(BF16)
        s = a_ref[nkt:nkt + nrow, lanes] + _dot(qab, feats)
        s_ref[:, lanes] = (s.reshape(MLA_HEADS, lq, nk) * inv[:, None, :]).reshape(nrow, nk)
    update(s_ref[...], ctb_ref[...])

    @pl.when(j == pl.num_programs(1) - 1)
    def _():
        nk = new_ref.shape[2]
        tq = lax.broadcasted_iota(jnp.int32, (nrow, nk), 0) % lq
        tk = lax.broadcasted_iota(jnp.int32, (nrow, nk), 1)
        s, ctb = scores(new_ref[0, 0:kvw, :], new_ref[0, kvw:kvw + ROPE_DIM, :], tabn_ref[0], tabn_ref[1])
        update(jnp.where(tk <= tq, s, -jnp.inf), ctb)
        ctx = (acc_ref[...] / l_ref[...]).astype(BF16)
        vd = wuv_ref.shape[-1]
        for hh in range(MLA_HEADS):
            o_ref[0, :, hh * vd:(hh + 1) * vd] = _dot(ctx[hh * lq:(hh + 1) * lq, :], wuv_ref[hh])


def _decode(page_table, qt, qab, tab, tabn, rows_new_t, wukt, wuv, pool_t, layer):
    bs, nrow, kvw = qt.shape
    n_pages = page_table.shape[1]
    cw, psz = pool_t.shape[2], pool_t.shape[3]
    pages = math.gcd(n_pages, DECODE_PAGES)
    group = math.gcd(pages, DECODE_GROUP)
    lq = nrow // MLA_HEADS
    vd = wuv.shape[-1]
    per_s = lambda shp: pl.BlockSpec((1,) + shp, lambda s, j, pt: (s,) + (0,) * len(shp))

    def page_spec(pi):
        return pl.BlockSpec((None, None, cw, psz), lambda s, j, pt: (layer, pt[s, j * pages + pi], 0, 0))

    grid_spec = pltpu.PrefetchScalarGridSpec(
        num_scalar_prefetch=1,
        grid=(bs, n_pages // pages),
        in_specs=[per_s((nrow, kvw)), per_s((nrow, 2 * ROPE_DIM)),
                  pl.BlockSpec((2, ROPE_DIM, pages * psz), lambda s, j, pt: (0, 0, j)),
                  _resident(tabn.shape), per_s(rows_new_t.shape[1:]),
                  _resident(wukt.shape), _resident(wuv.shape)] + [page_spec(pi) for pi in range(pages)],
        out_specs=per_s((lq, MLA_HEADS * vd)),
        scratch_shapes=[pltpu.VMEM((wukt.shape[0] + nrow, kvw), BF16),
                        pltpu.VMEM((kvw, pages * psz), BF16),
                        pltpu.VMEM((wukt.shape[0] + nrow, pages * psz), F32), pltpu.VMEM((nrow, pages * psz), F32),
                        pltpu.VMEM((nrow, 1), F32), pltpu.VMEM((nrow, 1), F32), pltpu.VMEM((nrow, kvw), F32)],
    )
    return pl.pallas_call(
        functools.partial(_decode_kernel, pages=pages, group=group),
        grid_spec=grid_spec,
        out_shape=jax.ShapeDtypeStruct((bs, lq, MLA_HEADS * vd), F32),
        compiler_params=_cparams(("parallel", "arbitrary")),
        name="decode",
    )(page_table, qt, qab, tab, tabn, rows_new_t, wukt, wuv, *([pool_t] * pages))


def _merge_kernel(h_ref, sh_ref, sc_ref, gt_ref, g_ref, ya_ref, yb_ref, yc_ref, yd_ref,
                  wg_ref, bg_ref, wb_ref, wo_ref, o_ref):
    bb, tl, d = h_ref.shape
    m = bb * tl
    h = h_ref[...]
    n = (_rms(h) * g_ref[...]) * (1.0 + sc_ref[...]) + sh_ref[...]
    n = n.reshape(m, d).astype(BF16)
    merged = None
    for r, y_ref in enumerate((ya_ref, yb_ref, yc_ref, yd_ref)):
        gate = jax.nn.sigmoid(_dot(n, wg_ref[r]) + bg_ref[r])
        term = gate * _dot(y_ref[...].reshape(m, y_ref.shape[-1]).astype(BF16), wb_ref[r])
        merged = term if merged is None else merged + term
    out = _dot(merged.astype(BF16), wo_ref[...])
    o_ref[...] = h + gt_ref[...] * out.reshape(bb, tl, d)


def _merge(h, sh, sc, gt, g, ys, wg, bg, wb, wo):
    b, l, d = h.shape
    bb, tl = _tok_blocks(b, l)
    tok = lambda w_: pl.BlockSpec((bb, tl, w_), lambda i, j: (i, j, 0))
    mod = pl.BlockSpec((bb, 1, d), lambda i, j: (i, 0, 0))
    return pl.pallas_call(
        _merge_kernel,
        grid=(b // bb, l // tl),
        in_specs=[tok(d), mod, mod, mod, _resident((1, d))] + [tok(y.shape[-1]) for y in ys]
                 + [_resident(wg.shape), _resident(bg.shape), _resident(wb.shape), _resident(wo.shape)],
        out_specs=tok(d),
        out_shape=jax.ShapeDtypeStruct(h.shape, F32),
        compiler_params=_cparams(("parallel", "parallel")),
        name="merge",
    )(h, sh, sc, gt, g, *ys, wg, bg, wb, wo)


def _rope_tables(pos):
    half = ROPE_DIM // 2
    inv = ROPE_THETA ** (-jnp.arange(half, dtype=F32) / half)
    ang = pos.astype(F32)[:, None] * inv
    return jnp.cos(ang), jnp.sin(ang)


def _lane_tables(pos):
    cos, sin = _rope_tables(pos)
    n = pos.shape[0]
    cos_t = jnp.concatenate([jnp.ones((n, NOPE_DIM), F32), cos, cos, jnp.zeros((n, LANES - QK_DIM), F32)], axis=1)
    sin_t = jnp.concatenate([jnp.zeros((n, NOPE_DIM), F32), -sin, sin, jnp.zeros((n, LANES - QK_DIM), F32)], axis=1)
    return cos_t, sin_t


def _pad_lanes(x, before, total):
    pad = [(0, 0)] * (x.ndim - 1) + [(before, total - before - x.shape[-1])]
    return jnp.pad(x, pad)


def _layer_weights(lw):
    d = lw["w_in"].shape[0]
    bd = lw["g_gm_v"].shape[0]
    q_lora, kv_lora = lw["g_q_lat"].shape[0], lw["g_kv_lat"].shape[0]
    layout, _ = _proj_layout(bd, q_lora, kv_lora)
    w = {}
    w["layout"] = layout
    w["w_ada"] = lw["w_ada"].astype(BF16)
    w["b_ada"] = lw["b_ada"][None, :]
    w["g_norm"] = [lw["g_norm"][i][None, :] for i in range(3)]
    w["w_ffn_in"] = [lw["w_ffn_in"][i].astype(BF16) for i in range(2)]
    w["w_ffn_out"] = [lw["w_ffn_out"][i].astype(BF16) for i in range(2)]
    w["w_proj"] = _proj_weight(lw["w_in"], bd, q_lora, kv_lora)
    w["g_gm_v"] = lw["g_gm_v"][None, :]
    w["g_q_lat"] = lw["g_q_lat"][None, :]
    w["g_kv_lat"] = lw["g_kv_lat"][None, :]
    w["w_spatial"] = lw["w_spatial"]
    w["b_spatial_t"] = lw["b_spatial"].T
    w["w_ssm_conv"] = lw["w_ssm_conv"]
    w["b_ssm_conv"] = lw["b_ssm_conv"][None, :]
    w["dt_bias"] = lw["dt_bias"][None, :]
    w["dt_bias_t"] = lw["dt_bias"][:, None]
    w["a_log"] = lw["a_log"][None, :]
    w["a_log_t"] = lw["a_log"][:, None]
    w["d_skip"] = jnp.repeat(lw["d_skip"], bd // SSM_HEADS)[None, :]
    w["g_ssm_norm"] = lw["g_ssm_norm"][None, :]
    w["w_sc_conv"] = lw["w_sc_conv"]
    w["wq"] = _pad_lanes(jnp.moveaxis(lw["w_uq"], 1, 0), 0, LANES).astype(BF16)
    wuk = jnp.moveaxis(lw["w_uk"], 1, 0)
    w["wk"] = _pad_lanes(wuk, 0, LANES).astype(BF16)
    wukt = jnp.swapaxes(wuk, 1, 2)
    w["wukt_pad"] = jnp.pad(wukt, ((0, 0), (0, LANES - NOPE_DIM), (0, 0))).astype(BF16)
    w["wukt_flat"] = wukt.reshape(MLA_HEADS * NOPE_DIM, kv_lora).astype(BF16)
    w["wuv"] = jnp.moveaxis(lw["w_uv"], 1, 0).astype(BF16)
    w["wuvt"] = jnp.swapaxes(w["wuv"], 1, 2)
    w["gq"] = _pad_lanes(lw["g_qk"][0][None, :], 0, LANES)
    w["gk"] = _pad_lanes(lw["g_qk"][1][None, :], 0, LANES)
    w["w_gate"] = lw["w_gate"].astype(BF16)
    w["b_gate"] = lw["b_gate"][:, None, :]
    w["w_branch_out"] = lw["w_branch_out"].astype(BF16)
    w["w_out"] = lw["w_out"].astype(BF16)
    return w


def _trunk_layer(x, mod, w, ssm_buf, ssm_s0, sc_buf, attn_fn):
    b, l, d = x.shape
    sh1, sc1, gt1, sh2, sc2, gt2, sh3, sc3, gt3 = [m[:, None, :] for m in jnp.split(mod, N_MOD, axis=-1)]
    h = _ffn(x, sh1, sc1, gt1, w["g_norm"][0], w["w_ffn_in"][0], w["w_ffn_out"][0])
    (a_u, a_v, b_z, b_xbc, c_p, c_b, ql, rows, cbf, krp, dtm, *ct) = _proj(
        h, sh2, sc2, w["g_norm"][1], w["w_proj"], w["g_gm_v"], w["g_q_lat"], w["g_kv_lat"], w["layout"])
    y_a = _gmlp(a_u, a_v, w["w_spatial"], w["b_spatial_t"])
    lp = -(-l // SSM_CHUNK) * SSM_CHUNK
    padl = lambda t: jnp.pad(t, ((0, 0), (0, lp - l), (0, 0)))
    dtt = jnp.swapaxes(padl(dtm)[:, :, 0:SSM_HEADS], 1, 2)
    npair = SSM_HEADS // 2
    s0 = ssm_s0.reshape(b, npair, LANES, SSM_STATE)
    y_b, s_new = _ssm(padl(b_xbc), padl(b_z), padl(dtm), dtt, ssm_buf, s0, w["w_ssm_conv"], w["b_ssm_conv"],
                      w["dt_bias"], w["dt_bias_t"], w["a_log"], w["a_log_t"], w["d_skip"], w["g_ssm_norm"],
                      valid=min(l, SSM_CHUNK) if lp != l else SSM_CHUNK)
    y_b = y_b[:, :l]
    s_new = s_new.reshape(ssm_s0.shape)
    ssm_buf_new = jnp.concatenate([ssm_buf, b_xbc], axis=1)[:, -(SSM_CONV - 1):]
    y_c = _sconv(c_p, c_b, sc_buf, w["w_sc_conv"])
    sc_buf_new = jnp.concatenate([sc_buf, c_p], axis=1)[:, -(SC_WIDTH - 1):]
    y_d = attn_fn(ql, rows, cbf, krp, *ct)
    h = _merge(h, sh2, sc2, gt2, w["g_norm"][1], (y_a, y_b, y_c, y_d), w["w_gate"], w["b_gate"],
               w["w_branch_out"], w["w_out"])
    h = _ffn(h, sh3, sc3, gt3, w["g_norm"][2], w["w_ffn_in"][1], w["w_ffn_out"][1])
    return h, rows, ssm_buf_new, s_new, sc_buf_new, a_v


def kernel(x_prompt, x_sample, c_prompt, c_sample, cache_mla, page_table, state_ssm, state_ssm_conv,
           state_short_conv, w_ada, b_ada, g_norm, w_ffn_in, w_ffn_out, w_in, g_gm_v, w_spatial, b_spatial,
           w_ssm_conv, b_ssm_conv, dt_bias, a_log, d_skip, g_ssm_norm, w_sc_conv, g_q_lat, w_uq, g_kv_lat,
           w_uk, w_uv, g_qk, w_branch_out, w_gate, b_gate, w_out):
    params = dict(w_ada=w_ada, b_ada=b_ada, g_norm=g_norm, w_ffn_in=w_ffn_in, w_ffn_out=w_ffn_out, w_in=w_in,
                  g_gm_v=g_gm_v, w_spatial=w_spatial, b_spatial=b_spatial, w_ssm_conv=w_ssm_conv,
                  b_ssm_conv=b_ssm_conv, dt_bias=dt_bias, a_log=a_log, d_skip=d_skip, g_ssm_norm=g_ssm_norm,
                  w_sc_conv=w_sc_conv, g_q_lat=g_q_lat, w_uq=w_uq, g_kv_lat=g_kv_lat, w_uk=w_uk, w_uv=w_uv,
                  g_qk=g_qk, w_branch_out=w_branch_out, w_gate=w_gate, b_gate=b_gate, w_out=w_out)
    depth = w_ada.shape[0]
    bp, lp, d = x_prompt.shape
    bs, ls, _ = x_sample.shape
    page = cache_mla.shape[2]
    past = page_table.shape[1] * page
    bd = g_gm_v.shape[1]
    conv_ch = w_ssm_conv.shape[-1]

    pos_p = jnp.arange(lp, dtype=jnp.int32)
    pos_s = past + jnp.arange(ls, dtype=jnp.int32)
    cos_p, sin_p = _lane_tables(pos_p)
    cos_s, sin_s = _lane_tables(pos_s)
    def key_tables(pos, width):
        cos, sin = _rope_tables(pos)
        t = jnp.stack([jnp.concatenate([cos, cos], axis=1).T, jnp.concatenate([sin, sin], axis=1).T])
        return jnp.pad(t, ((0, 0), (0, 0), (0, width - pos.shape[0])))

    tab_k = key_tables(jnp.arange(past, dtype=jnp.int32), past)
    new_w = -(-ls // LANES) * LANES
    tab_n = key_tables(pos_s, new_w)
    pool_t = jnp.swapaxes(cache_mla, 2, 3)
    c_all = jnp.concatenate([c_prompt, c_sample], axis=0)

    yp, ys = x_prompt, x_sample
    outs = [[] for _ in range(9)]
    for layer in range(depth):
        w = _layer_weights({k: v[layer] for k, v in params.items()})
        mod = _ada(c_all, w["w_ada"], w["b_ada"])

        def attn_prompt(ql, rows, cbf, krp, ct, w=w):
            q, k, vt = _qk_prompt(ql, cbf, krp, ct, cos_p, sin_p, w["wq"], w["wk"], w["wuvt"], w["gq"], w["gk"])
            return _flash(q, k, vt)

        def attn_sample(ql, rows, cbf, krp, *unused, w=w, layer=layer):
            qt, qa, qb = _q_decode(ql, cos_s, sin_s, w["wq"], w["gq"], w["gk"], w["wukt_pad"])
            flat = lambda t: t.reshape(bs, MLA_HEADS * ls, t.shape[-1])
            rows_t = jnp.pad(jnp.swapaxes(rows, 1, 2), ((0, 0), (0, 0), (0, new_w - ls)))
            return _decode(page_table, flat(qt), flat(jnp.concatenate([qa, qb], axis=-1)), tab_k, tab_n, rows_t,
                           w["wukt_flat"], w["wuv"], pool_t, layer)

        yp, r, cb, s, scb, _ = _trunk_layer(
            yp, mod[:bp], w,
            jnp.zeros((bp, SSM_CONV - 1, conv_ch), F32),
            jnp.zeros((bp, SSM_HEADS, bd // SSM_HEADS, SSM_STATE), F32),
            jnp.zeros((bp, SC_WIDTH - 1, bd), F32), attn_prompt)
        for lst, v in zip((outs[0], outs[2], outs[4], outs[6]), (r, s, cb, scb)):
            lst.append(v)
        ys, r, cb, s, scb, v = _trunk_layer(
            ys, mod[bp:], w, state_ssm_conv[layer], state_ssm[layer], state_short_conv[layer], attn_sample)
        for lst, val in zip((outs[1], outs[3], outs[5], outs[7], outs[8]), (r, s, cb, scb, v)):
            lst.append(val)
    st = jnp.stack
    return (yp, ys, st(outs[0]), st(outs[1]), st(outs[2]), st(outs[3]), st(outs[4]), st(outs[5]),
            st(outs[6]), st(outs[7]), st(outs[8]))
```

```python
import functools
import math

import jax
import jax.numpy as jnp
from jax import lax
from jax.experimental import pallas as pl
from jax.experimental.pallas import tpu as pltpu

F32 = jnp.float32
BF16 = jnp.bfloat16
HIGHEST = lax.Precision.HIGHEST

EPS = 1e-6
LANES = 128
N_MOD = 9
GM_CHUNK = 128
GM_HEADS = 4
SSM_HEADS = 8
SSM_GROUPS = 2
SSM_STATE = 128
SSM_CHUNK = 128
SSM_CONV = 4
SC_WIDTH = 3
MLA_HEADS = 8
NOPE_DIM = 64
ROPE_DIM = 32
QK_DIM = NOPE_DIM + ROPE_DIM
ROPE_THETA = 10000.0
QK_SCALE = 1.0 / math.sqrt(QK_DIM)
LOG2E = math.log2(math.e)
ROW_BLOCK = 512
ATTN_BLOCK = 512
DECODE_PAGES = 16
DECODE_GROUP = 4
VMEM_LIMIT = 56 * 1024 * 1024


def _cparams(sem):
    return pltpu.CompilerParams(dimension_semantics=sem, vmem_limit_bytes=VMEM_LIMIT)


def _resident(shape):
    nd = len(shape)
    return pl.BlockSpec(shape, lambda *_: (0,) * nd, pipeline_mode=pl.Buffered(1))


def _tok_blocks(b, l):
    tl = min(l, ROW_BLOCK)
    bb = min(b, max(1, ROW_BLOCK // tl))
    assert l % tl == 0 and b % bb == 0 and tl % 8 == 0
    return bb, tl


def _silu(x):
    return x * jax.nn.sigmoid(x)


def _rms(x):
    return x * lax.rsqrt(jnp.mean(x * x, axis=-1, keepdims=True) + EPS)


def _dot(a, b):
    return jnp.dot(a, b, preferred_element_type=F32)


def _dot_nt(a, b, precision=None):
    return lax.dot_general(a, b, (((1,), (1,)), ((), ())), preferred_element_type=F32, precision=precision)


def _ada_kernel(c_ref, w_ref, b_ref, o_ref):
    o_ref[...] = _dot(_silu(c_ref[...]).astype(BF16), w_ref[...]) + b_ref[...]


def _ada(c, w, b):
    bt, d = c.shape
    n = w.shape[1]
    return pl.pallas_call(
        _ada_kernel,
        grid=(n // d,),
        in_specs=[pl.BlockSpec((bt, d), lambda j: (0, 0)),
                  pl.BlockSpec((d, d), lambda j: (0, j)),
                  pl.BlockSpec((1, d), lambda j: (0, j))],
        out_specs=pl.BlockSpec((bt, d), lambda j: (0, j)),
        out_shape=jax.ShapeDtypeStruct((bt, n), F32),
        compiler_params=_cparams(("arbitrary",)),
        name="ada",
    )(c, w, b)


def _ffn_kernel(x_ref, sh_ref, sc_ref, gt_ref, g_ref, win_ref, wout_ref, o_ref, acc_ref, *, tf):
    bb, tl, d = x_ref.shape
    f = wout_ref.shape[0]
    x = x_ref[...]
    xn = (_rms(x) * g_ref[...]) * (1.0 + sc_ref[...]) + sh_ref[...]
    xn = xn.reshape(bb * tl, d).astype(BF16)
    for i in range(f // tf):
        g = _dot(xn, win_ref[:, i * tf:(i + 1) * tf])
        u = _dot(xn, win_ref[:, f + i * tf:f + (i + 1) * tf])
        a = (_silu(g) * u).astype(BF16)
        part = _dot(a, wout_ref[i * tf:(i + 1) * tf, :])
        if i == 0:
            acc_ref[...] = part
        else:
            acc_ref[...] += part
    o_ref[...] = x + 0.5 * gt_ref[...] * acc_ref[...].reshape(bb, tl, d)


def _ffn(x, sh, sc, gt, g, w_in, w_out):
    b, l, d = x.shape
    bb, tl = _tok_blocks(b, l)
    f = w_out.shape[0]
    tf = 256 if f % 256 == 0 else LANES
    tok = pl.BlockSpec((bb, tl, d), lambda i, j: (i, j, 0))
    mod = pl.BlockSpec((bb, 1, d), lambda i, j: (i, 0, 0))
    return pl.pallas_call(
        functools.partial(_ffn_kernel, tf=tf),
        grid=(b // bb, l // tl),
        in_specs=[tok, mod, mod, mod, _resident((1, d)), _resident(w_in.shape), _resident(w_out.shape)],
        out_specs=tok,
        out_shape=jax.ShapeDtypeStruct(x.shape, F32),
        scratch_shapes=[pltpu.VMEM((bb * tl, d), F32)],
        compiler_params=_cparams(("parallel", "parallel")),
        name="ffn",
    )(x, sh, sc, gt, g, w_in, w_out)


_PROJ_GROUPS = ("a_u", "a_v", "b_z", "b_xbc", "c_h", "c_b", "c_c", "d_q", "d_kv", "kr0", "krp", "dt")


def _proj_layout(bd, q_lora, kv_lora):
    widths = dict(a_u=bd, a_v=bd, b_z=bd, b_xbc=2 * bd, c_h=bd, c_b=bd, c_c=bd, d_q=q_lora, d_kv=kv_lora,
                  kr0=LANES, krp=LANES, dt=LANES)
    off, layout = 0, {}
    for name in _PROJ_GROUPS:
        layout[name] = (off, widths[name])
        off += widths[name]
    return layout, off


def _proj_weight(w_in, bd, q_lora, kv_lora):
    d = w_in.shape[0]
    sizes = (bd, bd, bd, 2 * bd, SSM_HEADS, bd, bd, bd, q_lora, kv_lora, ROPE_DIM)
    a_u, a_v, b_z, b_xbc, b_dt, c_h, c_b, c_c, d_q, d_kv, d_kr = jnp.split(w_in, _cumsum(sizes)[:-1], axis=1)
    z = lambda n: jnp.zeros((d, n), w_in.dtype)
    kr0 = jnp.concatenate([d_kr, z(LANES - ROPE_DIM)], axis=1)
    krp = jnp.concatenate([z(NOPE_DIM), d_kr, z(LANES - QK_DIM)], axis=1)
    dt = jnp.concatenate([b_dt, z(LANES - SSM_HEADS)], axis=1)
    return jnp.concatenate([a_u, a_v, b_z, b_xbc, c_h, c_b, c_c, d_q, d_kv, kr0, krp, dt], axis=1).astype(BF16)


def _cumsum(sizes):
    out, s = [], 0
    for v in sizes:
        s += v
        out.append(s)
    return out


def _proj_kernel(h_ref, sh_ref, sc_ref, g_ref, w_ref, ggm_ref, gq_ref, gkv_ref,
                 au_ref, av_ref, z_ref, xbc_ref, p_ref, gb_ref, ql_ref, rows_ref, cbf_ref, krp_ref, dt_ref,
                 *maybe_ct_ref, layout):
    bb, tl, d = h_ref.shape
    n = (_rms(h_ref[...]) * g_ref[...]) * (1.0 + sc_ref[...]) + sh_ref[...]
    n = n.reshape(bb * tl, d).astype(BF16)

    def grp(name):
        off, w = layout[name]
        return _dot(n, w_ref[:, off:off + w])

    def put(ref, val):
        ref[...] = val.reshape(bb, tl, val.shape[-1]).astype(ref.dtype)

    put(au_ref, jax.nn.gelu(grp("a_u")))
    put(av_ref, _rms(jax.nn.gelu(grp("a_v"))) * ggm_ref[...])
    put(z_ref, grp("b_z"))
    put(xbc_ref, grp("b_xbc"))
    put(p_ref, grp("c_c") * grp("c_h"))
    put(gb_ref, grp("c_b"))
    put(ql_ref, _rms(grp("d_q")) * gq_ref[...])
    kv = _rms(grp("d_kv")) * gkv_ref[...]
    put(cbf_ref, kv)
    if maybe_ct_ref:
        maybe_ct_ref[0][0] = kv.T.astype(BF16)
    kvw = kv.shape[-1]
    rows_ref[:, :, 0:kvw] = kv.reshape(bb, tl, kvw)
    rows_ref[:, :, kvw:kvw + ROPE_DIM] = grp("kr0")[:, 0:ROPE_DIM].reshape(bb, tl, ROPE_DIM)
    put(krp_ref, grp("krp"))
    put(dt_ref, grp("dt"))


def _proj(h, sh, sc, g, w, ggm, gq, gkv, layout):
    b, l, d = h.shape
    bb, tl = _tok_blocks(b, l)
    bd, q_lora, kv_lora = ggm.shape[1], gq.shape[1], gkv.shape[1]
    tok = lambda w_: pl.BlockSpec((bb, tl, w_), lambda i, j: (i, j, 0))
    mod = pl.BlockSpec((bb, 1, d), lambda i, j: (i, 0, 0))
    outs = [(bd, F32), (bd, F32), (bd, F32), (2 * bd, F32), (bd, F32), (bd, F32), (q_lora, BF16),
            (kv_lora + ROPE_DIM, F32), (kv_lora, BF16), (LANES, F32), (LANES, F32)]
    out_specs = [tok(w_) for w_, _ in outs]
    out_shape = [jax.ShapeDtypeStruct((b, l, w_), dt) for w_, dt in outs]
    if bb == 1 and tl % LANES == 0:
        out_specs.append(pl.BlockSpec((1, kv_lora, tl), lambda i, j: (i, 0, j)))
        out_shape.append(jax.ShapeDtypeStruct((b, kv_lora, l), BF16))
    return pl.pallas_call(
        functools.partial(_proj_kernel, layout=layout),
        grid=(b // bb, l // tl),
        in_specs=[tok(d), mod, mod, _resident((1, d)), _resident(w.shape),
                  _resident(ggm.shape), _resident(gq.shape), _resident(gkv.shape)],
        out_specs=out_specs,
        out_shape=out_shape,
        compiler_params=_cparams(("parallel", "parallel")),
        name="proj",
    )(h, sh, sc, g, w, ggm, gq, gkv)


def _gmlp_kernel(u_ref, v_ref, w_ref, bt_ref, o_ref, *, chunk):
    bb, tl, bd = u_ref.shape
    hd = bd // GM_HEADS
    row = lax.broadcasted_iota(jnp.int32, (chunk, chunk), 0)
    col = lax.broadcasted_iota(jnp.int32, (chunk, chunk), 1)
    for hh in range(GM_HEADS):
        wm = jnp.where(col <= row, w_ref[hh, 0:chunk, 0:chunk], 0.0)
        bias = bt_ref[0:chunk, hh:hh + 1]
        cols = slice(hh * hd, (hh + 1) * hd)
        if chunk == GM_CHUNK:
            wmb = wm.astype(BF16)
            for bi in range(bb):
                for ci in range(tl // chunk):
                    rows = slice(ci * chunk, (ci + 1) * chunk)
                    mixed = _dot(wmb, v_ref[bi, rows, cols].astype(BF16)) + bias
                    o_ref[bi, rows, cols] = u_ref[bi, rows, cols] * mixed
        else:
            v = v_ref[:, :, cols]
            mixed = jnp.zeros((bb, chunk, hd), F32) + bias
            for j in range(chunk):
                mixed = mixed + wm[:, j:j + 1] * v[:, j:j + 1, :]
            o_ref[:, :, cols] = u_ref[:, :, cols] * mixed


def _gmlp(u, v, w_s, b_t):
    b, l, bd = u.shape
    bb, tl = _tok_blocks(b, l)
    chunk = min(l, GM_CHUNK)
    assert chunk == GM_CHUNK or chunk == tl
    tok = pl.BlockSpec((bb, tl, bd), lambda i, j: (i, j, 0))
    return pl.pallas_call(
        functools.partial(_gmlp_kernel, chunk=chunk),
        grid=(b // bb, l // tl),
        in_specs=[tok, tok, _resident(w_s.shape), _resident(b_t.shape)],
        out_specs=tok,
        out_shape=jax.ShapeDtypeStruct(u.shape, F32),
        compiler_params=_cparams(("parallel", "parallel")),
        name="gmlp",
    )(u, v, w_s, b_t)


def _sconv_kernel(p_ref, gb_ref, buf_ref, w_ref, o_ref, hist_ref):
    bb, tl, c = p_ref.shape
    k = w_ref.shape[0]

    @pl.when(pl.program_id(1) == 0)
    def _():
        hist_ref[:, 0:8, :] = jnp.zeros((bb, 8, c), F32)
        hist_ref[:, 8 - (k - 1):8, :] = buf_ref[...]

    @pl.when(pl.program_id(1) > 0)
    def _():
        hist_ref[:, 0:8, :] = hist_ref[:, tl:tl + 8, :]

    hist_ref[:, 8:8 + tl, :] = p_ref[...]
    acc = jnp.zeros((bb, tl, c), F32)
    for i in range(k):
        s = 8 - (k - 1) + i
        acc = acc + w_ref[i:i + 1, :] * hist_ref[:, s:s + tl, :]
    o_ref[...] = gb_ref[...] * acc


def _sconv(p, gb, buf, w):
    b, l, c = p.shape
    bb, tl = _tok_blocks(b, l)
    tok = pl.BlockSpec((bb, tl, c), lambda i, j: (i, j, 0))
    return pl.pallas_call(
        _sconv_kernel,
        grid=(b // bb, l // tl),
        in_specs=[tok, tok, pl.BlockSpec((bb, buf.shape[1], c), lambda i, j: (i, 0, 0)), _resident(w.shape)],
        out_specs=tok,
        out_shape=jax.ShapeDtypeStruct(p.shape, F32),
        scratch_shapes=[pltpu.VMEM((bb, 8 + tl, c), F32)],
        compiler_params=_cparams(("parallel", "arbitrary")),
        name="sconv",
    )(p, gb, buf, w)


def _softplus(x):
    return jnp.maximum(x, 0.0) + jnp.log1p(jnp.exp(-jnp.abs(x)))


def _ssm_kernel(xbc_ref, z_ref, dtm_ref, dtt_ref, buf_ref, s0_ref, wc_ref, bc_ref, dtb_ref, dtbt_ref,
                alog_ref, alogt_ref, dsk_ref, gn_ref, y_ref, sout_ref, hist_ref, s_ref):
    q = xbc_ref.shape[1]
    c = xbc_ref.shape[-1]
    bd = z_ref.shape[-1]
    k = wc_ref.shape[0]
    ci = pl.program_id(1)

    @pl.when(ci == 0)
    def _():
        hist_ref[0:8, :] = jnp.zeros((8, c), F32)
        hist_ref[8 - (k - 1):8, :] = buf_ref[0]
        s_ref[...] = s0_ref[0]

    @pl.when(ci > 0)
    def _():
        hist_ref[0:8, :] = hist_ref[q:q + 8, :]

    hist_ref[8:8 + q, :] = xbc_ref[0]
    acc = jnp.zeros((q, c), F32) + bc_ref[...]
    for i in range(k):
        s = 8 - (k - 1) + i
        acc = acc + wc_ref[i:i + 1, :] * hist_ref[s:s + q, :]
    xc = _silu(acc)
    xs = xc[:, 0:bd]
    gw = SSM_STATE
    bmf = [xc[:, bd + g * gw:bd + (g + 1) * gw] for g in range(SSM_GROUPS)]
    bm = [t.astype(BF16) for t in bmf]
    cm = [xc[:, bd + (SSM_GROUPS + g) * gw:bd + (SSM_GROUPS + g + 1) * gw].astype(BF16) for g in range(SSM_GROUPS)]

    row = lax.broadcasted_iota(jnp.int32, (q, q), 0)
    col = lax.broadcasted_iota(jnp.int32, (q, q), 1)
    causal = col <= row
    dt = _softplus(dtm_ref[0][:, 0:SSM_HEADS] + dtb_ref[...])
    dtt = _softplus(dtt_ref[0] + dtbt_ref[...])
    da = dt * (-jnp.exp(alog_ref[...]))
    dat = dtt * (-jnp.exp(alogt_ref[...]))
    cum = jnp.dot(causal.astype(F32), da, preferred_element_type=F32, precision=HIGHEST)
    cumt = jnp.dot(dat, (row <= col).astype(F32), preferred_element_type=F32, precision=HIGHEST)
    last = cum[q - 1:q, :]

    lane = lax.broadcasted_iota(jnp.int32, (q, LANES), 1)
    srow = lax.broadcasted_iota(jnp.int32, (LANES, gw), 0)
    p_dim = bd // SSM_HEADS
    rep = SSM_HEADS // SSM_GROUPS
    for pair in range(SSM_HEADS // 2):
        h0, h1 = 2 * pair, 2 * pair + 1
        g = h0 // rep
        cb = _dot_nt(cm[g], bm[g])
        x_pair = xs[:, pair * LANES:(pair + 1) * LANES]
        xb = x_pair.astype(BF16)
        ys = []
        for hh in (h0, h1):
            seg = cum[:, hh:hh + 1] - cumt[hh:hh + 1, :]
            dec = jnp.where(causal, jnp.exp(jnp.where(causal, seg, 0.0)), 0.0) * dtt[hh:hh + 1, :]
            ys.append(_dot((cb * dec).astype(BF16), xb))
        first = lane < p_dim
        cum_pair = jnp.where(first, cum[:, h0:h0 + 1], cum[:, h1:h1 + 1])
        last_pair = jnp.where(first, last[:, h0:h0 + 1], last[:, h1:h1 + 1])
        dt_pair = jnp.where(first, dt[:, h0:h0 + 1], dt[:, h1:h1 + 1])
        s_in = s_ref[pair]
        y_pair = jnp.where(first, ys[0], ys[1]) + jnp.exp(cum_pair) * _dot_nt(cm[g], s_in.astype(BF16))
        y_ref[0, :, pair * LANES:(pair + 1) * LANES] = y_pair
        xw = x_pair * (jnp.exp(last_pair - cum_pair) * dt_pair)
        bmg = bm[g]
        if q < LANES:
            zpad = jnp.zeros((LANES - q, LANES), F32)
            xw = jnp.concatenate([xw, zpad], axis=0)
            bmg = jnp.concatenate([bmf[g], zpad], axis=0).astype(BF16)
        cs = _dot(xw.T.astype(BF16), bmg)
        cd = jnp.where(srow < p_dim, jnp.exp(last[:, h0:h0 + 1]), jnp.exp(last[:, h1:h1 + 1]))
        s_ref[pair] = s_in * cd + cs

    y = (y_ref[0] + dsk_ref[...] * xs) * _silu(z_ref[0])
    gs = bd // SSM_GROUPS
    for g in range(SSM_GROUPS):
        y_ref[0, :, g * gs:(g + 1) * gs] = _rms(y[:, g * gs:(g + 1) * gs]) * gn_ref[:, g * gs:(g + 1) * gs]
    sout_ref[0] = s_ref[...]


def _ssm(xbc, z, dtm, dtt, buf, s0, wc, bc, dtb, dtbt, alog, alogt, dsk, gn):
    b, l, c = xbc.shape
    bd = z.shape[-1]
    q = min(l, SSM_CHUNK)
    assert SSM_STATE == LANES and l % q == 0 and q % 8 == 0 and SSM_HEADS % 2 == 0 and bd // SSM_HEADS * 2 == LANES and (SSM_HEADS // SSM_GROUPS) % 2 == 0
    npair = SSM_HEADS // 2
    tok = lambda w_: pl.BlockSpec((1, q, w_), lambda i, j: (i, j, 0))
    per_b = lambda shp: pl.BlockSpec((1,) + shp, lambda i, j: (i,) + (0,) * len(shp))
    small = [wc, bc, dtb, dtbt, alog, alogt, dsk, gn]
    return pl.pallas_call(
        _ssm_kernel,
        grid=(b, l // q),
        in_specs=[tok(c), tok(bd), tok(LANES), pl.BlockSpec((1, SSM_HEADS, q), lambda i, j: (i, 0, j)),
                  per_b(buf.shape[1:]), per_b(s0.shape[1:])] + [_resident(a.shape) for a in small],
        out_specs=[tok(bd), per_b(s0.shape[1:])],
        out_shape=[jax.ShapeDtypeStruct((b, l, bd), F32), jax.ShapeDtypeStruct(s0.shape, F32)],
        scratch_shapes=[pltpu.VMEM((8 + q, c), F32), pltpu.VMEM((npair, LANES, SSM_STATE), F32)],
        compiler_params=_cparams(("parallel", "arbitrary")),
        name="ssm",
    )(xbc, z, dtm, dtt, buf, s0, *small)


def _rope_swap(x, lane, sign):
    half = ROPE_DIM // 2
    return jnp.where(lane < NOPE_DIM + half, pltpu.roll(x, LANES - half, 1), sign * pltpu.roll(x, half, 1))


def _qk_kernel(*refs, decode):
    if decode:
        (ql_ref, cos_ref, sin_ref, wq_ref, gq_ref, gk_ref, wukt_ref, qt_ref, qa_ref, qb_ref) = refs
    else:
        (ql_ref, cbf_ref, krp_ref, ct_ref, cos_ref, sin_ref, wq_ref, wk_ref, wuvt_ref, gq_ref, gk_ref,
         q_ref, k_ref, vt_ref) = refs
    bb, tl, _ = ql_ref.shape
    m = bb * tl
    ql = ql_ref[...].reshape(m, ql_ref.shape[-1])
    lane = lax.broadcasted_iota(jnp.int32, (m, LANES), 1)
    cos = jnp.broadcast_to(cos_ref[...][None], (bb, tl, LANES)).reshape(m, LANES)
    sin = jnp.broadcast_to(sin_ref[...][None], (bb, tl, LANES)).reshape(m, LANES)

    def norm_rope(x, g):
        xn = x * lax.rsqrt(jnp.sum(x * x, axis=-1, keepdims=True) * (1.0 / QK_DIM) + EPS) * g
        return xn * cos + _rope_swap(xn, lane, 1.0) * sin

    if not decode:
        cbf = cbf_ref[...].reshape(m, cbf_ref.shape[-1])
        krp = krp_ref[...].reshape(m, LANES)
    for hh in range(MLA_HEADS):
        qf = norm_rope(_dot(ql, wq_ref[hh]), gq_ref[...]) * (QK_SCALE * LOG2E)
        if decode:
            gk = gk_ref[...]
            nope = lane < NOPE_DIM
            qn = jnp.where(nope, qf * gk, 0.0).astype(BF16)
            qt_ref[:, hh] = _dot(qn, wukt_ref[hh]).reshape(bb, tl, wukt_ref.shape[-1])
            qa = qf * gk
            qb = _rope_swap(qf, lane, -1.0) * gk
            qa_ref[:, hh] = qa[:, NOPE_DIM:QK_DIM].reshape(bb, tl, ROPE_DIM)
            qb_ref[:, hh] = qb[:, NOPE_DIM:QK_DIM].reshape(bb, tl, ROPE_DIM)
        else:
            q_ref[:, hh] = qf.reshape(bb, tl, LANES).astype(BF16)
            kf = norm_rope(_dot(cbf, wk_ref[hh]) + krp, gk_ref[...])
            k_ref[:, hh] = kf.reshape(bb, tl, LANES).astype(BF16)
            vt_ref[0, hh] = _dot(wuvt_ref[hh], ct_ref[0]).astype(BF16)


def _qk_prompt(ql, cbf, krp, ct, cos, sin, wq, wk, wuvt, gq, gk):
    b, l, _ = ql.shape
    bb, tl = _tok_blocks(b, l)
    assert bb == 1
    vd = wuvt.shape[1]
    tok = lambda w_: pl.BlockSpec((bb, tl, w_), lambda i, j: (i, j, 0))
    tab = pl.BlockSpec((tl, LANES), lambda i, j: (j, 0))
    head = pl.BlockSpec((bb, MLA_HEADS, tl, LANES), lambda i, j: (i, 0, j, 0))
    shp = jax.ShapeDtypeStruct((b, MLA_HEADS, l, LANES), BF16)
    return pl.pallas_call(
        functools.partial(_qk_kernel, decode=False),
        grid=(b // bb, l // tl),
        in_specs=[tok(ql.shape[-1]), tok(cbf.shape[-1]), tok(LANES),
                  pl.BlockSpec((1, ct.shape[1], tl), lambda i, j: (i, 0, j)), tab, tab,
                  _resident(wq.shape), _resident(wk.shape), _resident(wuvt.shape),
                  _resident(gq.shape), _resident(gk.shape)],
        out_specs=[head, head, pl.BlockSpec((1, MLA_HEADS, vd, tl), lambda i, j: (i, 0, 0, j))],
        out_shape=[shp, shp, jax.ShapeDtypeStruct((b, MLA_HEADS, vd, l), BF16)],
        compiler_params=_cparams(("parallel", "parallel")),
        name="qk_prompt",
    )(ql, cbf, krp, ct, cos, sin, wq, wk, wuvt, gq, gk)


def _q_decode(ql, cos, sin, wq, gq, gk, wukt):
    b, l, _ = ql.shape
    bb, tl = _tok_blocks(b, l)
    assert tl == l
    kv = wukt.shape[-1]
    tok = lambda w_: pl.BlockSpec((bb, tl, w_), lambda i: (i, 0, 0))
    head = lambda w_: pl.BlockSpec((bb, MLA_HEADS, tl, w_), lambda i: (i, 0, 0, 0))
    shp = lambda w_: jax.ShapeDtypeStruct((b, MLA_HEADS, l, w_), F32)
    return pl.pallas_call(
        functools.partial(_qk_kernel, decode=True),
        grid=(b // bb,),
        in_specs=[tok(ql.shape[-1]), _resident(cos.shape), _resident(sin.shape),
                  _resident(wq.shape), _resident(gq.shape), _resident(gk.shape), _resident(wukt.shape)],
        out_specs=[head(kv), head(ROPE_DIM), head(ROPE_DIM)],
        out_shape=[shp(kv), shp(ROPE_DIM), shp(ROPE_DIM)],
        compiler_params=_cparams(("parallel",)),
        name="q_decode",
    )(ql, cos, sin, wq, gq, gk, wukt)


def _flash_kernel(qi_ref, ki_ref, q_ref, k_ref, vt_ref, o_ref, m_ref, l_ref, acc_ref):
    tq = q_ref.shape[2]
    tk = k_ref.shape[2]
    step_i = pl.program_id(1)
    qi, ki = qi_ref[step_i], ki_ref[step_i]

    @pl.when(ki == 0)
    def _():
        m_ref[...] = jnp.full(m_ref.shape, -jnp.inf, F32)
        l_ref[...] = jnp.zeros(l_ref.shape, F32)
        acc_ref[...] = jnp.zeros(acc_ref.shape, F32)

    vd = vt_ref.shape[2]

    def step(masked):
        if masked:
            keep = (ki * tk + lax.broadcasted_iota(jnp.int32, (tk, tq), 0)
                    <= qi * tq + lax.broadcasted_iota(jnp.int32, (tk, tq), 1))
        for hh in range(MLA_HEADS):
            st = _dot_nt(k_ref[0, hh], q_ref[0, hh])
            if masked:
                st = jnp.where(keep, st, -jnp.inf)
            m_prev = m_ref[hh:hh + 1, :]
            m_new = jnp.maximum(m_prev, jnp.max(st, axis=0, keepdims=True))
            alpha = jnp.exp2(m_prev - m_new)
            p = jnp.exp2(st - m_new)
            l_ref[hh:hh + 1, :] = alpha * l_ref[hh:hh + 1, :] + jnp.sum(p, axis=0, keepdims=True)
            rows = slice(hh * vd, (hh + 1) * vd)
            acc_ref[rows, :] = alpha * acc_ref[rows, :] + _dot(vt_ref[0, hh], p.astype(BF16))
            m_ref[hh:hh + 1, :] = m_new

    straddles = (ki + 1) * tk - 1 > qi * tq

    @pl.when(jnp.logical_not(straddles))
    def _():
        step(False)

    @pl.when(straddles)
    def _():
        step(True)

    @pl.when((ki + 1) * tk >= (qi + 1) * tq)
    def _():
        for hh in range(MLA_HEADS):
            rows = slice(hh * vd, (hh + 1) * vd)
            acc_ref[rows, :] = acc_ref[rows, :] / l_ref[hh:hh + 1, :]
        o_ref[0] = acc_ref[...].T


def _flash(q, k, vt):
    b, h, l, _ = q.shape
    t = min(l, ATTN_BLOCK)
    assert l % t == 0
    vd = vt.shape[2]
    n = l // t
    pairs = [(qi, ki) for qi in range(n) for ki in range(qi + 1)]
    qi_of = jnp.asarray([p[0] for p in pairs], jnp.int32)
    ki_of = jnp.asarray([p[1] for p in pairs], jnp.int32)
    grid_spec = pltpu.PrefetchScalarGridSpec(
        num_scalar_prefetch=2,
        grid=(b, len(pairs)),
        in_specs=[pl.BlockSpec((1, h, t, LANES), lambda i, s, qo, ko: (i, 0, qo[s], 0)),
                  pl.BlockSpec((1, h, t, LANES), lambda i, s, qo, ko: (i, 0, ko[s], 0)),
                  pl.BlockSpec((1, h, vd, t), lambda i, s, qo, ko: (i, 0, 0, ko[s]))],
        out_specs=pl.BlockSpec((1, t, h * vd), lambda i, s, qo, ko: (i, qo[s], 0)),
        scratch_shapes=[pltpu.VMEM((h, t), F32), pltpu.VMEM((h, t), F32), pltpu.VMEM((h * vd, t), F32)],
    )
    return pl.pallas_call(
        _flash_kernel,
        grid_spec=grid_spec,
        out_shape=jax.ShapeDtypeStruct((b, l, h * vd), F32),
        compiler_params=_cparams(("parallel", "arbitrary")),
        name="flash",
    )(qi_of, ki_of, q, k, vt)


def _decode_kernel(pt_ref, qt_ref, qab_ref, tab_ref, tabn_ref, new_ref, wukt_ref, wuv_ref, pool_ref,
                   o_ref, buf_ref, sem, lhs_ref, ctb_ref, a_ref, s_ref, m_ref, l_ref, acc_ref, *, pages, group, layer):
    seq, j = pl.program_id(0), pl.program_id(1)
    nseq, nj = pl.num_programs(0), pl.num_programs(1)
    nrow = qt_ref.shape[1]
    lq = nrow // MLA_HEADS
    kvw = qt_ref.shape[2]
    nkt = wukt_ref.shape[0]
    psz = buf_ref.shape[3]
    step = seq * nj + j
    slot = step % 2

    def page_copy(sq, chunk, pi, sl):
        return pltpu.make_async_copy(pool_ref.at[layer, pt_ref[sq, chunk * pages + pi]], buf_ref.at[sl, pi], sem.at[sl])

    @pl.when(step == 0)
    def _():
        for pi in range(pages):
            page_copy(seq, j, pi, slot).start()

    @pl.when(step + 1 < nseq * nj)
    def _():
        wrap = j + 1 == nj
        nxt_seq = jnp.where(wrap, seq + 1, seq)
        nxt_j = jnp.where(wrap, 0, j + 1)
        for pi in range(pages):
            page_copy(nxt_seq, nxt_j, pi, 1 - slot).start()

    for pi in range(pages):
        page_copy(seq, j, pi, slot).wait()
    page_refs = [buf_ref.at[slot, pi] for pi in range(pages)]

    @pl.when(j == 0)
    def _():
        m_ref[...] = jnp.full(m_ref.shape, -jnp.inf, F32)
        l_ref[...] = jnp.zeros(l_ref.shape, F32)
        acc_ref[...] = jnp.zeros(acc_ref.shape, F32)
        lhs_ref[0:nkt, :] = wukt_ref[...]
        lhs_ref[nkt:nkt + nrow, :] = qt_ref[0].astype(BF16)

    qab = qab_ref[0].astype(BF16)

    def scores(ct, krt, cct, snt):
        nk = ct.shape[1]
        ctb = ct.astype(BF16)
        a = _dot(lhs_ref[...], ctb)
        kt = a[0:nkt]
        ss = jnp.sum((kt * kt).reshape(MLA_HEADS, nkt // MLA_HEADS, nk), axis=1)
        ss = ss + jnp.sum(krt * krt, axis=0, keepdims=True)
        inv = lax.rsqrt(ss * (1.0 / QK_DIM) + EPS)
        feats = jnp.concatenate([krt * cct, krt * snt], axis=0).astype(BF16)
        s = a[nkt:nkt + nrow] + _dot(qab, feats)
        return (s.reshape(MLA_HEADS, lq, nk) * inv[:, None, :]).reshape(nrow, nk), ctb

    def update(s, ctb):
        m_prev = m_ref[...]
        m_new = jnp.maximum(m_prev, jnp.max(s, axis=-1, keepdims=True))
        alpha = jnp.exp2(m_prev - m_new)
        p = jnp.exp2(s - m_new)
        l_ref[...] = alpha * l_ref[...] + jnp.sum(p, axis=-1, keepdims=True)
        acc_ref[...] = alpha * acc_ref[...] + _dot_nt(p.astype(BF16), ctb)
        m_ref[...] = m_new

    for pi in range(pages):
        ctb_ref[:, pi * psz:(pi + 1) * psz] = page_refs[pi][0:kvw, :].astype(BF16)
    a_ref[...] = _dot(lhs_ref[...], ctb_ref[...])
    for g0 in range(0, pages, group):
        refs = page_refs[g0:g0 + group]
        lanes = slice(g0 * psz, (g0 + group) * psz)
        nk = group * psz
        kt = a_ref[0:nkt, lanes]
        krt = jnp.concatenate([r[kvw:kvw + ROPE_DIM, :] for r in refs], axis=1) if group > 1 else refs[0][kvw:kvw + ROPE_DIM, :]
        ss = jnp.sum((kt * kt).reshape(MLA_HEADS, nkt // MLA_HEADS, nk), axis=1)
        ss = ss + jnp.sum(krt * krt, axis=0, keepdims=True)
        inv = lax.rsqrt(ss * (1.0 / QK_DIM) + EPS)
        feats = jnp.concatenate([krt * tab_ref[0, :, lanes], krt * tab_ref[1, :, lanes]], axis=0).astype(BF16)
        s = a_ref[nkt:nkt + nrow, lanes] + _dot(qab, feats)
        s_ref[:, lanes] = (s.reshape(MLA_HEADS, lq, nk) * inv[:, None, :]).reshape(nrow, nk)
    update(s_ref[...], ctb_ref[...])

    @pl.when(j == pl.num_programs(1) - 1)
    def _():
        nk = new_ref.shape[2]
        tq = lax.broadcasted_iota(jnp.int32, (nrow, nk), 0) % lq
        tk = lax.broadcasted_iota(jnp.int32, (nrow, nk), 1)
        s, ctb = scores(new_ref[0, 0:kvw, :], new_ref[0, kvw:kvw + ROPE_DIM, :], tabn_ref[0], tabn_ref[1])
        update(jnp.where(tk <= tq, s, -jnp.inf), ctb)
        ctx = (acc_ref[...] / l_ref[...]).astype(BF16)
        vd = wuv_ref.shape[-1]
        for hh in range(MLA_HEADS):
            o_ref[0, :, hh * vd:(hh + 1) * vd] = _dot(ctx[hh * lq:(hh + 1) * lq, :], wuv_ref[hh])


def _decode(page_table, qt, qab, tab, tabn, rows_new_t, wukt, wuv, pool_t, layer):
    bs, nrow, kvw = qt.shape
    n_pages = page_table.shape[1]
    cw, psz = pool_t.shape[2], pool_t.shape[3]
    pages = math.gcd(n_pages, DECODE_PAGES)
    group = math.gcd(pages, DECODE_GROUP)
    lq = nrow // MLA_HEADS
    vd = wuv.shape[-1]
    per_s = lambda shp: pl.BlockSpec((1,) + shp, lambda s, j, pt: (s,) + (0,) * len(shp))
    grid_spec = pltpu.PrefetchScalarGridSpec(
        num_scalar_prefetch=1,
        grid=(bs, n_pages // pages),
        in_specs=[per_s((nrow, kvw)), per_s((nrow, 2 * ROPE_DIM)),
                  pl.BlockSpec((2, ROPE_DIM, pages * psz), lambda s, j, pt: (0, 0, j)),
                  _resident(tabn.shape), per_s(rows_new_t.shape[1:]),
                  _resident(wukt.shape), _resident(wuv.shape), pl.BlockSpec(memory_space=pl.ANY)],
        out_specs=per_s((lq, MLA_HEADS * vd)),
        scratch_shapes=[pltpu.VMEM((2, pages, cw, psz), F32), pltpu.SemaphoreType.DMA((2,)),
                        pltpu.VMEM((wukt.shape[0] + nrow, kvw), BF16),
                        pltpu.VMEM((kvw, pages * psz), BF16),
                        pltpu.VMEM((wukt.shape[0] + nrow, pages * psz), F32), pltpu.VMEM((nrow, pages * psz), F32),
                        pltpu.VMEM((nrow, 1), F32), pltpu.VMEM((nrow, 1), F32), pltpu.VMEM((nrow, kvw), F32)],
    )
    return pl.pallas_call(
        functools.partial(_decode_kernel, pages=pages, group=group, layer=layer),
        grid_spec=grid_spec,
        out_shape=jax.ShapeDtypeStruct((bs, lq, MLA_HEADS * vd), F32),
        compiler_params=_cparams(("arbitrary", "arbitrary")),
        name="decode",
    )(page_table, qt, qab, tab, tabn, rows_new_t, wukt, wuv, pool_t)


def _merge_kernel(h_ref, sh_ref, sc_ref, gt_ref, g_ref, ya_ref, yb_ref, yc_ref, yd_ref,
                  wg_ref, bg_ref, wb_ref, wo_ref, o_ref):
    bb, tl, d = h_ref.shape
    m = bb * tl
    h = h_ref[...]
    n = (_rms(h) * g_ref[...]) * (1.0 + sc_ref[...]) + sh_ref[...]
    n = n.reshape(m, d).astype(BF16)
    merged = None
    for r, y_ref in enumerate((ya_ref, yb_ref, yc_ref, yd_ref)):
        gate = jax.nn.sigmoid(_dot(n, wg_ref[r]) + bg_ref[r])
        term = gate * _dot(y_ref[...].reshape(m, y_ref.shape[-1]).astype(BF16), wb_ref[r])
        merged = term if merged is None else merged + term
    out = _dot(merged.astype(BF16), wo_ref[...])
    o_ref[...] = h + gt_ref[...] * out.reshape(bb, tl, d)


def _merge(h, sh, sc, gt, g, ys, wg, bg, wb, wo):
    b, l, d = h.shape
    bb, tl = _tok_blocks(b, l)
    tok = lambda w_: pl.BlockSpec((bb, tl, w_), lambda i, j: (i, j, 0))
    mod = pl.BlockSpec((bb, 1, d), lambda i, j: (i, 0, 0))
    return pl.pallas_call(
        _merge_kernel,
        grid=(b // bb, l // tl),
        in_specs=[tok(d), mod, mod, mod, _resident((1, d))] + [tok(y.shape[-1]) for y in ys]
                 + [_resident(wg.shape), _resident(bg.shape), _resident(wb.shape), _resident(wo.shape)],
        out_specs=tok(d),
        out_shape=jax.ShapeDtypeStruct(h.shape, F32),
        compiler_params=_cparams(("parallel", "parallel")),
        name="merge",
    )(h, sh, sc, gt, g, *ys, wg, bg, wb, wo)


def _rope_tables(pos):
    half = ROPE_DIM // 2
    inv = ROPE_THETA ** (-jnp.arange(half, dtype=F32) / half)
    ang = pos.astype(F32)[:, None] * inv
    return jnp.cos(ang), jnp.sin(ang)


def _lane_tables(pos):
    cos, sin = _rope_tables(pos)
    n = pos.shape[0]
    cos_t = jnp.concatenate([jnp.ones((n, NOPE_DIM), F32), cos, cos, jnp.zeros((n, LANES - QK_DIM), F32)], axis=1)
    sin_t = jnp.concatenate([jnp.zeros((n, NOPE_DIM), F32), -sin, sin, jnp.zeros((n, LANES - QK_DIM), F32)], axis=1)
    return cos_t, sin_t


def _pad_lanes(x, before, total):
    pad = [(0, 0)] * (x.ndim - 1) + [(before, total - before - x.shape[-1])]
    return jnp.pad(x, pad)


def _layer_weights(lw):
    d = lw["w_in"].shape[0]
    bd = lw["g_gm_v"].shape[0]
    q_lora, kv_lora = lw["g_q_lat"].shape[0], lw["g_kv_lat"].shape[0]
    layout, _ = _proj_layout(bd, q_lora, kv_lora)
    w = {}
    w["layout"] = layout
    w["w_ada"] = lw["w_ada"].astype(BF16)
    w["b_ada"] = lw["b_ada"][None, :]
    w["g_norm"] = [lw["g_norm"][i][None, :] for i in range(3)]
    w["w_ffn_in"] = [lw["w_ffn_in"][i].astype(BF16) for i in range(2)]
    w["w_ffn_out"] = [lw["w_ffn_out"][i].astype(BF16) for i in range(2)]
    w["w_proj"] = _proj_weight(lw["w_in"], bd, q_lora, kv_lora)
    w["g_gm_v"] = lw["g_gm_v"][None, :]
    w["g_q_lat"] = lw["g_q_lat"][None, :]
    w["g_kv_lat"] = lw["g_kv_lat"][None, :]
    w["w_spatial"] = lw["w_spatial"]
    w["b_spatial_t"] = lw["b_spatial"].T
    w["w_ssm_conv"] = lw["w_ssm_conv"]
    w["b_ssm_conv"] = lw["b_ssm_conv"][None, :]
    w["dt_bias"] = lw["dt_bias"][None, :]
    w["dt_bias_t"] = lw["dt_bias"][:, None]
    w["a_log"] = lw["a_log"][None, :]
    w["a_log_t"] = lw["a_log"][:, None]
    w["d_skip"] = jnp.repeat(lw["d_skip"], bd // SSM_HEADS)[None, :]
    w["g_ssm_norm"] = lw["g_ssm_norm"][None, :]
    w["w_sc_conv"] = lw["w_sc_conv"]
    w["wq"] = _pad_lanes(jnp.moveaxis(lw["w_uq"], 1, 0), 0, LANES).astype(BF16)
    wuk = jnp.moveaxis(lw["w_uk"], 1, 0)
    w["wk"] = _pad_lanes(wuk, 0, LANES).astype(BF16)
    wukt = jnp.swapaxes(wuk, 1, 2)
    w["wukt_pad"] = jnp.pad(wukt, ((0, 0), (0, LANES - NOPE_DIM), (0, 0))).astype(BF16)
    w["wukt_flat"] = wukt.reshape(MLA_HEADS * NOPE_DIM, kv_lora).astype(BF16)
    w["wuv"] = jnp.moveaxis(lw["w_uv"], 1, 0).astype(BF16)
    w["wuvt"] = jnp.swapaxes(w["wuv"], 1, 2)
    w["gq"] = _pad_lanes(lw["g_qk"][0][None, :], 0, LANES)
    w["gk"] = _pad_lanes(lw["g_qk"][1][None, :], 0, LANES)
    w["w_gate"] = lw["w_gate"].astype(BF16)
    w["b_gate"] = lw["b_gate"][:, None, :]
    w["w_branch_out"] = lw["w_branch_out"].astype(BF16)
    w["w_out"] = lw["w_out"].astype(BF16)
    return w


def _trunk_layer(x, mod, w, ssm_buf, ssm_s0, sc_buf, attn_fn):
    b, l, d = x.shape
    sh1, sc1, gt1, sh2, sc2, gt2, sh3, sc3, gt3 = [m[:, None, :] for m in jnp.split(mod, N_MOD, axis=-1)]
    h = _ffn(x, sh1, sc1, gt1, w["g_norm"][0], w["w_ffn_in"][0], w["w_ffn_out"][0])
    (a_u, a_v, b_z, b_xbc, c_p, c_b, ql, rows, cbf, krp, dtm, *ct) = _proj(
        h, sh2, sc2, w["g_norm"][1], w["w_proj"], w["g_gm_v"], w["g_q_lat"], w["g_kv_lat"], w["layout"])
    y_a = _gmlp(a_u, a_v, w["w_spatial"], w["b_spatial_t"])
    dtt = jnp.swapaxes(dtm[:, :, 0:SSM_HEADS], 1, 2)
    npair = SSM_HEADS // 2
    s0 = ssm_s0.reshape(b, npair, LANES, SSM_STATE)
    y_b, s_new = _ssm(b_xbc, b_z, dtm, dtt, ssm_buf, s0, w["w_ssm_conv"], w["b_ssm_conv"],
                      w["dt_bias"], w["dt_bias_t"], w["a_log"], w["a_log_t"], w["d_skip"], w["g_ssm_norm"])
    s_new = s_new.reshape(ssm_s0.shape)
    ssm_buf_new = jnp.concatenate([ssm_buf, b_xbc], axis=1)[:, -(SSM_CONV - 1):]
    y_c = _sconv(c_p, c_b, sc_buf, w["w_sc_conv"])
    sc_buf_new = jnp.concatenate([sc_buf, c_p], axis=1)[:, -(SC_WIDTH - 1):]
    y_d = attn_fn(ql, rows, cbf, krp, *ct)
    h = _merge(h, sh2, sc2, gt2, w["g_norm"][1], (y_a, y_b, y_c, y_d), w["w_gate"], w["b_gate"],
               w["w_branch_out"], w["w_out"])
    h = _ffn(h, sh3, sc3, gt3, w["g_norm"][2], w["w_ffn_in"][1], w["w_ffn_out"][1])
    return h, rows, ssm_buf_new, s_new, sc_buf_new, a_v


def kernel(x_prompt, x_sample, c_prompt, c_sample, cache_mla, page_table, state_ssm, state_ssm_conv,
           state_short_conv, w_ada, b_ada, g_norm, w_ffn_in, w_ffn_out, w_in, g_gm_v, w_spatial, b_spatial,
           w_ssm_conv, b_ssm_conv, dt_bias, a_log, d_skip, g_ssm_norm, w_sc_conv, g_q_lat, w_uq, g_kv_lat,
           w_uk, w_uv, g_qk, w_branch_out, w_gate, b_gate, w_out):
    params = dict(w_ada=w_ada, b_ada=b_ada, g_norm=g_norm, w_ffn_in=w_ffn_in, w_ffn_out=w_ffn_out, w_in=w_in,
                  g_gm_v=g_gm_v, w_spatial=w_spatial, b_spatial=b_spatial, w_ssm_conv=w_ssm_conv,
                  b_ssm_conv=b_ssm_conv, dt_bias=dt_bias, a_log=a_log, d_skip=d_skip, g_ssm_norm=g_ssm_norm,
                  w_sc_conv=w_sc_conv, g_q_lat=g_q_lat, w_uq=w_uq, g_kv_lat=g_kv_lat, w_uk=w_uk, w_uv=w_uv,
                  g_qk=g_qk, w_branch_out=w_branch_out, w_gate=w_gate, b_gate=b_gate, w_out=w_out)
    depth = w_ada.shape[0]
    bp, lp, d = x_prompt.shape
    bs, ls, _ = x_sample.shape
    page = cache_mla.shape[2]
    past = page_table.shape[1] * page
    bd = g_gm_v.shape[1]
    conv_ch = w_ssm_conv.shape[-1]

    pos_p = jnp.arange(lp, dtype=jnp.int32)
    pos_s = past + jnp.arange(ls, dtype=jnp.int32)
    cos_p, sin_p = _lane_tables(pos_p)
    cos_s, sin_s = _lane_tables(pos_s)
    def key_tables(pos, width):
        cos, sin = _rope_tables(pos)
        t = jnp.stack([jnp.concatenate([cos, cos], axis=1).T, jnp.concatenate([sin, sin], axis=1).T])
        return jnp.pad(t, ((0, 0), (0, 0), (0, width - pos.shape[0])))

    tab_k = key_tables(jnp.arange(past, dtype=jnp.int32), past)
    new_w = -(-ls // LANES) * LANES
    tab_n = key_tables(pos_s, new_w)
    pool_t = jnp.swapaxes(cache_mla, 2, 3)
    c_all = jnp.concatenate([c_prompt, c_sample], axis=0)

    yp, ys = x_prompt, x_sample
    outs = [[] for _ in range(9)]
    for layer in range(depth):
        w = _layer_weights({k: v[layer] for k, v in params.items()})
        mod = _ada(c_all, w["w_ada"], w["b_ada"])

        def attn_prompt(ql, rows, cbf, krp, ct, w=w):
            q, k, vt = _qk_prompt(ql, cbf, krp, ct, cos_p, sin_p, w["wq"], w["wk"], w["wuvt"], w["gq"], w["gk"])
            return _flash(q, k, vt)

        def attn_sample(ql, rows, cbf, krp, *unused, w=w, layer=layer):
            qt, qa, qb = _q_decode(ql, cos_s, sin_s, w["wq"], w["gq"], w["gk"], w["wukt_pad"])
            flat = lambda t: t.reshape(bs, MLA_HEADS * ls, t.shape[-1])
            rows_t = jnp.pad(jnp.swapaxes(rows, 1, 2), ((0, 0), (0, 0), (0, new_w - ls)))
            return _decode(page_table, flat(qt), flat(jnp.concatenate([qa, qb], axis=-1)), tab_k, tab_n, rows_t,
                           w["wukt_flat"], w["wuv"], pool_t, layer)

        yp, r, cb, s, scb, _ = _trunk_layer(
            yp, mod[:bp], w,
            jnp.zeros((bp, SSM_CONV - 1, conv_ch), F32),
            jnp.zeros((bp, SSM_HEADS, bd // SSM_HEADS, SSM_STATE), F32),
            jnp.zeros((bp, SC_WIDTH - 1, bd), F32), attn_prompt)
        for lst, v in zip((outs[0], outs[2], outs[4], outs[6]), (r, s, cb, scb)):
            lst.append(v)
        ys, r, cb, s, scb, v = _trunk_layer(
            ys, mod[bp:], w, state_ssm_conv[layer], state_ssm[layer], state_short_conv[layer], attn_sample)
        for lst, val in zip((outs[1], outs[3], outs[5], outs[7], outs[8]), (r, s, cb, scb, v)):
            lst.append(val)
    st = jnp.stack
    return (yp, ys, st(outs[0]), st(outs[1]), st(outs[2]), st(outs[3]), st(outs[4]), st(outs[5]),
            st(outs[6]), st(outs[7]), st(outs[8]))
```

```python
import functools
import math

import jax
import jax.numpy as jnp
from jax import lax
from jax.experimental import pallas as pl
from jax.experimental.pallas import tpu as pltpu

F32 = jnp.float32
BF16 = jnp.bfloat16
HIGHEST = lax.Precision.HIGHEST

EPS = 1e-6
LANES = 128
N_MOD = 9
GM_CHUNK = 128
GM_HEADS = 4
SSM_HEADS = 8
SSM_GROUPS = 2
SSM_STATE = 128
SSM_CHUNK = 128
SSM_CONV = 4
SC_WIDTH = 3
MLA_HEADS = 8
NOPE_DIM = 64
ROPE_DIM = 32
QK_DIM = NOPE_DIM + ROPE_DIM
ROPE_THETA = 10000.0
QK_SCALE = 1.0 / math.sqrt(QK_DIM)
LOG2E = math.log2(math.e)
ROW_BLOCK = 512
FFN_ROW_BLOCK = 512
ATTN_Q_BLOCK = 512
ATTN_K_BLOCK = 512
DECODE_PAGES = 16
DECODE_GROUP = 4
VMEM_LIMIT = 56 * 1024 * 1024


def _cparams(sem):
    return pltpu.CompilerParams(dimension_semantics=sem, vmem_limit_bytes=VMEM_LIMIT)


def _resident(shape):
    nd = len(shape)
    return pl.BlockSpec(shape, lambda *_: (0,) * nd, pipeline_mode=pl.Buffered(1))


def _tok_blocks(b, l, rows=ROW_BLOCK):
    tl = min(l, rows)
    bb = min(b, max(1, rows // tl))
    assert l % tl == 0 and b % bb == 0 and tl % 8 == 0
    return bb, tl


def _silu(x):
    return x * jax.nn.sigmoid(x)


def _rms(x):
    return x * lax.rsqrt(jnp.mean(x * x, axis=-1, keepdims=True) + EPS)


def _dot(a, b):
    return jnp.dot(a, b, preferred_element_type=F32)


def _dot_nt(a, b, precision=None):
    return lax.dot_general(a, b, (((1,), (1,)), ((), ())), preferred_element_type=F32, precision=precision)


def _ada_kernel(c_ref, w_ref, b_ref, o_ref):
    o_ref[...] = _dot(_silu(c_ref[...]).astype(BF16), w_ref[...]) + b_ref[...]


def _ada(c, w, b):
    bt, d = c.shape
    n = w.shape[1]
    return pl.pallas_call(
        _ada_kernel,
        grid=(n // d,),
        in_specs=[pl.BlockSpec((bt, d), lambda j: (0, 0)),
                  pl.BlockSpec((d, d), lambda j: (0, j)),
                  pl.BlockSpec((1, d), lambda j: (0, j))],
        out_specs=pl.BlockSpec((bt, d), lambda j: (0, j)),
        out_shape=jax.ShapeDtypeStruct((bt, n), F32),
        compiler_params=_cparams(("arbitrary",)),
        name="ada",
    )(c, w, b)


def _ffn_kernel(x_ref, sh_ref, sc_ref, gt_ref, g_ref, win_ref, wout_ref, o_ref, acc_ref, *, tf):
    bb, tl, d = x_ref.shape
    f = wout_ref.shape[0]
    x = x_ref[...]
    xn = (_rms(x) * g_ref[...]) * (1.0 + sc_ref[...]) + sh_ref[...]
    xn = xn.reshape(bb * tl, d).astype(BF16)
    for i in range(f // tf):
        g = _dot(xn, win_ref[:, i * tf:(i + 1) * tf])
        u = _dot(xn, win_ref[:, f + i * tf:f + (i + 1) * tf])
        a = (_silu(g) * u).astype(BF16)
        part = _dot(a, wout_ref[i * tf:(i + 1) * tf, :])
        if i == 0:
            acc_ref[...] = part
        else:
            acc_ref[...] += part
    o_ref[...] = x + 0.5 * gt_ref[...] * acc_ref[...].reshape(bb, tl, d)


def _ffn(x, sh, sc, gt, g, w_in, w_out):
    b, l, d = x.shape
    bb, tl = _tok_blocks(b, l, FFN_ROW_BLOCK)
    f = w_out.shape[0]
    tf = 256 if f % 256 == 0 else LANES
    tok = pl.BlockSpec((bb, tl, d), lambda i, j: (i, j, 0))
    mod = pl.BlockSpec((bb, 1, d), lambda i, j: (i, 0, 0))
    return pl.pallas_call(
        functools.partial(_ffn_kernel, tf=tf),
        grid=(b // bb, l // tl),
        in_specs=[tok, mod, mod, mod, _resident((1, d)), _resident(w_in.shape), _resident(w_out.shape)],
        out_specs=tok,
        out_shape=jax.ShapeDtypeStruct(x.shape, F32),
        scratch_shapes=[pltpu.VMEM((bb * tl, d), F32)],
        compiler_params=_cparams(("parallel", "parallel")),
        name="ffn",
    )(x, sh, sc, gt, g, w_in, w_out)


_PROJ_GROUPS = ("a_u", "a_v", "b_z", "b_xbc", "c_h", "c_b", "c_c", "d_q", "d_kv", "kr0", "krp", "dt")


def _proj_layout(bd, q_lora, kv_lora):
    widths = dict(a_u=bd, a_v=bd, b_z=bd, b_xbc=2 * bd, c_h=bd, c_b=bd, c_c=bd, d_q=q_lora, d_kv=kv_lora,
                  kr0=LANES, krp=LANES, dt=LANES)
    off, layout = 0, {}
    for name in _PROJ_GROUPS:
        layout[name] = (off, widths[name])
        off += widths[name]
    return layout, off


def _proj_weight(w_in, bd, q_lora, kv_lora):
    d = w_in.shape[0]
    sizes = (bd, bd, bd, 2 * bd, SSM_HEADS, bd, bd, bd, q_lora, kv_lora, ROPE_DIM)
    a_u, a_v, b_z, b_xbc, b_dt, c_h, c_b, c_c, d_q, d_kv, d_kr = jnp.split(w_in, _cumsum(sizes)[:-1], axis=1)
    z = lambda n: jnp.zeros((d, n), w_in.dtype)
    kr0 = jnp.concatenate([d_kr, z(LANES - ROPE_DIM)], axis=1)
    krp = jnp.concatenate([z(NOPE_DIM), d_kr, z(LANES - QK_DIM)], axis=1)
    dt = jnp.concatenate([b_dt, z(LANES - SSM_HEADS)], axis=1)
    return jnp.concatenate([a_u, a_v, b_z, b_xbc, c_h, c_b, c_c, d_q, d_kv, kr0, krp, dt], axis=1).astype(BF16)


def _cumsum(sizes):
    out, s = [], 0
    for v in sizes:
        s += v
        out.append(s)
    return out


def _proj_kernel(h_ref, sh_ref, sc_ref, g_ref, w_ref, ggm_ref, gq_ref, gkv_ref, ws_ref, bst_ref, scbuf_ref, wsc_ref,
                 ya_ref, av_ref, z_ref, xbc_ref, yc_ref, ptail_ref, ql_ref, rows_ref, cbf_ref, krp_ref, dt_ref,
                 *rest, layout, chunk):
    *maybe_ct_ref, hist_ref = rest
    bb, tl, d = h_ref.shape
    n = (_rms(h_ref[...]) * g_ref[...]) * (1.0 + sc_ref[...]) + sh_ref[...]
    n = n.reshape(bb * tl, d).astype(BF16)

    def grp(name):
        off, w = layout[name]
        return _dot(n, w_ref[:, off:off + w])

    def put(ref, val):
        ref[...] = val.reshape(bb, tl, val.shape[-1]).astype(ref.dtype)

    a_u = jax.nn.gelu(grp("a_u"))
    a_v = _rms(jax.nn.gelu(grp("a_v"))) * ggm_ref[...]
    put(av_ref, a_v)
    bd = a_u.shape[-1]
    hd = bd // GM_HEADS
    row = lax.broadcasted_iota(jnp.int32, (chunk, chunk), 0)
    col = lax.broadcasted_iota(jnp.int32, (chunk, chunk), 1)
    for hh in range(GM_HEADS):
        wm = jnp.where(col <= row, ws_ref[hh, 0:chunk, 0:chunk], 0.0)
        bias = bst_ref[0:chunk, hh:hh + 1]
        cols = slice(hh * hd, (hh + 1) * hd)
        if chunk == GM_CHUNK:
            wmb = wm.astype(BF16)
            for ci in range(tl // chunk):
                rows = slice(ci * chunk, (ci + 1) * chunk)
                mixed = _dot(wmb, a_v[rows, cols].astype(BF16)) + bias
                ya_ref[0, rows, cols] = a_u[rows, cols] * mixed
        else:
            u3 = a_u[:, cols].reshape(bb, tl, hd)
            v3 = a_v[:, cols].reshape(bb, tl, hd)
            mixed = jnp.zeros((bb, chunk, hd), F32) + bias
            for j in range(chunk):
                mixed = mixed + wm[:, j:j + 1] * v3[:, j:j + 1, :]
            ya_ref[:, :, cols] = u3 * mixed

    kc = wsc_ref.shape[0]

    @pl.when(pl.program_id(1) == 0)
    def _():
        hist_ref[:, 0:8, :] = jnp.zeros((bb, 8, bd), F32)
        hist_ref[:, 8 - (kc - 1):8, :] = scbuf_ref[...]

    @pl.when(pl.program_id(1) > 0)
    def _():
        hist_ref[:, 0:8, :] = hist_ref[:, tl:tl + 8, :]

    hist_ref[:, 8:8 + tl, :] = (grp("c_c") * grp("c_h")).reshape(bb, tl, bd)
    acc = jnp.zeros((bb, tl, bd), F32)
    for i in range(kc):
        s = 8 - (kc - 1) + i
        acc = acc + wsc_ref[i:i + 1, :] * hist_ref[:, s:s + tl, :]
    yc_ref[...] = grp("c_b").reshape(bb, tl, bd) * acc
    ptail_ref[...] = hist_ref[:, tl:tl + 8, :]

    put(z_ref, grp("b_z"))
    put(xbc_ref, grp("b_xbc"))
    put(ql_ref, _rms(grp("d_q")) * gq_ref[...])
    kv = _rms(grp("d_kv")) * gkv_ref[...]
    put(cbf_ref, kv)
    if maybe_ct_ref:
        maybe_ct_ref[0][0] = kv.T.astype(BF16)
    kvw = kv.shape[-1]
    rows_ref[:, :, 0:kvw] = kv.reshape(bb, tl, kvw)
    rows_ref[:, :, kvw:kvw + ROPE_DIM] = grp("kr0")[:, 0:ROPE_DIM].reshape(bb, tl, ROPE_DIM)
    put(krp_ref, grp("krp"))
    put(dt_ref, grp("dt"))


def _proj(h, sh, sc, g, w, ggm, gq, gkv, w_s, b_st, sc_buf, w_sc, layout):
    b, l, d = h.shape
    bb, tl = _tok_blocks(b, l)
    bd, q_lora, kv_lora = ggm.shape[1], gq.shape[1], gkv.shape[1]
    chunk = min(l, GM_CHUNK)
    assert (chunk == GM_CHUNK and bb == 1 and tl % chunk == 0) or chunk == tl
    tok = lambda w_: pl.BlockSpec((bb, tl, w_), lambda i, j: (i, j, 0))
    mod = pl.BlockSpec((bb, 1, d), lambda i, j: (i, 0, 0))
    per_b = lambda r: pl.BlockSpec((bb, r, bd), lambda i, j: (i, 0, 0))
    outs = [(bd, F32), (bd, F32), (bd, F32), (2 * bd, F32), (bd, F32), None, (q_lora, BF16),
            (kv_lora + ROPE_DIM, F32), (kv_lora, BF16), (LANES, F32), (LANES, F32)]
    out_specs = [per_b(8) if o is None else tok(o[0]) for o in outs]
    out_shape = [jax.ShapeDtypeStruct((b, 8, bd), F32) if o is None else jax.ShapeDtypeStruct((b, l, o[0]), o[1])
                 for o in outs]
    if bb == 1 and tl % LANES == 0:
        out_specs.append(pl.BlockSpec((1, kv_lora, tl), lambda i, j: (i, 0, j)))
        out_shape.append(jax.ShapeDtypeStruct((b, kv_lora, l), BF16))
    return pl.pallas_call(
        functools.partial(_proj_kernel, layout=layout, chunk=chunk),
        grid=(b // bb, l // tl),
        in_specs=[tok(d), mod, mod, _resident((1, d)), _resident(w.shape),
                  _resident(ggm.shape), _resident(gq.shape), _resident(gkv.shape),
                  _resident(w_s.shape), _resident(b_st.shape), per_b(sc_buf.shape[1]), _resident(w_sc.shape)],
        out_specs=out_specs,
        out_shape=out_shape,
        scratch_shapes=[pltpu.VMEM((bb, 8 + tl, bd), F32)],
        compiler_params=_cparams(("parallel", "arbitrary")),
        name="proj",
    )(h, sh, sc, g, w, ggm, gq, gkv, w_s, b_st, sc_buf, w_sc)


def _softplus(x):
    return jnp.maximum(x, 0.0) + jnp.log1p(jnp.exp(-jnp.abs(x)))


def _ssm_kernel(xbc_ref, z_ref, dtm_ref, dtt_ref, buf_ref, s0_ref, wc_ref, bc_ref, dtb_ref, dtbt_ref,
                alog_ref, alogt_ref, dsk_ref, gn_ref, y_ref, sout_ref, hist_ref, s_ref):
    q = xbc_ref.shape[1]
    c = xbc_ref.shape[-1]
    bd = z_ref.shape[-1]
    k = wc_ref.shape[0]
    ci = pl.program_id(1)

    @pl.when(ci == 0)
    def _():
        hist_ref[0:8, :] = jnp.zeros((8, c), F32)
        hist_ref[8 - (k - 1):8, :] = buf_ref[0]
        s_ref[...] = s0_ref[0]

    @pl.when(ci > 0)
    def _():
        hist_ref[0:8, :] = hist_ref[q:q + 8, :]

    hist_ref[8:8 + q, :] = xbc_ref[0]
    acc = jnp.zeros((q, c), F32) + bc_ref[...]
    for i in range(k):
        s = 8 - (k - 1) + i
        acc = acc + wc_ref[i:i + 1, :] * hist_ref[s:s + q, :]
    xc = _silu(acc)
    xs = xc[:, 0:bd]
    gw = SSM_STATE
    bmf = [xc[:, bd + g * gw:bd + (g + 1) * gw] for g in range(SSM_GROUPS)]
    bm = [t.astype(BF16) for t in bmf]
    cm = [xc[:, bd + (SSM_GROUPS + g) * gw:bd + (SSM_GROUPS + g + 1) * gw].astype(BF16) for g in range(SSM_GROUPS)]

    row = lax.broadcasted_iota(jnp.int32, (q, q), 0)
    col = lax.broadcasted_iota(jnp.int32, (q, q), 1)
    causal = col <= row
    dt = _softplus(dtm_ref[0][:, 0:SSM_HEADS] + dtb_ref[...])
    dtt = _softplus(dtt_ref[0] + dtbt_ref[...])
    da = dt * (-jnp.exp(alog_ref[...]) * LOG2E)
    dat = dtt * (-jnp.exp(alogt_ref[...]) * LOG2E)
    cum = jnp.dot(causal.astype(F32), da, preferred_element_type=F32, precision=HIGHEST)
    cumt = jnp.dot(dat, (row <= col).astype(F32), preferred_element_type=F32, precision=HIGHEST)
    last = cum[q - 1:q, :]

    lane = lax.broadcasted_iota(jnp.int32, (q, LANES), 1)
    srow = lax.broadcasted_iota(jnp.int32, (LANES, gw), 0)
    p_dim = bd // SSM_HEADS
    rep = SSM_HEADS // SSM_GROUPS
    cbs = [_dot_nt(cm[g], bm[g]) for g in range(SSM_GROUPS)]
    for pair in range(SSM_HEADS // 2):
        h0, h1 = 2 * pair, 2 * pair + 1
        g = h0 // rep
        first = lane < p_dim
        cum_pair = jnp.where(first, cum[:, h0:h0 + 1], cum[:, h1:h1 + 1])
        last_pair = jnp.where(first, last[:, h0:h0 + 1], last[:, h1:h1 + 1])
        dt_pair = jnp.where(first, dt[:, h0:h0 + 1], dt[:, h1:h1 + 1])
        xdt = xs[:, pair * LANES:(pair + 1) * LANES] * dt_pair
        xb = xdt.astype(BF16)
        ys = []
        for hh in (h0, h1):
            dec = jnp.where(causal, jnp.exp2(cum[:, hh:hh + 1] - cumt[hh:hh + 1, :]), 0.0)
            ys.append(_dot((cbs[g] * dec).astype(BF16), xb))
        s_in = s_ref[pair]
        y_pair = jnp.where(first, ys[0], ys[1]) + jnp.exp2(cum_pair) * _dot_nt(cm[g], s_in.astype(BF16))
        y_ref[0, :, pair * LANES:(pair + 1) * LANES] = y_pair
        xw = xdt * jnp.exp2(last_pair - cum_pair)
        bmg = bm[g]
        if q < LANES:
            zpad = jnp.zeros((LANES - q, LANES), F32)
            xw = jnp.concatenate([xw, zpad], axis=0)
            bmg = jnp.concatenate([bmf[g], zpad], axis=0).astype(BF16)
        cs = _dot(xw.T.astype(BF16), bmg)
        cd = jnp.where(srow < p_dim, jnp.exp2(last[:, h0:h0 + 1]), jnp.exp2(last[:, h1:h1 + 1]))
        s_ref[pair] = s_in * cd + cs

    y = (y_ref[0] + dsk_ref[...] * xs) * _silu(z_ref[0])
    gs = bd // SSM_GROUPS
    for g in range(SSM_GROUPS):
        y_ref[0, :, g * gs:(g + 1) * gs] = _rms(y[:, g * gs:(g + 1) * gs]) * gn_ref[:, g * gs:(g + 1) * gs]
    sout_ref[0] = s_ref[...]


def _ssm(xbc, z, dtm, dtt, buf, s0, wc, bc, dtb, dtbt, alog, alogt, dsk, gn):
    b, l, c = xbc.shape
    bd = z.shape[-1]
    q = min(l, SSM_CHUNK)
    assert SSM_STATE == LANES and l % q == 0 and q % 8 == 0 and SSM_HEADS % 2 == 0 and bd // SSM_HEADS * 2 == LANES and (SSM_HEADS // SSM_GROUPS) % 2 == 0
    npair = SSM_HEADS // 2
    tok = lambda w_: pl.BlockSpec((1, q, w_), lambda i, j: (i, j, 0))
    per_b = lambda shp: pl.BlockSpec((1,) + shp, lambda i, j: (i,) + (0,) * len(shp))
    small = [wc, bc, dtb, dtbt, alog, alogt, dsk, gn]
    return pl.pallas_call(
        _ssm_kernel,
        grid=(b, l // q),
        in_specs=[tok(c), tok(bd), tok(LANES), pl.BlockSpec((1, SSM_HEADS, q), lambda i, j: (i, 0, j)),
                  per_b(buf.shape[1:]), per_b(s0.shape[1:])] + [_resident(a.shape) for a in small],
        out_specs=[tok(bd), per_b(s0.shape[1:])],
        out_shape=[jax.ShapeDtypeStruct((b, l, bd), F32), jax.ShapeDtypeStruct(s0.shape, F32)],
        scratch_shapes=[pltpu.VMEM((8 + q, c), F32), pltpu.VMEM((npair, LANES, SSM_STATE), F32)],
        compiler_params=_cparams(("parallel", "arbitrary")),
        name="ssm",
    )(xbc, z, dtm, dtt, buf, s0, *small)


def _rope_swap(x, lane, sign):
    half = ROPE_DIM // 2
    return jnp.where(lane < NOPE_DIM + half, pltpu.roll(x, LANES - half, 1), sign * pltpu.roll(x, half, 1))


def _qk_kernel(*refs, decode):
    if decode:
        (ql_ref, cos_ref, sin_ref, wq_ref, gq_ref, gk_ref, wukt_ref, qt_ref, qa_ref, qb_ref) = refs
    else:
        (ql_ref, cbf_ref, krp_ref, ct_ref, cos_ref, sin_ref, wq_ref, wk_ref, wuvt_ref, gq_ref, gk_ref,
         q_ref, k_ref, vt_ref) = refs
    bb, tl, _ = ql_ref.shape
    m = bb * tl
    ql = ql_ref[...].reshape(m, ql_ref.shape[-1])
    lane = lax.broadcasted_iota(jnp.int32, (m, LANES), 1)
    cos = jnp.broadcast_to(cos_ref[...][None], (bb, tl, LANES)).reshape(m, LANES)
    sin = jnp.broadcast_to(sin_ref[...][None], (bb, tl, LANES)).reshape(m, LANES)

    def norm_rope(x, g):
        xn = x * lax.rsqrt(jnp.sum(x * x, axis=-1, keepdims=True) * (1.0 / QK_DIM) + EPS) * g
        return xn * cos + _rope_swap(xn, lane, 1.0) * sin

    if not decode:
        cbf = cbf_ref[...].reshape(m, cbf_ref.shape[-1])
        krp = krp_ref[...].reshape(m, LANES)
    for hh in range(MLA_HEADS):
        qf = norm_rope(_dot(ql, wq_ref[hh]), gq_ref[...]) * (QK_SCALE * LOG2E)
        if decode:
            gk = gk_ref[...]
            nope = lane < NOPE_DIM
            qn = jnp.where(nope, qf * gk, 0.0).astype(BF16)
            qt_ref[:, hh] = _dot(qn, wukt_ref[hh]).reshape(bb, tl, wukt_ref.shape[-1])
            qa = qf * gk
            qb = _rope_swap(qf, lane, -1.0) * gk
            qa_ref[:, hh] = qa[:, NOPE_DIM:QK_DIM].reshape(bb, tl, ROPE_DIM)
            qb_ref[:, hh] = qb[:, NOPE_DIM:QK_DIM].reshape(bb, tl, ROPE_DIM)
        else:
            q_ref[:, hh] = qf.reshape(bb, tl, LANES).astype(BF16)
            kf = norm_rope(_dot(cbf, wk_ref[hh]) + krp, gk_ref[...])
            k_ref[:, hh] = kf.reshape(bb, tl, LANES).astype(BF16)
            vt_ref[0, hh] = _dot(wuvt_ref[hh], ct_ref[0]).astype(BF16)


def _qk_prompt(ql, cbf, krp, ct, cos, sin, wq, wk, wuvt, gq, gk):
    b, l, _ = ql.shape
    bb, tl = _tok_blocks(b, l)
    assert bb == 1
    vd = wuvt.shape[1]
    tok = lambda w_: pl.BlockSpec((bb, tl, w_), lambda i, j: (i, j, 0))
    tab = pl.BlockSpec((tl, LANES), lambda i, j: (j, 0))
    head = pl.BlockSpec((bb, MLA_HEADS, tl, LANES), lambda i, j: (i, 0, j, 0))
    shp = jax.ShapeDtypeStruct((b, MLA_HEADS, l, LANES), BF16)
    return pl.pallas_call(
        functools.partial(_qk_kernel, decode=False),
        grid=(b // bb, l // tl),
        in_specs=[tok(ql.shape[-1]), tok(cbf.shape[-1]), tok(LANES),
                  pl.BlockSpec((1, ct.shape[1], tl), lambda i, j: (i, 0, j)), tab, tab,
                  _resident(wq.shape), _resident(wk.shape), _resident(wuvt.shape),
                  _resident(gq.shape), _resident(gk.shape)],
        out_specs=[head, head, pl.BlockSpec((1, MLA_HEADS, vd, tl), lambda i, j: (i, 0, 0, j))],
        out_shape=[shp, shp, jax.ShapeDtypeStruct((b, MLA_HEADS, vd, l), BF16)],
        compiler_params=_cparams(("parallel", "parallel")),
        name="qk_prompt",
    )(ql, cbf, krp, ct, cos, sin, wq, wk, wuvt, gq, gk)


def _q_decode(ql, cos, sin, wq, gq, gk, wukt):
    b, l, _ = ql.shape
    bb, tl = _tok_blocks(b, l)
    assert tl == l
    kv = wukt.shape[-1]
    tok = lambda w_: pl.BlockSpec((bb, tl, w_), lambda i: (i, 0, 0))
    head = lambda w_: pl.BlockSpec((bb, MLA_HEADS, tl, w_), lambda i: (i, 0, 0, 0))
    shp = lambda w_: jax.ShapeDtypeStruct((b, MLA_HEADS, l, w_), F32)
    return pl.pallas_call(
        functools.partial(_qk_kernel, decode=True),
        grid=(b // bb,),
        in_specs=[tok(ql.shape[-1]), _resident(cos.shape), _resident(sin.shape),
                  _resident(wq.shape), _resident(gq.shape), _resident(gk.shape), _resident(wukt.shape)],
        out_specs=[head(kv), head(ROPE_DIM), head(ROPE_DIM)],
        out_shape=[shp(kv), shp(ROPE_DIM), shp(ROPE_DIM)],
        compiler_params=_cparams(("parallel",)),
        name="q_decode",
    )(ql, cos, sin, wq, gq, gk, wukt)


def _flash_kernel(qi_ref, ki_ref, q_ref, k_ref, vt_ref, o_ref, m_ref, l_ref, acc_ref):
    tq = q_ref.shape[2]
    tk = k_ref.shape[2]
    step_i = pl.program_id(1)
    qi, ki = qi_ref[step_i], ki_ref[step_i]

    @pl.when(ki == 0)
    def _():
        m_ref[...] = jnp.full(m_ref.shape, -jnp.inf, F32)
        l_ref[...] = jnp.zeros(l_ref.shape, F32)
        acc_ref[...] = jnp.zeros(acc_ref.shape, F32)

    vd = vt_ref.shape[2]

    def step(masked):
        if masked:
            keep = (ki * tk + lax.broadcasted_iota(jnp.int32, (tk, tq), 0)
                    <= qi * tq + lax.broadcasted_iota(jnp.int32, (tk, tq), 1))
        for hh in range(MLA_HEADS):
            st = _dot_nt(k_ref[0, hh], q_ref[0, hh])
            if masked:
                st = jnp.where(keep, st, -jnp.inf)
            m_prev = m_ref[hh:hh + 1, :]
            m_new = jnp.maximum(m_prev, jnp.max(st, axis=0, keepdims=True))
            alpha = jnp.exp2(m_prev - m_new)
            p = jnp.exp2(st - m_new)
            l_ref[hh:hh + 1, :] = alpha * l_ref[hh:hh + 1, :] + jnp.sum(p, axis=0, keepdims=True)
            rows = slice(hh * vd, (hh + 1) * vd)
            acc_ref[rows, :] = alpha * acc_ref[rows, :] + _dot(vt_ref[0, hh], p.astype(BF16))
            m_ref[hh:hh + 1, :] = m_new

    straddles = (ki + 1) * tk - 1 > qi * tq

    @pl.when(jnp.logical_not(straddles))
    def _():
        step(False)

    @pl.when(straddles)
    def _():
        step(True)

    @pl.when((ki + 1) * tk >= (qi + 1) * tq)
    def _():
        for hh in range(MLA_HEADS):
            rows = slice(hh * vd, (hh + 1) * vd)
            acc_ref[rows, :] = acc_ref[rows, :] / l_ref[hh:hh + 1, :]
        o_ref[0] = acc_ref[...].T


def _flash(q, k, vt):
    b, h, l, _ = q.shape
    tq, tk = min(l, ATTN_Q_BLOCK), min(l, ATTN_K_BLOCK)
    assert l % tq == 0 and l % tk == 0
    vd = vt.shape[2]
    pairs = [(qi, ki) for qi in range(l // tq) for ki in range(l // tk) if ki * tk <= qi * tq + tq - 1]
    qi_of = jnp.asarray([p[0] for p in pairs], jnp.int32)
    ki_of = jnp.asarray([p[1] for p in pairs], jnp.int32)
    grid_spec = pltpu.PrefetchScalarGridSpec(
        num_scalar_prefetch=2,
        grid=(b, len(pairs)),
        in_specs=[pl.BlockSpec((1, h, tq, LANES), lambda i, s, qo, ko: (i, 0, qo[s], 0)),
                  pl.BlockSpec((1, h, tk, LANES), lambda i, s, qo, ko: (i, 0, ko[s], 0)),
                  pl.BlockSpec((1, h, vd, tk), lambda i, s, qo, ko: (i, 0, 0, ko[s]))],
        out_specs=pl.BlockSpec((1, tq, h * vd), lambda i, s, qo, ko: (i, qo[s], 0)),
        scratch_shapes=[pltpu.VMEM((h, tq), F32), pltpu.VMEM((h, tq), F32), pltpu.VMEM((h * vd, tq), F32)],
    )
    return pl.pallas_call(
        _flash_kernel,
        grid_spec=grid_spec,
        out_shape=jax.ShapeDtypeStruct((b, l, h * vd), F32),
        compiler_params=_cparams(("parallel", "arbitrary")),
        name="flash",
    )(qi_of, ki_of, q, k, vt)


def _decode_kernel(pt_ref, qt_ref, qab_ref, tab_ref, tabn_ref, new_ref, wukt_ref, wuv_ref, pool_ref,
                   o_ref, buf_ref, sem, lhs_ref, ctb_ref, a_ref, s_ref, m_ref, l_ref, acc_ref, *, pages, group, layer):
    seq, j = pl.program_id(0), pl.program_id(1)
    nseq, nj = pl.num_programs(0), pl.num_programs(1)
    nrow = qt_ref.shape[1]
    lq = nrow // MLA_HEADS
    kvw = qt_ref.shape[2]
    nkt = wukt_ref.shape[0]
    psz = buf_ref.shape[3]
    step = seq * nj + j
    slot = step % 2

    def page_copy(sq, chunk, pi, sl):
        return pltpu.make_async_copy(pool_ref.at[layer, pt_ref[sq, chunk * pages + pi]], buf_ref.at[sl, pi], sem.at[sl])

    @pl.when(step == 0)
    def _():
        for pi in range(pages):
            page_copy(seq, j, pi, slot).start()

    @pl.when(step + 1 < nseq * nj)
    def _():
        wrap = j + 1 == nj
        nxt_seq = jnp.where(wrap, seq + 1, seq)
        nxt_j = jnp.where(wrap, 0, j + 1)
        for pi in range(pages):
            page_copy(nxt_seq, nxt_j, pi, 1 - slot).start()

    for pi in range(pages):
        page_copy(seq, j, pi, slot).wait()
    page_refs = [buf_ref.at[slot, pi] for pi in range(pages)]

    @pl.when(j == 0)
    def _():
        m_ref[...] = jnp.full(m_ref.shape, -jnp.inf, F32)
        l_ref[...] = jnp.zeros(l_ref.shape, F32)
        acc_ref[...] = jnp.zeros(acc_ref.shape, F32)
        lhs_ref[0:nkt, :] = wukt_ref[...]
        lhs_ref[nkt:nkt + nrow, :] = qt_ref[0].astype(BF16)

    qab = qab_ref[0].astype(BF16)

    def scores(ct, krt, cct, snt):
        nk = ct.shape[1]
        ctb = ct.astype(BF16)
        a = _dot(lhs_ref[...], ctb)
        kt = a[0:nkt]
        ss = jnp.sum((kt * kt).reshape(MLA_HEADS, nkt // MLA_HEADS, nk), axis=1)
        ss = ss + jnp.sum(krt * krt, axis=0, keepdims=True)
        inv = lax.rsqrt(ss * (1.0 / QK_DIM) + EPS)
        feats = jnp.concatenate([krt * cct, krt * snt], axis=0).astype(BF16)
        s = a[nkt:nkt + nrow] + _dot(qab, feats)
        return (s.reshape(MLA_HEADS, lq, nk) * inv[:, None, :]).reshape(nrow, nk), ctb

    def update(s, ctb):
        m_prev = m_ref[...]
        m_new = jnp.maximum(m_prev, jnp.max(s, axis=-1, keepdims=True))
        alpha = jnp.exp2(m_prev - m_new)
        p = jnp.exp2(s - m_new)
        l_ref[...] = alpha * l_ref[...] + jnp.sum(p, axis=-1, keepdims=True)
        acc_ref[...] = alpha * acc_ref[...] + _dot_nt(p.astype(BF16), ctb)
        m_ref[...] = m_new

    for pi in range(pages):
        ctb_ref[:, pi * psz:(pi + 1) * psz] = page_refs[pi][0:kvw, :].astype(BF16)
    a_ref[...] = _dot(lhs_ref[...], ctb_ref[...])
    for g0 in range(0, pages, group):
        refs = page_refs[g0:g0 + group]
        lanes = slice(g0 * psz, (g0 + group) * psz)
        nk = group * psz
        kt = a_ref[0:nkt, lanes]
        krt = jnp.concatenate([r[kvw:kvw + ROPE_DIM, :] for r in refs], axis=1) if group > 1 else refs[0][kvw:kvw + ROPE_DIM, :]
        ss = jnp.sum((kt * kt).reshape(MLA_HEADS, nkt // MLA_HEADS, nk), axis=1)
        ss = ss + jnp.sum(krt * krt, axis=0, keepdims=True)
        inv = lax.rsqrt(ss * (1.0 / QK_DIM) + EPS)
        feats = jnp.concatenate([krt * tab_ref[0, :, lanes], krt * tab_ref[1, :, lanes]], axis=0).astype(BF16)
        s = a_ref[nkt:nkt + nrow, lanes] + _dot(qab, feats)
        s_ref[:, lanes] = (s.reshape(MLA_HEADS, lq, nk) * inv[:, None, :]).reshape(nrow, nk)
    update(s_ref[...], ctb_ref[...])

    @pl.when(j == pl.num_programs(1) - 1)
    def _():
        nk = new_ref.shape[2]
        tq = lax.broadcasted_iota(jnp.int32, (nrow, nk), 0) % lq
        tk = lax.broadcasted_iota(jnp.int32, (nrow, nk), 1)
        s, ctb = scores(new_ref[0, 0:kvw, :], new_ref[0, kvw:kvw + ROPE_DIM, :], tabn_ref[0], tabn_ref[1])
        update(jnp.where(tk <= tq, s, -jnp.inf), ctb)
        ctx = (acc_ref[...] / l_ref[...]).astype(BF16)
        vd = wuv_ref.shape[-1]
        for hh in range(MLA_HEADS):
            o_ref[0, :, hh * vd:(hh + 1) * vd] = _dot(ctx[hh * lq:(hh + 1) * lq, :], wuv_ref[hh])


def _decode(page_table, qt, qab, tab, tabn, rows_new_t, wukt, wuv, pool_t, layer):
    bs, nrow, kvw = qt.shape
    n_pages = page_table.shape[1]
    cw, psz = pool_t.shape[2], pool_t.shape[3]
    pages = math.gcd(n_pages, DECODE_PAGES)
    group = math.gcd(pages, DECODE_GROUP)
    lq = nrow // MLA_HEADS
    vd = wuv.shape[-1]
    per_s = lambda shp: pl.BlockSpec((1,) + shp, lambda s, j, pt: (s,) + (0,) * len(shp))
    grid_spec = pltpu.PrefetchScalarGridSpec(
        num_scalar_prefetch=1,
        grid=(bs, n_pages // pages),
        in_specs=[per_s((nrow, kvw)), per_s((nrow, 2 * ROPE_DIM)),
                  pl.BlockSpec((2, ROPE_DIM, pages * psz), lambda s, j, pt: (0, 0, j)),
                  _resident(tabn.shape), per_s(rows_new_t.shape[1:]),
                  _resident(wukt.shape), _resident(wuv.shape), pl.BlockSpec(memory_space=pl.ANY)],
        out_specs=per_s((lq, MLA_HEADS * vd)),
        scratch_shapes=[pltpu.VMEM((2, pages, cw, psz), F32), pltpu.SemaphoreType.DMA((2,)),
                        pltpu.VMEM((wukt.shape[0] + nrow, kvw), BF16),
                        pltpu.VMEM((kvw, pages * psz), BF16),
                        pltpu.VMEM((wukt.shape[0] + nrow, pages * psz), F32), pltpu.VMEM((nrow, pages * psz), F32),
                        pltpu.VMEM((nrow, 1), F32), pltpu.VMEM((nrow, 1), F32), pltpu.VMEM((nrow, kvw), F32)],
    )
    return pl.pallas_call(
        functools.partial(_decode_kernel, pages=pages, group=group, layer=layer),
        grid_spec=grid_spec,
        out_shape=jax.ShapeDtypeStruct((bs, lq, MLA_HEADS * vd), F32),
        compiler_params=_cparams(("arbitrary", "arbitrary")),
        name="decode",
    )(page_table, qt, qab, tab, tabn, rows_new_t, wukt, wuv, pool_t)


def _merge_kernel(h_ref, sh_ref, sc_ref, gt_ref, g_ref, ya_ref, yb_ref, yc_ref, yd_ref,
                  wg_ref, bg_ref, wb_ref, wo_ref, o_ref):
    bb, tl, d = h_ref.shape
    m = bb * tl
    h = h_ref[...]
    n = (_rms(h) * g_ref[...]) * (1.0 + sc_ref[...]) + sh_ref[...]
    n = n.reshape(m, d).astype(BF16)
    merged = None
    for r, y_ref in enumerate((ya_ref, yb_ref, yc_ref, yd_ref)):
        gate = jax.nn.sigmoid(_dot(n, wg_ref[r]) + bg_ref[r])
        term = gate * _dot(y_ref[...].reshape(m, y_ref.shape[-1]).astype(BF16), wb_ref[r])
        merged = term if merged is None else merged + term
    out = _dot(merged.astype(BF16), wo_ref[...])
    o_ref[...] = h + gt_ref[...] * out.reshape(bb, tl, d)


def _merge(h, sh, sc, gt, g, ys, wg, bg, wb, wo):
    b, l, d = h.shape
    bb, tl = _tok_blocks(b, l)
    tok = lambda w_: pl.BlockSpec((bb, tl, w_), lambda i, j: (i, j, 0))
    mod = pl.BlockSpec((bb, 1, d), lambda i, j: (i, 0, 0))
    return pl.pallas_call(
        _merge_kernel,
        grid=(b // bb, l // tl),
        in_specs=[tok(d), mod, mod, mod, _resident((1, d))] + [tok(y.shape[-1]) for y in ys]
                 + [_resident(wg.shape), _resident(bg.shape), _resident(wb.shape), _resident(wo.shape)],
        out_specs=tok(d),
        out_shape=jax.ShapeDtypeStruct(h.shape, F32),
        compiler_params=_cparams(("parallel", "parallel")),
        name="merge",
    )(h, sh, sc, gt, g, *ys, wg, bg, wb, wo)


def _rope_tables(pos):
    half = ROPE_DIM // 2
    inv = ROPE_THETA ** (-jnp.arange(half, dtype=F32) / half)
    ang = pos.astype(F32)[:, None] * inv
    return jnp.cos(ang), jnp.sin(ang)


def _lane_tables(pos):
    cos, sin = _rope_tables(pos)
    n = pos.shape[0]
    cos_t = jnp.concatenate([jnp.ones((n, NOPE_DIM), F32), cos, cos, jnp.zeros((n, LANES - QK_DIM), F32)], axis=1)
    sin_t = jnp.concatenate([jnp.zeros((n, NOPE_DIM), F32), -sin, sin, jnp.zeros((n, LANES - QK_DIM), F32)], axis=1)
    return cos_t, sin_t


def _pad_lanes(x, before, total):
    pad = [(0, 0)] * (x.ndim - 1) + [(before, total - before - x.shape[-1])]
    return jnp.pad(x, pad)


def _layer_weights(lw):
    d = lw["w_in"].shape[0]
    bd = lw["g_gm_v"].shape[0]
    q_lora, kv_lora = lw["g_q_lat"].shape[0], lw["g_kv_lat"].shape[0]
    layout, _ = _proj_layout(bd, q_lora, kv_lora)
    w = {}
    w["layout"] = layout
    w["w_ada"] = lw["w_ada"].astype(BF16)
    w["b_ada"] = lw["b_ada"][None, :]
    w["g_norm"] = [lw["g_norm"][i][None, :] for i in range(3)]
    w["w_ffn_in"] = [lw["w_ffn_in"][i].astype(BF16) for i in range(2)]
    w["w_ffn_out"] = [lw["w_ffn_out"][i].astype(BF16) for i in range(2)]
    w["w_proj"] = _proj_weight(lw["w_in"], bd, q_lora, kv_lora)
    w["g_gm_v"] = lw["g_gm_v"][None, :]
    w["g_q_lat"] = lw["g_q_lat"][None, :]
    w["g_kv_lat"] = lw["g_kv_lat"][None, :]
    w["w_spatial"] = lw["w_spatial"]
    w["b_spatial_t"] = lw["b_spatial"].T
    w["w_ssm_conv"] = lw["w_ssm_conv"]
    w["b_ssm_conv"] = lw["b_ssm_conv"][None, :]
    w["dt_bias"] = lw["dt_bias"][None, :]
    w["dt_bias_t"] = lw["dt_bias"][:, None]
    w["a_log"] = lw["a_log"][None, :]
    w["a_log_t"] = lw["a_log"][:, None]
    w["d_skip"] = jnp.repeat(lw["d_skip"], bd // SSM_HEADS)[None, :]
    w["g_ssm_norm"] = lw["g_ssm_norm"][None, :]
    w["w_sc_conv"] = lw["w_sc_conv"]
    w["wq"] = _pad_lanes(jnp.moveaxis(lw["w_uq"], 1, 0), 0, LANES).astype(BF16)
    wuk = jnp.moveaxis(lw["w_uk"], 1, 0)
    w["wk"] = _pad_lanes(wuk, 0, LANES).astype(BF16)
    wukt = jnp.swapaxes(wuk, 1, 2)
    w["wukt_pad"] = jnp.pad(wukt, ((0, 0), (0, LANES - NOPE_DIM), (0, 0))).astype(BF16)
    w["wukt_flat"] = wukt.reshape(MLA_HEADS * NOPE_DIM, kv_lora).astype(BF16)
    w["wuv"] = jnp.moveaxis(lw["w_uv"], 1, 0).astype(BF16)
    w["wuvt"] = jnp.swapaxes(w["wuv"], 1, 2)
    w["gq"] = _pad_lanes(lw["g_qk"][0][None, :], 0, LANES)
    w["gk"] = _pad_lanes(lw["g_qk"][1][None, :], 0, LANES)
    w["w_gate"] = lw["w_gate"].astype(BF16)
    w["b_gate"] = lw["b_gate"][:, None, :]
    w["w_branch_out"] = lw["w_branch_out"].astype(BF16)
    w["w_out"] = lw["w_out"].astype(BF16)
    return w


def _trunk_layer(x, mod, w, ssm_buf, ssm_s0, sc_buf, attn_fn):
    b, l, d = x.shape
    sh1, sc1, gt1, sh2, sc2, gt2, sh3, sc3, gt3 = [m[:, None, :] for m in jnp.split(mod, N_MOD, axis=-1)]
    h = _ffn(x, sh1, sc1, gt1, w["g_norm"][0], w["w_ffn_in"][0], w["w_ffn_out"][0])
    (y_a, a_v, b_z, b_xbc, y_c, c_ptail, ql, rows, cbf, krp, dtm, *ct) = _proj(
        h, sh2, sc2, w["g_norm"][1], w["w_proj"], w["g_gm_v"], w["g_q_lat"], w["g_kv_lat"],
        w["w_spatial"], w["b_spatial_t"], sc_buf, w["w_sc_conv"], w["layout"])
    sc_buf_new = jnp.concatenate([sc_buf, c_ptail[:, -min(l, 8):]], axis=1)[:, -(SC_WIDTH - 1):]
    dtt = jnp.swapaxes(dtm[:, :, 0:SSM_HEADS], 1, 2)
    npair = SSM_HEADS // 2
    s0 = ssm_s0.reshape(b, npair, LANES, SSM_STATE)
    y_b, s_new = _ssm(b_xbc, b_z, dtm, dtt, ssm_buf, s0, w["w_ssm_conv"], w["b_ssm_conv"],
                      w["dt_bias"], w["dt_bias_t"], w["a_log"], w["a_log_t"], w["d_skip"], w["g_ssm_norm"])
    s_new = s_new.reshape(ssm_s0.shape)
    ssm_buf_new = jnp.concatenate([ssm_buf, b_xbc], axis=1)[:, -(SSM_CONV - 1):]
    y_d = attn_fn(ql, rows, cbf, krp, *ct)
    h = _merge(h, sh2, sc2, gt2, w["g_norm"][1], (y_a, y_b, y_c, y_d), w["w_gate"], w["b_gate"],
               w["w_branch_out"], w["w_out"])
    h = _ffn(h, sh3, sc3, gt3, w["g_norm"][2], w["w_ffn_in"][1], w["w_ffn_out"][1])
    return h, rows, ssm_buf_new, s_new, sc_buf_new, a_v


def kernel(x_prompt, x_sample, c_prompt, c_sample, cache_mla, page_table, state_ssm, state_ssm_conv,
           state_short_conv, w_ada, b_ada, g_norm, w_ffn_in, w_ffn_out, w_in, g_gm_v, w_spatial, b_spatial,
           w_ssm_conv, b_ssm_conv, dt_bias, a_log, d_skip, g_ssm_norm, w_sc_conv, g_q_lat, w_uq, g_kv_lat,
           w_uk, w_uv, g_qk, w_branch_out, w_gate, b_gate, w_out):
    params = dict(w_ada=w_ada, b_ada=b_ada, g_norm=g_norm, w_ffn_in=w_ffn_in, w_ffn_out=w_ffn_out, w_in=w_in,
                  g_gm_v=g_gm_v, w_spatial=w_spatial, b_spatial=b_spatial, w_ssm_conv=w_ssm_conv,
                  b_ssm_conv=b_ssm_conv, dt_bias=dt_bias, a_log=a_log, d_skip=d_skip, g_ssm_norm=g_ssm_norm,
                  w_sc_conv=w_sc_conv, g_q_lat=g_q_lat, w_uq=w_uq, g_kv_lat=g_kv_lat, w_uk=w_uk, w_uv=w_uv,
                  g_qk=g_qk, w_branch_out=w_branch_out, w_gate=w_gate, b_gate=b_gate, w_out=w_out)
    depth = w_ada.shape[0]
    bp, lp, d = x_prompt.shape
    bs, ls, _ = x_sample.shape
    page = cache_mla.shape[2]
    past = page_table.shape[1] * page
    bd = g_gm_v.shape[1]
    conv_ch = w_ssm_conv.shape[-1]

    pos_p = jnp.arange(lp, dtype=jnp.int32)
    pos_s = past + jnp.arange(ls, dtype=jnp.int32)
    cos_p, sin_p = _lane_tables(pos_p)
    cos_s, sin_s = _lane_tables(pos_s)
    def key_tables(pos, width):
        cos, sin = _rope_tables(pos)
        t = jnp.stack([jnp.concatenate([cos, cos], axis=1).T, jnp.concatenate([sin, sin], axis=1).T])
        return jnp.pad(t, ((0, 0), (0, 0), (0, width - pos.shape[0])))

    tab_k = key_tables(jnp.arange(past, dtype=jnp.int32), past)
    new_w = -(-ls // LANES) * LANES
    tab_n = key_tables(pos_s, new_w)
    pool_t = jnp.swapaxes(cache_mla, 2, 3)
    c_all = jnp.concatenate([c_prompt, c_sample], axis=0)

    yp, ys = x_prompt, x_sample
    outs = [[] for _ in range(9)]
    for layer in range(depth):
        w = _layer_weights({k: v[layer] for k, v in params.items()})
        mod = _ada(c_all, w["w_ada"], w["b_ada"])

        def attn_prompt(ql, rows, cbf, krp, ct, w=w):
            q, k, vt = _qk_prompt(ql, cbf, krp, ct, cos_p, sin_p, w["wq"], w["wk"], w["wuvt"], w["gq"], w["gk"])
            return _flash(q, k, vt)

        def attn_sample(ql, rows, cbf, krp, *unused, w=w, layer=layer):
            qt, qa, qb = _q_decode(ql, cos_s, sin_s, w["wq"], w["gq"], w["gk"], w["wukt_pad"])
            flat = lambda t: t.reshape(bs, MLA_HEADS * ls, t.shape[-1])
            rows_t = jnp.pad(jnp.swapaxes(rows, 1, 2), ((0, 0), (0, 0), (0, new_w - ls)))
            return _decode(page_table, flat(qt), flat(jnp.concatenate([qa, qb], axis=-1)), tab_k, tab_n, rows_t,
                           w["wukt_flat"], w["wuv"], pool_t, layer)

        yp, r, cb, s, scb, _ = _trunk_layer(
            yp, mod[:bp], w,
            jnp.zeros((bp, SSM_CONV - 1, conv_ch), F32),
            jnp.zeros((bp, SSM_HEADS, bd // SSM_HEADS, SSM_STATE), F32),
            jnp.zeros((bp, SC_WIDTH - 1, bd), F32), attn_prompt)
        for lst, v in zip((outs[0], outs[2], outs[4], outs[6]), (r, s, cb, scb)):
            lst.append(v)
        ys, r, cb, s, scb, v = _trunk_layer(
            ys, mod[bp:], w, state_ssm_conv[layer], state_ssm[layer], state_short_conv[layer], attn_sample)
        for lst, val in zip((outs[1], outs[3], outs[5], outs[7], outs[8]), (r, s, cb, scb, v)):
            lst.append(val)
    st = jnp.stack
    return (yp, ys, st(outs[0]), st(outs[1]), st(outs[2]), st(outs[3]), st(outs[4]), st(outs[5]),
            st(outs[6]), st(outs[7]), st(outs[8]))
```

```python
import functools
import math

import jax
import jax.numpy as jnp
from jax import lax
from jax.experimental import pallas as pl
from jax.experimental.pallas import tpu as pltpu

F32 = jnp.float32
BF16 = jnp.bfloat16
HIGHEST = lax.Precision.HIGHEST

EPS = 1e-6
LANES = 128
N_MOD = 9
GM_CHUNK = 128
GM_HEADS = 4
SSM_HEADS = 8
SSM_GROUPS = 2
SSM_STATE = 128
SSM_CHUNK = 128
SSM_CONV = 4
SC_WIDTH = 3
MLA_HEADS = 8
NOPE_DIM = 64
ROPE_DIM = 32
QK_DIM = NOPE_DIM + ROPE_DIM
ROPE_THETA = 10000.0
QK_SCALE = 1.0 / math.sqrt(QK_DIM)
LOG2E = math.log2(math.e)
ROW_BLOCK = 512
FFN_ROW_BLOCK = 512
ATTN_Q_BLOCK = 512
ATTN_K_BLOCK = 512
DECODE_PAGES = 16
DECODE_GROUP = 4
VMEM_LIMIT = 56 * 1024 * 1024


def _cparams(sem):
    return pltpu.CompilerParams(dimension_semantics=sem, vmem_limit_bytes=VMEM_LIMIT)


def _resident(shape):
    nd = len(shape)
    return pl.BlockSpec(shape, lambda *_: (0,) * nd, pipeline_mode=pl.Buffered(1))


def _tok_blocks(b, l, rows=ROW_BLOCK):
    tl = min(l, rows)
    bb = min(b, max(1, rows // tl))
    assert l % tl == 0 and b % bb == 0 and tl % 8 == 0
    return bb, tl


def _silu(x):
    return x * jax.nn.sigmoid(x)


def _rms(x):
    return x * lax.rsqrt(jnp.mean(x * x, axis=-1, keepdims=True) + EPS)


def _dot(a, b):
    return jnp.dot(a, b, preferred_element_type=F32)


def _dot_nt(a, b, precision=None):
    return lax.dot_general(a, b, (((1,), (1,)), ((), ())), preferred_element_type=F32, precision=precision)


def _ada_kernel(c_ref, w_ref, b_ref, o_ref):
    o_ref[...] = _dot(_silu(c_ref[...]).astype(BF16), w_ref[...]) + b_ref[...]


def _ada(c, w, b):
    bt, d = c.shape
    n = w.shape[1]
    return pl.pallas_call(
        _ada_kernel,
        grid=(n // d,),
        in_specs=[pl.BlockSpec((bt, d), lambda j: (0, 0)),
                  pl.BlockSpec((d, d), lambda j: (0, j)),
                  pl.BlockSpec((1, d), lambda j: (0, j))],
        out_specs=pl.BlockSpec((bt, d), lambda j: (0, j)),
        out_shape=jax.ShapeDtypeStruct((bt, n), F32),
        compiler_params=_cparams(("arbitrary",)),
        name="ada",
    )(c, w, b)


def _ffn_kernel(x_ref, sh_ref, sc_ref, gt_ref, g_ref, win_ref, wout_ref, o_ref, acc_ref, *, tf):
    bb, tl, d = x_ref.shape
    f = wout_ref.shape[0]
    x = x_ref[...]
    xn = (_rms(x) * g_ref[...]) * (1.0 + sc_ref[...]) + sh_ref[...]
    xn = xn.reshape(bb * tl, d).astype(BF16)
    for i in range(f // tf):
        g = _dot(xn, win_ref[:, i * tf:(i + 1) * tf])
        u = _dot(xn, win_ref[:, f + i * tf:f + (i + 1) * tf])
        a = (_silu(g) * u).astype(BF16)
        part = _dot(a, wout_ref[i * tf:(i + 1) * tf, :])
        if i == 0:
            acc_ref[...] = part
        else:
            acc_ref[...] += part
    o_ref[...] = x + 0.5 * gt_ref[...] * acc_ref[...].reshape(bb, tl, d)


def _ffn(x, sh, sc, gt, g, w_in, w_out):
    b, l, d = x.shape
    bb, tl = _tok_blocks(b, l, FFN_ROW_BLOCK)
    f = w_out.shape[0]
    tf = 256 if f % 256 == 0 else LANES
    tok = pl.BlockSpec((bb, tl, d), lambda i, j: (i, j, 0))
    mod = pl.BlockSpec((bb, 1, d), lambda i, j: (i, 0, 0))
    return pl.pallas_call(
        functools.partial(_ffn_kernel, tf=tf),
        grid=(b // bb, l // tl),
        in_specs=[tok, mod, mod, mod, _resident((1, d)), _resident(w_in.shape), _resident(w_out.shape)],
        out_specs=tok,
        out_shape=jax.ShapeDtypeStruct(x.shape, F32),
        scratch_shapes=[pltpu.VMEM((bb * tl, d), F32)],
        compiler_params=_cparams(("parallel", "parallel")),
        name="ffn",
    )(x, sh, sc, gt, g, w_in, w_out)


_PROJ_GROUPS = ("a_u", "a_v", "b_z", "b_xbc", "c_h", "c_b", "c_c", "d_q", "d_kv", "kr0", "krp", "dt")


def _proj_layout(bd, q_lora, kv_lora):
    widths = dict(a_u=bd, a_v=bd, b_z=bd, b_xbc=2 * bd, c_h=bd, c_b=bd, c_c=bd, d_q=q_lora, d_kv=kv_lora,
                  kr0=LANES, krp=LANES, dt=LANES)
    off, layout = 0, {}
    for name in _PROJ_GROUPS:
        layout[name] = (off, widths[name])
        off += widths[name]
    return layout, off


def _proj_weight(w_in, bd, q_lora, kv_lora):
    d = w_in.shape[0]
    sizes = (bd, bd, bd, 2 * bd, SSM_HEADS, bd, bd, bd, q_lora, kv_lora, ROPE_DIM)
    a_u, a_v, b_z, b_xbc, b_dt, c_h, c_b, c_c, d_q, d_kv, d_kr = jnp.split(w_in, _cumsum(sizes)[:-1], axis=1)
    z = lambda n: jnp.zeros((d, n), w_in.dtype)
    kr0 = jnp.concatenate([d_kr, z(LANES - ROPE_DIM)], axis=1)
    krp = jnp.concatenate([z(NOPE_DIM), d_kr, z(LANES - QK_DIM)], axis=1)
    dt = jnp.concatenate([b_dt, z(LANES - SSM_HEADS)], axis=1)
    return jnp.concatenate([a_u, a_v, b_z, b_xbc, c_h, c_b, c_c, d_q, d_kv, kr0, krp, dt], axis=1).astype(BF16)


def _cumsum(sizes):
    out, s = [], 0
    for v in sizes:
        s += v
        out.append(s)
    return out


def _proj_kernel(h_ref, sh_ref, sc_ref, g_ref, w_ref, ggm_ref, gq_ref, gkv_ref, ws_ref, bst_ref, scbuf_ref, wsc_ref,
                 ya_ref, av_ref, z_ref, xbc_ref, yc_ref, ptail_ref, ql_ref, rows_ref, cbf_ref, krp_ref, dt_ref,
                 *rest, layout, chunk):
    *maybe_ct_ref, hist_ref = rest
    bb, tl, d = h_ref.shape
    n = (_rms(h_ref[...]) * g_ref[...]) * (1.0 + sc_ref[...]) + sh_ref[...]
    n = n.reshape(bb * tl, d).astype(BF16)

    def grp(name):
        off, w = layout[name]
        return _dot(n, w_ref[:, off:off + w])

    def put(ref, val):
        ref[...] = val.reshape(bb, tl, val.shape[-1]).astype(ref.dtype)

    a_u = jax.nn.gelu(grp("a_u"))
    a_v = _rms(jax.nn.gelu(grp("a_v"))) * ggm_ref[...]
    put(av_ref, a_v)
    bd = a_u.shape[-1]
    hd = bd // GM_HEADS
    row = lax.broadcasted_iota(jnp.int32, (chunk, chunk), 0)
    col = lax.broadcasted_iota(jnp.int32, (chunk, chunk), 1)
    for hh in range(GM_HEADS):
        wm = jnp.where(col <= row, ws_ref[hh, 0:chunk, 0:chunk], 0.0)
        bias = bst_ref[0:chunk, hh:hh + 1]
        cols = slice(hh * hd, (hh + 1) * hd)
        if chunk == GM_CHUNK:
            wmb = wm.astype(BF16)
            for ci in range(tl // chunk):
                rows = slice(ci * chunk, (ci + 1) * chunk)
                mixed = _dot(wmb, a_v[rows, cols].astype(BF16)) + bias
                ya_ref[0, rows, cols] = a_u[rows, cols] * mixed
        else:
            u3 = a_u[:, cols].reshape(bb, tl, hd)
            v3 = a_v[:, cols].reshape(bb, tl, hd)
            mixed = jnp.zeros((bb, chunk, hd), F32) + bias
            for j in range(chunk):
                mixed = mixed + wm[:, j:j + 1] * v3[:, j:j + 1, :]
            ya_ref[:, :, cols] = u3 * mixed

    kc = wsc_ref.shape[0]

    @pl.when(pl.program_id(1) == 0)
    def _():
        hist_ref[:, 0:8, :] = jnp.zeros((bb, 8, bd), F32)
        hist_ref[:, 8 - (kc - 1):8, :] = scbuf_ref[...]

    @pl.when(pl.program_id(1) > 0)
    def _():
        hist_ref[:, 0:8, :] = hist_ref[:, tl:tl + 8, :]

    hist_ref[:, 8:8 + tl, :] = (grp("c_c") * grp("c_h")).reshape(bb, tl, bd)
    acc = jnp.zeros((bb, tl, bd), F32)
    for i in range(kc):
        s = 8 - (kc - 1) + i
        acc = acc + wsc_ref[i:i + 1, :] * hist_ref[:, s:s + tl, :]
    yc_ref[...] = grp("c_b").reshape(bb, tl, bd) * acc
    ptail_ref[...] = hist_ref[:, tl:tl + 8, :]

    put(z_ref, grp("b_z"))
    put(xbc_ref, grp("b_xbc"))
    put(ql_ref, _rms(grp("d_q")) * gq_ref[...])
    kv = _rms(grp("d_kv")) * gkv_ref[...]
    put(cbf_ref, kv)
    if maybe_ct_ref:
        maybe_ct_ref[0][0] = kv.T.astype(BF16)
    kvw = kv.shape[-1]
    rows_ref[:, :, 0:kvw] = kv.reshape(bb, tl, kvw)
    rows_ref[:, :, kvw:kvw + ROPE_DIM] = grp("kr0")[:, 0:ROPE_DIM].reshape(bb, tl, ROPE_DIM)
    put(krp_ref, grp("krp"))
    put(dt_ref, grp("dt"))


def _proj(h, sh, sc, g, w, ggm, gq, gkv, w_s, b_st, sc_buf, w_sc, layout):
    b, l, d = h.shape
    bb, tl = _tok_blocks(b, l)
    bd, q_lora, kv_lora = ggm.shape[1], gq.shape[1], gkv.shape[1]
    chunk = min(l, GM_CHUNK)
    assert (chunk == GM_CHUNK and bb == 1 and tl % chunk == 0) or chunk == tl
    tok = lambda w_: pl.BlockSpec((bb, tl, w_), lambda i, j: (i, j, 0))
    mod = pl.BlockSpec((bb, 1, d), lambda i, j: (i, 0, 0))
    per_b = lambda r: pl.BlockSpec((bb, r, bd), lambda i, j: (i, 0, 0))
    outs = [(bd, F32), (bd, F32), (bd, F32), (2 * bd, F32), (bd, F32), None, (q_lora, BF16),
            (kv_lora + ROPE_DIM, F32), (kv_lora, BF16), (LANES, F32), (LANES, F32)]
    out_specs = [per_b(8) if o is None else tok(o[0]) for o in outs]
    out_shape = [jax.ShapeDtypeStruct((b, 8, bd), F32) if o is None else jax.ShapeDtypeStruct((b, l, o[0]), o[1])
                 for o in outs]
    if bb == 1 and tl % LANES == 0:
        out_specs.append(pl.BlockSpec((1, kv_lora, tl), lambda i, j: (i, 0, j)))
        out_shape.append(jax.ShapeDtypeStruct((b, kv_lora, l), BF16))
    return pl.pallas_call(
        functools.partial(_proj_kernel, layout=layout, chunk=chunk),
        grid=(b // bb, l // tl),
        in_specs=[tok(d), mod, mod, _resident((1, d)), _resident(w.shape),
                  _resident(ggm.shape), _resident(gq.shape), _resident(gkv.shape),
                  _resident(w_s.shape), _resident(b_st.shape), per_b(sc_buf.shape[1]), _resident(w_sc.shape)],
        out_specs=out_specs,
        out_shape=out_shape,
        scratch_shapes=[pltpu.VMEM((bb, 8 + tl, bd), F32)],
        compiler_params=_cparams(("parallel", "arbitrary")),
        name="proj",
    )(h, sh, sc, g, w, ggm, gq, gkv, w_s, b_st, sc_buf, w_sc)


def _softplus(x):
    return jnp.maximum(x, 0.0) + jnp.log1p(jnp.exp(-jnp.abs(x)))


def _ssm_kernel(xbc_ref, z_ref, dtm_ref, dtt_ref, buf_ref, s0_ref, wc_ref, bc_ref, dtb_ref, dtbt_ref,
                alog_ref, alogt_ref, dsk_ref, gn_ref, y_ref, sout_ref, hist_ref, s_ref):
    q = xbc_ref.shape[1]
    c = xbc_ref.shape[-1]
    bd = z_ref.shape[-1]
    k = wc_ref.shape[0]
    ci = pl.program_id(1)

    @pl.when(ci == 0)
    def _():
        hist_ref[0:8, :] = jnp.zeros((8, c), F32)
        hist_ref[8 - (k - 1):8, :] = buf_ref[0]
        s_ref[...] = s0_ref[0]

    @pl.when(ci > 0)
    def _():
        hist_ref[0:8, :] = hist_ref[q:q + 8, :]

    hist_ref[8:8 + q, :] = xbc_ref[0]
    acc = jnp.zeros((q, c), F32) + bc_ref[...]
    for i in range(k):
        s = 8 - (k - 1) + i
        acc = acc + wc_ref[i:i + 1, :] * hist_ref[s:s + q, :]
    xc = _silu(acc)
    xs = xc[:, 0:bd]
    gw = SSM_STATE
    bmf = [xc[:, bd + g * gw:bd + (g + 1) * gw] for g in range(SSM_GROUPS)]
    bm = [t.astype(BF16) for t in bmf]
    cm = [xc[:, bd + (SSM_GROUPS + g) * gw:bd + (SSM_GROUPS + g + 1) * gw].astype(BF16) for g in range(SSM_GROUPS)]

    row = lax.broadcasted_iota(jnp.int32, (q, q), 0)
    col = lax.broadcasted_iota(jnp.int32, (q, q), 1)
    causal = col <= row
    dt = _softplus(dtm_ref[0][:, 0:SSM_HEADS] + dtb_ref[...])
    dtt = _softplus(dtt_ref[0] + dtbt_ref[...])
    da = dt * (-jnp.exp(alog_ref[...]) * LOG2E)
    dat = dtt * (-jnp.exp(alogt_ref[...]) * LOG2E)
    cum = jnp.dot(causal.astype(F32), da, preferred_element_type=F32, precision=HIGHEST)
    cumt = jnp.dot(dat, (row <= col).astype(F32), preferred_element_type=F32, precision=HIGHEST)
    last = cum[q - 1:q, :]

    lane = lax.broadcasted_iota(jnp.int32, (q, LANES), 1)
    srow = lax.broadcasted_iota(jnp.int32, (LANES, gw), 0)
    p_dim = bd // SSM_HEADS
    rep = SSM_HEADS // SSM_GROUPS
    cbs = [_dot_nt(cm[g], bm[g]) for g in range(SSM_GROUPS)]
    for pair in range(SSM_HEADS // 2):
        h0, h1 = 2 * pair, 2 * pair + 1
        g = h0 // rep
        first = lane < p_dim
        cum_pair = jnp.where(first, cum[:, h0:h0 + 1], cum[:, h1:h1 + 1])
        last_pair = jnp.where(first, last[:, h0:h0 + 1], last[:, h1:h1 + 1])
        dt_pair = jnp.where(first, dt[:, h0:h0 + 1], dt[:, h1:h1 + 1])
        xdt = xs[:, pair * LANES:(pair + 1) * LANES] * dt_pair
        xb = xdt.astype(BF16)
        ys = []
        for hh in (h0, h1):
            dec = jnp.where(causal, jnp.exp2(cum[:, hh:hh + 1] - cumt[hh:hh + 1, :]), 0.0)
            ys.append(_dot((cbs[g] * dec).astype(BF16), xb))
        s_in = s_ref[pair]
        y_pair = jnp.where(first, ys[0], ys[1]) + jnp.exp2(cum_pair) * _dot_nt(cm[g], s_in.astype(BF16))
        y_ref[0, :, pair * LANES:(pair + 1) * LANES] = y_pair
        xw = xdt * jnp.exp2(last_pair - cum_pair)
        bmg = bm[g]
        if q < LANES:
            zpad = jnp.zeros((LANES - q, LANES), F32)
            xw = jnp.concatenate([xw, zpad], axis=0)
            bmg = jnp.concatenate([bmf[g], zpad], axis=0).astype(BF16)
        cs = _dot(xw.T.astype(BF16), bmg)
        cd = jnp.where(srow < p_dim, jnp.exp2(last[:, h0:h0 + 1]), jnp.exp2(last[:, h1:h1 + 1]))
        s_ref[pair] = s_in * cd + cs

    y = (y_ref[0] + dsk_ref[...] * xs) * _silu(z_ref[0])
    gs = bd // SSM_GROUPS
    for g in range(SSM_GROUPS):
        y_ref[0, :, g * gs:(g + 1) * gs] = _rms(y[:, g * gs:(g + 1) * gs]) * gn_ref[:, g * gs:(g + 1) * gs]
    sout_ref[0] = s_ref[...]


def _ssm(xbc, z, dtm, dtt, buf, s0, wc, bc, dtb, dtbt, alog, alogt, dsk, gn):
    b, l, c = xbc.shape
    bd = z.shape[-1]
    q = min(l, SSM_CHUNK)
    assert SSM_STATE == LANES and l % q == 0 and q % 8 == 0 and SSM_HEADS % 2 == 0 and bd // SSM_HEADS * 2 == LANES and (SSM_HEADS // SSM_GROUPS) % 2 == 0
    npair = SSM_HEADS // 2
    tok = lambda w_: pl.BlockSpec((1, q, w_), lambda i, j: (i, j, 0))
    per_b = lambda shp: pl.BlockSpec((1,) + shp, lambda i, j: (i,) + (0,) * len(shp))
    small = [wc, bc, dtb, dtbt, alog, alogt, dsk, gn]
    return pl.pallas_call(
        _ssm_kernel,
        grid=(b, l // q),
        in_specs=[tok(c), tok(bd), tok(LANES), pl.BlockSpec((1, SSM_HEADS, q), lambda i, j: (i, 0, j)),
                  per_b(buf.shape[1:]), per_b(s0.shape[1:])] + [_resident(a.shape) for a in small],
        out_specs=[tok(bd), per_b(s0.shape[1:])],
        out_shape=[jax.ShapeDtypeStruct((b, l, bd), F32), jax.ShapeDtypeStruct(s0.shape, F32)],
        scratch_shapes=[pltpu.VMEM((8 + q, c), F32), pltpu.VMEM((npair, LANES, SSM_STATE), F32)],
        compiler_params=_cparams(("parallel", "arbitrary")),
        name="ssm",
    )(xbc, z, dtm, dtt, buf, s0, *small)


def _rope_swap(x, lane, sign):
    half = ROPE_DIM // 2
    return jnp.where(lane < NOPE_DIM + half, pltpu.roll(x, LANES - half, 1), sign * pltpu.roll(x, half, 1))


def _qk_kernel(*refs, decode):
    if decode:
        (ql_ref, cos_ref, sin_ref, wq_ref, gq_ref, gk_ref, wukt_ref, qt_ref, qa_ref, qb_ref) = refs
    else:
        (ql_ref, cbf_ref, krp_ref, ct_ref, cos_ref, sin_ref, wq_ref, wk_ref, wuvt_ref, gq_ref, gk_ref,
         q_ref, k_ref, vt_ref) = refs
    bb, tl, _ = ql_ref.shape
    m = bb * tl
    ql = ql_ref[...].reshape(m, ql_ref.shape[-1])
    lane = lax.broadcasted_iota(jnp.int32, (m, LANES), 1)
    cos = jnp.broadcast_to(cos_ref[...][None], (bb, tl, LANES)).reshape(m, LANES)
    sin = jnp.broadcast_to(sin_ref[...][None], (bb, tl, LANES)).reshape(m, LANES)

    def norm_rope(x, g):
        xn = x * lax.rsqrt(jnp.sum(x * x, axis=-1, keepdims=True) * (1.0 / QK_DIM) + EPS) * g
        return xn * cos + _rope_swap(xn, lane, 1.0) * sin

    if not decode:
        cbf = cbf_ref[...].reshape(m, cbf_ref.shape[-1])
        krp = krp_ref[...].reshape(m, LANES)
    for hh in range(MLA_HEADS):
        qf = norm_rope(_dot(ql, wq_ref[hh]), gq_ref[...]) * (QK_SCALE * LOG2E)
        if decode:
            gk = gk_ref[...]
            nope = lane < NOPE_DIM
            qn = jnp.where(nope, qf * gk, 0.0).astype(BF16)
            qt_ref[:, hh] = _dot(qn, wukt_ref[hh]).reshape(bb, tl, wukt_ref.shape[-1])
            qa = qf * gk
            qb = _rope_swap(qf, lane, -1.0) * gk
            qa_ref[:, hh] = qa[:, NOPE_DIM:QK_DIM].reshape(bb, tl, ROPE_DIM)
            qb_ref[:, hh] = qb[:, NOPE_DIM:QK_DIM].reshape(bb, tl, ROPE_DIM)
        else:
            q_ref[:, hh] = qf.reshape(bb, tl, LANES).astype(BF16)
            kf = norm_rope(_dot(cbf, wk_ref[hh]) + krp, gk_ref[...])
            k_ref[:, hh] = kf.reshape(bb, tl, LANES).astype(BF16)
            vt_ref[0, hh] = _dot(wuvt_ref[hh], ct_ref[0]).astype(BF16)


def _qk_prompt(ql, cbf, krp, ct, cos, sin, wq, wk, wuvt, gq, gk):
    b, l, _ = ql.shape
    bb, tl = _tok_blocks(b, l)
    assert bb == 1
    vd = wuvt.shape[1]
    tok = lambda w_: pl.BlockSpec((bb, tl, w_), lambda i, j: (i, j, 0))
    tab = pl.BlockSpec((tl, LANES), lambda i, j: (j, 0))
    head = pl.BlockSpec((bb, MLA_HEADS, tl, LANES), lambda i, j: (i, 0, j, 0))
    shp = jax.ShapeDtypeStruct((b, MLA_HEADS, l, LANES), BF16)
    return pl.pallas_call(
        functools.partial(_qk_kernel, decode=False),
        grid=(b // bb, l // tl),
        in_specs=[tok(ql.shape[-1]), tok(cbf.shape[-1]), tok(LANES),
                  pl.BlockSpec((1, ct.shape[1], tl), lambda i, j: (i, 0, j)), tab, tab,
                  _resident(wq.shape), _resident(wk.shape), _resident(wuvt.shape),
                  _resident(gq.shape), _resident(gk.shape)],
        out_specs=[head, head, pl.BlockSpec((1, MLA_HEADS, vd, tl), lambda i, j: (i, 0, 0, j))],
        out_shape=[shp, shp, jax.ShapeDtypeStruct((b, MLA_HEADS, vd, l), BF16)],
        compiler_params=_cparams(("parallel", "parallel")),
        name="qk_prompt",
    )(ql, cbf, krp, ct, cos, sin, wq, wk, wuvt, gq, gk)


def _q_decode(ql, cos, sin, wq, gq, gk, wukt):
    b, l, _ = ql.shape
    bb, tl = _tok_blocks(b, l)
    assert tl == l
    kv = wukt.shape[-1]
    tok = lambda w_: pl.BlockSpec((bb, tl, w_), lambda i: (i, 0, 0))
    head = lambda w_: pl.BlockSpec((bb, MLA_HEADS, tl, w_), lambda i: (i, 0, 0, 0))
    shp = lambda w_: jax.ShapeDtypeStruct((b, MLA_HEADS, l, w_), F32)
    return pl.pallas_call(
        functools.partial(_qk_kernel, decode=True),
        grid=(b // bb,),
        in_specs=[tok(ql.shape[-1]), _resident(cos.shape), _resident(sin.shape),
                  _resident(wq.shape), _resident(gq.shape), _resident(gk.shape), _resident(wukt.shape)],
        out_specs=[head(kv), head(ROPE_DIM), head(ROPE_DIM)],
        out_shape=[shp(kv), shp(ROPE_DIM), shp(ROPE_DIM)],
        compiler_params=_cparams(("parallel",)),
        name="q_decode",
    )(ql, cos, sin, wq, gq, gk, wukt)


def _flash_kernel(qi_ref, ki_ref, q_ref, k_ref, vt_ref, o_ref, m_ref, l_ref, acc_ref):
    tq = q_ref.shape[2]
    tk = k_ref.shape[2]
    step_i = pl.program_id(1)
    qi, ki = qi_ref[step_i], ki_ref[step_i]

    @pl.when(ki == 0)
    def _():
        m_ref[...] = jnp.full(m_ref.shape, -jnp.inf, F32)
        l_ref[...] = jnp.zeros(l_ref.shape, F32)
        acc_ref[...] = jnp.zeros(acc_ref.shape, F32)

    vd = vt_ref.shape[2]

    def step(masked):
        if masked:
            keep = (ki * tk + lax.broadcasted_iota(jnp.int32, (tk, tq), 0)
                    <= qi * tq + lax.broadcasted_iota(jnp.int32, (tk, tq), 1))
        def values(hh, alpha, pb):
            rows = slice(hh * vd, (hh + 1) * vd)
            acc_ref[rows, :] = alpha * acc_ref[rows, :] + _dot(vt_ref[0, hh], pb)

        st_next = _dot_nt(k_ref[0, 0], q_ref[0, 0])
        pending = None
        for hh in range(MLA_HEADS):
            st = st_next
            if hh + 1 < MLA_HEADS:
                st_next = _dot_nt(k_ref[0, hh + 1], q_ref[0, hh + 1])
            if pending is not None:
                values(*pending)
            if masked:
                st = jnp.where(keep, st, -jnp.inf)
            m_prev = m_ref[hh:hh + 1, :]
            m_new = jnp.maximum(m_prev, jnp.max(st, axis=0, keepdims=True))
            alpha = jnp.exp2(m_prev - m_new)
            p = jnp.exp2(st - m_new)
            l_ref[hh:hh + 1, :] = alpha * l_ref[hh:hh + 1, :] + jnp.sum(p, axis=0, keepdims=True)
            m_ref[hh:hh + 1, :] = m_new
            pending = (hh, alpha, p.astype(BF16))
        values(*pending)

    straddles = (ki + 1) * tk - 1 > qi * tq

    @pl.when(jnp.logical_not(straddles))
    def _():
        step(False)

    @pl.when(straddles)
    def _():
        step(True)

    @pl.when((ki + 1) * tk >= (qi + 1) * tq)
    def _():
        for hh in range(MLA_HEADS):
            rows = slice(hh * vd, (hh + 1) * vd)
            acc_ref[rows, :] = acc_ref[rows, :] / l_ref[hh:hh + 1, :]
        o_ref[0] = acc_ref[...].T


def _flash(q, k, vt):
    b, h, l, _ = q.shape
    tq, tk = min(l, ATTN_Q_BLOCK), min(l, ATTN_K_BLOCK)
    assert l % tq == 0 and l % tk == 0
    vd = vt.shape[2]
    pairs = [(qi, ki) for qi in range(l // tq) for ki in range(l // tk) if ki * tk <= qi * tq + tq - 1]
    qi_of = jnp.asarray([p[0] for p in pairs], jnp.int32)
    ki_of = jnp.asarray([p[1] for p in pairs], jnp.int32)
    grid_spec = pltpu.PrefetchScalarGridSpec(
        num_scalar_prefetch=2,
        grid=(b, len(pairs)),
        in_specs=[pl.BlockSpec((1, h, tq, LANES), lambda i, s, qo, ko: (i, 0, qo[s], 0)),
                  pl.BlockSpec((1, h, tk, LANES), lambda i, s, qo, ko: (i, 0, ko[s], 0)),
                  pl.BlockSpec((1, h, vd, tk), lambda i, s, qo, ko: (i, 0, 0, ko[s]))],
        out_specs=pl.BlockSpec((1, tq, h * vd), lambda i, s, qo, ko: (i, qo[s], 0)),
        scratch_shapes=[pltpu.VMEM((h, tq), F32), pltpu.VMEM((h, tq), F32), pltpu.VMEM((h * vd, tq), F32)],
    )
    return pl.pallas_call(
        _flash_kernel,
        grid_spec=grid_spec,
        out_shape=jax.ShapeDtypeStruct((b, l, h * vd), F32),
        compiler_params=_cparams(("parallel", "arbitrary")),
        name="flash",
    )(qi_of, ki_of, q, k, vt)


def _decode_kernel(pt_ref, qt_ref, qab_ref, tab_ref, tabn_ref, new_ref, wukt_ref, wuv_ref, pool_ref,
                   o_ref, buf_ref, sem, lhs_ref, ctb_ref, a_ref, s_ref, pb_ref, alpha_ref, m_ref, l_ref, acc_ref,
                   *, pages, group, layer):
    seq, j = pl.program_id(0), pl.program_id(1)
    nseq, nj = pl.num_programs(0), pl.num_programs(1)
    nrow = qt_ref.shape[1]
    lq = nrow // MLA_HEADS
    kvw = qt_ref.shape[2]
    nkt = wukt_ref.shape[0]
    psz = buf_ref.shape[3]
    step = seq * nj + j
    slot = step % 2

    def page_copy(sq, chunk, pi, sl):
        return pltpu.make_async_copy(pool_ref.at[layer, pt_ref[sq, chunk * pages + pi]], buf_ref.at[sl, pi], sem.at[sl])

    @pl.when(step == 0)
    def _():
        for pi in range(pages):
            page_copy(seq, j, pi, slot).start()

    @pl.when(step + 1 < nseq * nj)
    def _():
        wrap = j + 1 == nj
        nxt_seq = jnp.where(wrap, seq + 1, seq)
        nxt_j = jnp.where(wrap, 0, j + 1)
        for pi in range(pages):
            page_copy(nxt_seq, nxt_j, pi, 1 - slot).start()

    for pi in range(pages):
        page_copy(seq, j, pi, slot).wait()
    page_refs = [buf_ref.at[slot, pi] for pi in range(pages)]

    @pl.when(j == 0)
    def _():
        m_ref[...] = jnp.full(m_ref.shape, -jnp.inf, F32)
        l_ref[...] = jnp.zeros(l_ref.shape, F32)
        acc_ref[...] = jnp.zeros(acc_ref.shape, F32)
        lhs_ref[0:nkt, :] = wukt_ref[...]
        lhs_ref[nkt:nkt + nrow, :] = qt_ref[0].astype(BF16)
        alpha_ref[...] = jnp.ones(alpha_ref.shape, F32)
        pb_ref[...] = jnp.zeros(pb_ref.shape, BF16)
        ctb_ref[1] = jnp.zeros(ctb_ref.shape[1:], BF16)

    qab = qab_ref[0].astype(BF16)
    cur = j % 2

    def scores(ct, krt, cct, snt):
        nk = ct.shape[1]
        ctb = ct.astype(BF16)
        a = _dot(lhs_ref[...], ctb)
        kt = a[0:nkt]
        ss = jnp.sum((kt * kt).reshape(MLA_HEADS, nkt // MLA_HEADS, nk), axis=1)
        ss = ss + jnp.sum(krt * krt, axis=0, keepdims=True)
        inv = lax.rsqrt(ss * (1.0 / QK_DIM) + EPS)
        feats = jnp.concatenate([krt * cct, krt * snt], axis=0).astype(BF16)
        s = a[nkt:nkt + nrow] + _dot(qab, feats)
        return (s.reshape(MLA_HEADS, lq, nk) * inv[:, None, :]).reshape(nrow, nk), ctb

    def softmax_stats(s):
        m_prev = m_ref[...]
        m_new = jnp.maximum(m_prev, jnp.max(s, axis=-1, keepdims=True))
        alpha = jnp.exp2(m_prev - m_new)
        p = jnp.exp2(s - m_new)
        l_ref[...] = alpha * l_ref[...] + jnp.sum(p, axis=-1, keepdims=True)
        m_ref[...] = m_new
        return alpha, p.astype(BF16)

    for pi in range(pages):
        ctb_ref[cur, :, pi * psz:(pi + 1) * psz] = page_refs[pi][0:kvw, :].astype(BF16)
    a_ref[...] = _dot(lhs_ref[...], ctb_ref[cur])
    pv_prev = _dot_nt(pb_ref[...], ctb_ref[1 - cur])
    for g0 in range(0, pages, group):
        refs = page_refs[g0:g0 + group]
        lanes = slice(g0 * psz, (g0 + group) * psz)
        nk = group * psz
        kt = a_ref[0:nkt, lanes]
        krt = jnp.concatenate([r[kvw:kvw + ROPE_DIM, :] for r in refs], axis=1) if group > 1 else refs[0][kvw:kvw + ROPE_DIM, :]
        ss = jnp.sum((kt * kt).reshape(MLA_HEADS, nkt // MLA_HEADS, nk), axis=1)
        ss = ss + jnp.sum(krt * krt, axis=0, keepdims=True)
        inv = lax.rsqrt(ss * (1.0 / QK_DIM) + EPS)
        feats = jnp.concatenate([krt * tab_ref[0, :, lanes], krt * tab_ref[1, :, lanes]], axis=0).astype(BF16)
        s = a_ref[nkt:nkt + nrow, lanes] + _dot(qab, feats)
        s_ref[:, lanes] = (s.reshape(MLA_HEADS, lq, nk) * inv[:, None, :]).reshape(nrow, nk)
    acc_ref[...] = alpha_ref[...] * acc_ref[...] + pv_prev
    alpha, pb = softmax_stats(s_ref[...])
    alpha_ref[...] = alpha
    pb_ref[...] = pb

    @pl.when(j == pl.num_programs(1) - 1)
    def _():
        acc_ref[...] = alpha_ref[...] * acc_ref[...] + _dot_nt(pb_ref[...], ctb_ref[cur])
        nk = new_ref.shape[2]
        tq = lax.broadcasted_iota(jnp.int32, (nrow, nk), 0) % lq
        tk = lax.broadcasted_iota(jnp.int32, (nrow, nk), 1)
        s, ctb = scores(new_ref[0, 0:kvw, :], new_ref[0, kvw:kvw + ROPE_DIM, :], tabn_ref[0], tabn_ref[1])
        alpha_n, pb_n = softmax_stats(jnp.where(tk <= tq, s, -jnp.inf))
        acc_ref[...] = alpha_n * acc_ref[...] + _dot_nt(pb_n, ctb)
        ctx = (acc_ref[...] / l_ref[...]).astype(BF16)
        vd = wuv_ref.shape[-1]
        for hh in range(MLA_HEADS):
            o_ref[0, :, hh * vd:(hh + 1) * vd] = _dot(ctx[hh * lq:(hh + 1) * lq, :], wuv_ref[hh])


def _decode(page_table, qt, qab, tab, tabn, rows_new_t, wukt, wuv, pool_t, layer):
    bs, nrow, kvw = qt.shape
    n_pages = page_table.shape[1]
    cw, psz = pool_t.shape[2], pool_t.shape[3]
    pages = math.gcd(n_pages, DECODE_PAGES)
    group = math.gcd(pages, DECODE_GROUP)
    lq = nrow // MLA_HEADS
    vd = wuv.shape[-1]
    per_s = lambda shp: pl.BlockSpec((1,) + shp, lambda s, j, pt: (s,) + (0,) * len(shp))
    grid_spec = pltpu.PrefetchScalarGridSpec(
        num_scalar_prefetch=1,
        grid=(bs, n_pages // pages),
        in_specs=[per_s((nrow, kvw)), per_s((nrow, 2 * ROPE_DIM)),
                  pl.BlockSpec((2, ROPE_DIM, pages * psz), lambda s, j, pt: (0, 0, j)),
                  _resident(tabn.shape), per_s(rows_new_t.shape[1:]),
                  _resident(wukt.shape), _resident(wuv.shape), pl.BlockSpec(memory_space=pl.ANY)],
        out_specs=per_s((lq, MLA_HEADS * vd)),
        scratch_shapes=[pltpu.VMEM((2, pages, cw, psz), F32), pltpu.SemaphoreType.DMA((2,)),
                        pltpu.VMEM((wukt.shape[0] + nrow, kvw), BF16),
                        pltpu.VMEM((2, kvw, pages * psz), BF16),
                        pltpu.VMEM((wukt.shape[0] + nrow, pages * psz), F32), pltpu.VMEM((nrow, pages * psz), F32),
                        pltpu.VMEM((nrow, pages * psz), BF16), pltpu.VMEM((nrow, 1), F32),
                        pltpu.VMEM((nrow, 1), F32), pltpu.VMEM((nrow, 1), F32), pltpu.VMEM((nrow, kvw), F32)],
    )
    return pl.pallas_call(
        functools.partial(_decode_kernel, pages=pages, group=group, layer=layer),
        grid_spec=grid_spec,
        out_shape=jax.ShapeDtypeStruct((bs, lq, MLA_HEADS * vd), F32),
        compiler_params=_cparams(("arbitrary", "arbitrary")),
        name="decode",
    )(page_table, qt, qab, tab, tabn, rows_new_t, wukt, wuv, pool_t)


def _merge_kernel(h_ref, sh_ref, sc_ref, gt_ref, g_ref, ya_ref, yb_ref, yc_ref, yd_ref,
                  wg_ref, bg_ref, wb_ref, wo_ref, o_ref):
    bb, tl, d = h_ref.shape
    m = bb * tl
    h = h_ref[...]
    n = (_rms(h) * g_ref[...]) * (1.0 + sc_ref[...]) + sh_ref[...]
    n = n.reshape(m, d).astype(BF16)
    merged = None
    for r, y_ref in enumerate((ya_ref, yb_ref, yc_ref, yd_ref)):
        gate = jax.nn.sigmoid(_dot(n, wg_ref[r]) + bg_ref[r])
        term = gate * _dot(y_ref[...].reshape(m, y_ref.shape[-1]).astype(BF16), wb_ref[r])
        merged = term if merged is None else merged + term
    out = _dot(merged.astype(BF16), wo_ref[...])
    o_ref[...] = h + gt_ref[...] * out.reshape(bb, tl, d)


def _merge(h, sh, sc, gt, g, ys, wg, bg, wb, wo):
    b, l, d = h.shape
    bb, tl = _tok_blocks(b, l)
    tok = lambda w_: pl.BlockSpec((bb, tl, w_), lambda i, j: (i, j, 0))
    mod = pl.BlockSpec((bb, 1, d), lambda i, j: (i, 0, 0))
    return pl.pallas_call(
        _merge_kernel,
        grid=(b // bb, l // tl),
        in_specs=[tok(d), mod, mod, mod, _resident((1, d))] + [tok(y.shape[-1]) for y in ys]
                 + [_resident(wg.shape), _resident(bg.shape), _resident(wb.shape), _resident(wo.shape)],
        out_specs=tok(d),
        out_shape=jax.ShapeDtypeStruct(h.shape, F32),
        compiler_params=_cparams(("parallel", "parallel")),
        name="merge",
    )(h, sh, sc, gt, g, *ys, wg, bg, wb, wo)


def _rope_tables(pos):
    half = ROPE_DIM // 2
    inv = ROPE_THETA ** (-jnp.arange(half, dtype=F32) / half)
    ang = pos.astype(F32)[:, None] * inv
    return jnp.cos(ang), jnp.sin(ang)


def _lane_tables(pos):
    cos, sin = _rope_tables(pos)
    n = pos.shape[0]
    cos_t = jnp.concatenate([jnp.ones((n, NOPE_DIM), F32), cos, cos, jnp.zeros((n, LANES - QK_DIM), F32)], axis=1)
    sin_t = jnp.concatenate([jnp.zeros((n, NOPE_DIM), F32), -sin, sin, jnp.zeros((n, LANES - QK_DIM), F32)], axis=1)
    return cos_t, sin_t


def _pad_lanes(x, before, total):
    pad = [(0, 0)] * (x.ndim - 1) + [(before, total - before - x.shape[-1])]
    return jnp.pad(x, pad)


def _layer_weights(lw):
    d = lw["w_in"].shape[0]
    bd = lw["g_gm_v"].shape[0]
    q_lora, kv_lora = lw["g_q_lat"].shape[0], lw["g_kv_lat"].shape[0]
    layout, _ = _proj_layout(bd, q_lora, kv_lora)
    w = {}
    w["layout"] = layout
    w["w_ada"] = lw["w_ada"].astype(BF16)
    w["b_ada"] = lw["b_ada"][None, :]
    w["g_norm"] = [lw["g_norm"][i][None, :] for i in range(3)]
    w["w_ffn_in"] = [lw["w_ffn_in"][i].astype(BF16) for i in range(2)]
    w["w_ffn_out"] = [lw["w_ffn_out"][i].astype(BF16) for i in range(2)]
    w["w_proj"] = _proj_weight(lw["w_in"], bd, q_lora, kv_lora)
    w["g_gm_v"] = lw["g_gm_v"][None, :]
    w["g_q_lat"] = lw["g_q_lat"][None, :]
    w["g_kv_lat"] = lw["g_kv_lat"][None, :]
    w["w_spatial"] = lw["w_spatial"]
    w["b_spatial_t"] = lw["b_spatial"].T
    w["w_ssm_conv"] = lw["w_ssm_conv"]
    w["b_ssm_conv"] = lw["b_ssm_conv"][None, :]
    w["dt_bias"] = lw["dt_bias"][None, :]
    w["dt_bias_t"] = lw["dt_bias"][:, None]
    w["a_log"] = lw["a_log"][None, :]
    w["a_log_t"] = lw["a_log"][:, None]
    w["d_skip"] = jnp.repeat(lw["d_skip"], bd // SSM_HEADS)[None, :]
    w["g_ssm_norm"] = lw["g_ssm_norm"][None, :]
    w["w_sc_conv"] = lw["w_sc_conv"]
    w["wq"] = _pad_lanes(jnp.moveaxis(lw["w_uq"], 1, 0), 0, LANES).astype(BF16)
    wuk = jnp.moveaxis(lw["w_uk"], 1, 0)
    w["wk"] = _pad_lanes(wuk, 0, LANES).astype(BF16)
    wukt = jnp.swapaxes(wuk, 1, 2)
    w["wukt_pad"] = jnp.pad(wukt, ((0, 0), (0, LANES - NOPE_DIM), (0, 0))).astype(BF16)
    w["wukt_flat"] = wukt.reshape(MLA_HEADS * NOPE_DIM, kv_lora).astype(BF16)
    w["wuv"] = jnp.moveaxis(lw["w_uv"], 1, 0).astype(BF16)
    w["wuvt"] = jnp.swapaxes(w["wuv"], 1, 2)
    w["gq"] = _pad_lanes(lw["g_qk"][0][None, :], 0, LANES)
    w["gk"] = _pad_lanes(lw["g_qk"][1][None, :], 0, LANES)
    w["w_gate"] = lw["w_gate"].astype(BF16)
    w["b_gate"] = lw["b_gate"][:, None, :]
    w["w_branch_out"] = lw["w_branch_out"].astype(BF16)
    w["w_out"] = lw["w_out"].astype(BF16)
    return w


def _trunk_layer(x, mod, w, ssm_buf, ssm_s0, sc_buf, attn_fn):
    b, l, d = x.shape
    sh1, sc1, gt1, sh2, sc2, gt2, sh3, sc3, gt3 = [m[:, None, :] for m in jnp.split(mod, N_MOD, axis=-1)]
    h = _ffn(x, sh1, sc1, gt1, w["g_norm"][0], w["w_ffn_in"][0], w["w_ffn_out"][0])
    (y_a, a_v, b_z, b_xbc, y_c, c_ptail, ql, rows, cbf, krp, dtm, *ct) = _proj(
        h, sh2, sc2, w["g_norm"][1], w["w_proj"], w["g_gm_v"], w["g_q_lat"], w["g_kv_lat"],
        w["w_spatial"], w["b_spatial_t"], sc_buf, w["w_sc_conv"], w["layout"])
    sc_buf_new = jnp.concatenate([sc_buf, c_ptail[:, -min(l, 8):]], axis=1)[:, -(SC_WIDTH - 1):]
    dtt = jnp.swapaxes(dtm[:, :, 0:SSM_HEADS], 1, 2)
    npair = SSM_HEADS // 2
    s0 = ssm_s0.reshape(b, npair, LANES, SSM_STATE)
    y_b, s_new = _ssm(b_xbc, b_z, dtm, dtt, ssm_buf, s0, w["w_ssm_conv"], w["b_ssm_conv"],
                      w["dt_bias"], w["dt_bias_t"], w["a_log"], w["a_log_t"], w["d_skip"], w["g_ssm_norm"])
    s_new = s_new.reshape(ssm_s0.shape)
    ssm_buf_new = jnp.concatenate([ssm_buf, b_xbc], axis=1)[:, -(SSM_CONV - 1):]
    y_d = attn_fn(ql, rows, cbf, krp, *ct)
    h = _merge(h, sh2, sc2, gt2, w["g_norm"][1], (y_a, y_b, y_c, y_d), w["w_gate"], w["b_gate"],
               w["w_branch_out"], w["w_out"])
    h = _ffn(h, sh3, sc3, gt3, w["g_norm"][2], w["w_ffn_in"][1], w["w_ffn_out"][1])
    return h, rows, ssm_buf_new, s_new, sc_buf_new, a_v


def kernel(x_prompt, x_sample, c_prompt, c_sample, cache_mla, page_table, state_ssm, state_ssm_conv,
           state_short_conv, w_ada, b_ada, g_norm, w_ffn_in, w_ffn_out, w_in, g_gm_v, w_spatial, b_spatial,
           w_ssm_conv, b_ssm_conv, dt_bias, a_log, d_skip, g_ssm_norm, w_sc_conv, g_q_lat, w_uq, g_kv_lat,
           w_uk, w_uv, g_qk, w_branch_out, w_gate, b_gate, w_out):
    params = dict(w_ada=w_ada, b_ada=b_ada, g_norm=g_norm, w_ffn_in=w_ffn_in, w_ffn_out=w_ffn_out, w_in=w_in,
                  g_gm_v=g_gm_v, w_spatial=w_spatial, b_spatial=b_spatial, w_ssm_conv=w_ssm_conv,
                  b_ssm_conv=b_ssm_conv, dt_bias=dt_bias, a_log=a_log, d_skip=d_skip, g_ssm_norm=g_ssm_norm,
                  w_sc_conv=w_sc_conv, g_q_lat=g_q_lat, w_uq=w_uq, g_kv_lat=g_kv_lat, w_uk=w_uk, w_uv=w_uv,
                  g_qk=g_qk, w_branch_out=w_branch_out, w_gate=w_gate, b_gate=b_gate, w_out=w_out)
    depth = w_ada.shape[0]
    bp, lp, d = x_prompt.shape
    bs, ls, _ = x_sample.shape
    page = cache_mla.shape[2]
    past = page_table.shape[1] * page
    bd = g_gm_v.shape[1]
    conv_ch = w_ssm_conv.shape[-1]

    pos_p = jnp.arange(lp, dtype=jnp.int32)
    pos_s = past + jnp.arange(ls, dtype=jnp.int32)
    cos_p, sin_p = _lane_tables(pos_p)
    cos_s, sin_s = _lane_tables(pos_s)
    def key_tables(pos, width):
        cos, sin = _rope_tables(pos)
        t = jnp.stack([jnp.concatenate([cos, cos], axis=1).T, jnp.concatenate([sin, sin], axis=1).T])
        return jnp.pad(t, ((0, 0), (0, 0), (0, width - pos.shape[0])))

    tab_k = key_tables(jnp.arange(past, dtype=jnp.int32), past)
    new_w = -(-ls // LANES) * LANES
    tab_n = key_tables(pos_s, new_w)
    pool_t = jnp.swapaxes(cache_mla, 2, 3)
    c_all = jnp.concatenate([c_prompt, c_sample], axis=0)

    yp, ys = x_prompt, x_sample
    outs = [[] for _ in range(9)]
    for layer in range(depth):
        w = _layer_weights({k: v[layer] for k, v in params.items()})
        mod = _ada(c_all, w["w_ada"], w["b_ada"])

        def attn_prompt(ql, rows, cbf, krp, ct, w=w):
            q, k, vt = _qk_prompt(ql, cbf, krp, ct, cos_p, sin_p, w["wq"], w["wk"], w["wuvt"], w["gq"], w["gk"])
            return _flash(q, k, vt)

        def attn_sample(ql, rows, cbf, krp, *unused, w=w, layer=layer):
            qt, qa, qb = _q_decode(ql, cos_s, sin_s, w["wq"], w["gq"], w["gk"], w["wukt_pad"])
            flat = lambda t: t.reshape(bs, MLA_HEADS * ls, t.shape[-1])
            rows_t = jnp.pad(jnp.swapaxes(rows, 1, 2), ((0, 0), (0, 0), (0, new_w - ls)))
            return _decode(page_table, flat(qt), flat(jnp.concatenate([qa, qb], axis=-1)), tab_k, tab_n, rows_t,
                           w["wukt_flat"], w["wuv"], pool_t, layer)

        yp, r, cb, s, scb, _ = _trunk_layer(
            yp, mod[:bp], w,
            jnp.zeros((bp, SSM_CONV - 1, conv_ch), F32),
            jnp.zeros((bp, SSM_HEADS, bd // SSM_HEADS, SSM_STATE), F32),
            jnp.zeros((bp, SC_WIDTH - 1, bd), F32), attn_prompt)
        for lst, v in zip((outs[0], outs[2], outs[4], outs[6]), (r, s, cb, scb)):
            lst.append(v)
        ys, r, cb, s, scb, v = _trunk_layer(
            ys, mod[bp:], w, state_ssm_conv[layer], state_ssm[layer], state_short_conv[layer], attn_sample)
        for lst, val in zip((outs[1], outs[3], outs[5], outs[7], outs[8]), (r, s, cb, scb, v)):
            lst.append(val)
    st = jnp.stack
    return (yp, ys, st(outs[0]), st(outs[1]), st(outs[2]), st(outs[3]), st(outs[4]), st(outs[5]),
            st(outs[6]), st(outs[7]), st(outs[8]))
```

```python
import functools
import math

import jax
import jax.numpy as jnp
from jax import lax
from jax.experimental import pallas as pl
from jax.experimental.pallas import tpu as pltpu

F32 = jnp.float32
BF16 = jnp.bfloat16
HIGHEST = lax.Precision.HIGHEST

EPS = 1e-6
LANES = 128
N_MOD = 9
GM_CHUNK = 128
GM_HEADS = 4
SSM_HEADS = 8
SSM_GROUPS = 2
SSM_STATE = 128
SSM_CHUNK = 128
SSM_CONV = 4
SC_WIDTH = 3
MLA_HEADS = 8
NOPE_DIM = 64
ROPE_DIM = 32
QK_DIM = NOPE_DIM + ROPE_DIM
ROPE_THETA = 10000.0
QK_SCALE = 1.0 / math.sqrt(QK_DIM)
LOG2E = math.log2(math.e)
SCORE_FLOOR = float(jnp.finfo(jnp.float32).min)
ROW_BLOCK = 512
FFN_ROW_BLOCK = 512
ATTN_Q_BLOCK = 512
ATTN_K_BLOCK = 512
QK_AHEAD = 1
DECODE_PAGES = 16
DECODE_GROUP = 4
VMEM_LIMIT = 56 * 1024 * 1024


def _cparams(sem):
    return pltpu.CompilerParams(dimension_semantics=sem, vmem_limit_bytes=VMEM_LIMIT)


def _resident(shape):
    nd = len(shape)
    return pl.BlockSpec(shape, lambda *_: (0,) * nd, pipeline_mode=pl.Buffered(1))


def _tok_blocks(b, l, rows=ROW_BLOCK):
    tl = min(l, rows)
    bb = min(b, max(1, rows // tl))
    assert l % tl == 0 and b % bb == 0 and tl % 8 == 0
    return bb, tl


def _silu(x):
    return x * jax.nn.sigmoid(x)


def _rms(x):
    return x * lax.rsqrt(jnp.mean(x * x, axis=-1, keepdims=True) + EPS)


def _dot(a, b):
    return jnp.dot(a, b, preferred_element_type=F32)


def _dot_nt(a, b, precision=None):
    return lax.dot_general(a, b, (((1,), (1,)), ((), ())), preferred_element_type=F32, precision=precision)


def _ada_kernel(c_ref, w_ref, b_ref, o_ref):
    o_ref[...] = _dot(_silu(c_ref[...]).astype(BF16), w_ref[...]) + b_ref[...]


def _ada(c, w, b):
    bt, d = c.shape
    n = w.shape[1]
    return pl.pallas_call(
        _ada_kernel,
        grid=(n // d,),
        in_specs=[pl.BlockSpec((bt, d), lambda j: (0, 0)),
                  pl.BlockSpec((d, d), lambda j: (0, j)),
                  pl.BlockSpec((1, d), lambda j: (0, j))],
        out_specs=pl.BlockSpec((bt, d), lambda j: (0, j)),
        out_shape=jax.ShapeDtypeStruct((bt, n), F32),
        compiler_params=_cparams(("arbitrary",)),
        name="ada",
    )(c, w, b)


def _ffn_kernel(x_ref, sh_ref, sc_ref, gt_ref, g_ref, win_ref, wout_ref, o_ref, acc_ref, *, tf):
    bb, tl, d = x_ref.shape
    f = wout_ref.shape[0]
    x = x_ref[...]
    xn = (_rms(x) * g_ref[...]) * (1.0 + sc_ref[...]) + sh_ref[...]
    xn = xn.reshape(bb * tl, d).astype(BF16)
    for i in range(f // tf):
        g = _dot(xn, win_ref[:, i * tf:(i + 1) * tf])
        u = _dot(xn, win_ref[:, f + i * tf:f + (i + 1) * tf])
        a = (_silu(g) * u).astype(BF16)
        part = _dot(a, wout_ref[i * tf:(i + 1) * tf, :])
        if i == 0:
            acc_ref[...] = part
        else:
            acc_ref[...] += part
    o_ref[...] = x + 0.5 * gt_ref[...] * acc_ref[...].reshape(bb, tl, d)


def _ffn(x, sh, sc, gt, g, w_in, w_out):
    b, l, d = x.shape
    bb, tl = _tok_blocks(b, l, FFN_ROW_BLOCK)
    f = w_out.shape[0]
    tf = 256 if f % 256 == 0 else LANES
    tok = pl.BlockSpec((bb, tl, d), lambda i, j: (i, j, 0))
    mod = pl.BlockSpec((bb, 1, d), lambda i, j: (i, 0, 0))
    return pl.pallas_call(
        functools.partial(_ffn_kernel, tf=tf),
        grid=(b // bb, l // tl),
        in_specs=[tok, mod, mod, mod, _resident((1, d)), _resident(w_in.shape), _resident(w_out.shape)],
        out_specs=tok,
        out_shape=jax.ShapeDtypeStruct(x.shape, F32),
        scratch_shapes=[pltpu.VMEM((bb * tl, d), F32)],
        compiler_params=_cparams(("parallel", "parallel")),
        name="ffn",
    )(x, sh, sc, gt, g, w_in, w_out)


_PROJ_GROUPS = ("a_u", "a_v", "b_z", "b_xbc", "c_h", "c_b", "c_c", "d_q", "d_kv", "kr0", "krp", "dt")


def _proj_layout(bd, q_lora, kv_lora):
    widths = dict(a_u=bd, a_v=bd, b_z=bd, b_xbc=2 * bd, c_h=bd, c_b=bd, c_c=bd, d_q=q_lora, d_kv=kv_lora,
                  kr0=LANES, krp=LANES, dt=LANES)
    off, layout = 0, {}
    for name in _PROJ_GROUPS:
        layout[name] = (off, widths[name])
        off += widths[name]
    return layout, off


def _proj_weight(w_in, bd, q_lora, kv_lora):
    d = w_in.shape[0]
    sizes = (bd, bd, bd, 2 * bd, SSM_HEADS, bd, bd, bd, q_lora, kv_lora, ROPE_DIM)
    a_u, a_v, b_z, b_xbc, b_dt, c_h, c_b, c_c, d_q, d_kv, d_kr = jnp.split(w_in, _cumsum(sizes)[:-1], axis=1)
    z = lambda n: jnp.zeros((d, n), w_in.dtype)
    kr0 = jnp.concatenate([d_kr, z(LANES - ROPE_DIM)], axis=1)
    krp = jnp.concatenate([z(NOPE_DIM), d_kr, z(LANES - QK_DIM)], axis=1)
    dt = jnp.concatenate([b_dt, z(LANES - SSM_HEADS)], axis=1)
    return jnp.concatenate([a_u, a_v, b_z, b_xbc, c_h, c_b, c_c, d_q, d_kv, kr0, krp, dt], axis=1).astype(BF16)


def _cumsum(sizes):
    out, s = [], 0
    for v in sizes:
        s += v
        out.append(s)
    return out


def _proj_kernel(h_ref, sh_ref, sc_ref, g_ref, w_ref, ggm_ref, gq_ref, gkv_ref, ws_ref, bst_ref, scbuf_ref, wsc_ref,
                 ya_ref, av_ref, z_ref, xbc_ref, yc_ref, ptail_ref, ql_ref, rows_ref, cbf_ref, krp_ref, dt_ref,
                 *rest, layout, chunk):
    *maybe_ct_ref, hist_ref = rest
    bb, tl, d = h_ref.shape
    n = (_rms(h_ref[...]) * g_ref[...]) * (1.0 + sc_ref[...]) + sh_ref[...]
    n = n.reshape(bb * tl, d).astype(BF16)

    order = ["a_u", "a_v", "c_c", "c_h", "c_b", "b_z", "b_xbc", "d_q", "d_kv", "kr0", "krp", "dt"]

    def issue(name):
        off, w = layout[name]
        return _dot(n, w_ref[:, off:off + w])

    ahead = [issue(order[0])]

    def grp(name):
        i = len(ahead) - 1
        assert order[i] == name
        ahead.append(issue(order[i + 1]) if i + 1 < len(order) else None)
        return ahead[i]

    def put(ref, val):
        ref[...] = val.reshape(bb, tl, val.shape[-1]).astype(ref.dtype)

    a_u = jax.nn.gelu(grp("a_u"))
    a_v = _rms(jax.nn.gelu(grp("a_v"))) * ggm_ref[...]
    put(av_ref, a_v)
    bd = a_u.shape[-1]
    hd = bd // GM_HEADS
    row = lax.broadcasted_iota(jnp.int32, (chunk, chunk), 0)
    col = lax.broadcasted_iota(jnp.int32, (chunk, chunk), 1)
    for hh in range(GM_HEADS):
        wm = jnp.where(col <= row, ws_ref[hh, 0:chunk, 0:chunk], 0.0)
        bias = bst_ref[0:chunk, hh:hh + 1]
        cols = slice(hh * hd, (hh + 1) * hd)
        if chunk == GM_CHUNK:
            wmb = wm.astype(BF16)
            for ci in range(tl // chunk):
                rows = slice(ci * chunk, (ci + 1) * chunk)
                mixed = _dot(wmb, a_v[rows, cols].astype(BF16)) + bias
                ya_ref[0, rows, cols] = a_u[rows, cols] * mixed
        else:
            u3 = a_u[:, cols].reshape(bb, tl, hd)
            v3 = a_v[:, cols].reshape(bb, tl, hd)
            mixed = jnp.zeros((bb, chunk, hd), F32) + bias
            for j in range(chunk):
                mixed = mixed + wm[:, j:j + 1] * v3[:, j:j + 1, :]
            ya_ref[:, :, cols] = u3 * mixed

    kc = wsc_ref.shape[0]

    @pl.when(pl.program_id(1) == 0)
    def _():
        hist_ref[:, 0:8, :] = jnp.zeros((bb, 8, bd), F32)
        hist_ref[:, 8 - (kc - 1):8, :] = scbuf_ref[...]

    @pl.when(pl.program_id(1) > 0)
    def _():
        hist_ref[:, 0:8, :] = hist_ref[:, tl:tl + 8, :]

    hist_ref[:, 8:8 + tl, :] = (grp("c_c") * grp("c_h")).reshape(bb, tl, bd)
    acc = jnp.zeros((bb, tl, bd), F32)
    for i in range(kc):
        s = 8 - (kc - 1) + i
        acc = acc + wsc_ref[i:i + 1, :] * hist_ref[:, s:s + tl, :]
    yc_ref[...] = grp("c_b").reshape(bb, tl, bd) * acc
    ptail_ref[...] = hist_ref[:, tl:tl + 8, :]

    put(z_ref, grp("b_z"))
    put(xbc_ref, grp("b_xbc"))
    put(ql_ref, _rms(grp("d_q")) * gq_ref[...])
    kv = _rms(grp("d_kv")) * gkv_ref[...]
    put(cbf_ref, kv)
    if maybe_ct_ref:
        maybe_ct_ref[0][0] = kv.T.astype(BF16)
    kvw = kv.shape[-1]
    rows_ref[:, :, 0:kvw] = kv.reshape(bb, tl, kvw)
    rows_ref[:, :, kvw:kvw + ROPE_DIM] = grp("kr0")[:, 0:ROPE_DIM].reshape(bb, tl, ROPE_DIM)
    put(krp_ref, grp("krp"))
    put(dt_ref, grp("dt"))


def _proj(h, sh, sc, g, w, ggm, gq, gkv, w_s, b_st, sc_buf, w_sc, layout):
    b, l, d = h.shape
    bb, tl = _tok_blocks(b, l)
    bd, q_lora, kv_lora = ggm.shape[1], gq.shape[1], gkv.shape[1]
    chunk = min(l, GM_CHUNK)
    assert (chunk == GM_CHUNK and bb == 1 and tl % chunk == 0) or chunk == tl
    tok = lambda w_: pl.BlockSpec((bb, tl, w_), lambda i, j: (i, j, 0))
    mod = pl.BlockSpec((bb, 1, d), lambda i, j: (i, 0, 0))
    per_b = lambda r: pl.BlockSpec((bb, r, bd), lambda i, j: (i, 0, 0))
    outs = [(bd, F32), (bd, F32), (bd, F32), (2 * bd, F32), (bd, F32), None, (q_lora, BF16),
            (kv_lora + ROPE_DIM, F32), (kv_lora, BF16), (LANES, F32), (LANES, F32)]
    out_specs = [per_b(8) if o is None else tok(o[0]) for o in outs]
    out_shape = [jax.ShapeDtypeStruct((b, 8, bd), F32) if o is None else jax.ShapeDtypeStruct((b, l, o[0]), o[1])
                 for o in outs]
    if bb == 1 and tl % LANES == 0:
        out_specs.append(pl.BlockSpec((1, kv_lora, tl), lambda i, j: (i, 0, j)))
        out_shape.append(jax.ShapeDtypeStruct((b, kv_lora, l), BF16))
    return pl.pallas_call(
        functools.partial(_proj_kernel, layout=layout, chunk=chunk),
        grid=(b // bb, l // tl),
        in_specs=[tok(d), mod, mod, _resident((1, d)), _resident(w.shape),
                  _resident(ggm.shape), _resident(gq.shape), _resident(gkv.shape),
                  _resident(w_s.shape), _resident(b_st.shape), per_b(sc_buf.shape[1]), _resident(w_sc.shape)],
        out_specs=out_specs,
        out_shape=out_shape,
        scratch_shapes=[pltpu.VMEM((bb, 8 + tl, bd), F32)],
        compiler_params=_cparams(("parallel", "arbitrary")),
        name="proj",
    )(h, sh, sc, g, w, ggm, gq, gkv, w_s, b_st, sc_buf, w_sc)


def _softplus(x):
    return jnp.maximum(x, 0.0) + jnp.log1p(jnp.exp(-jnp.abs(x)))


def _ssm_kernel(xbc_ref, z_ref, dtm_ref, dtt_ref, buf_ref, s0_ref, wc_ref, bc_ref, dtb_ref, dtbt_ref,
                alog_ref, alogt_ref, dsk_ref, gn_ref, y_ref, sout_ref, hist_ref, s_ref):
    q = xbc_ref.shape[1]
    c = xbc_ref.shape[-1]
    bd = z_ref.shape[-1]
    k = wc_ref.shape[0]
    ci = pl.program_id(1)

    @pl.when(ci == 0)
    def _():
        hist_ref[0:8, :] = jnp.zeros((8, c), F32)
        hist_ref[8 - (k - 1):8, :] = buf_ref[0]
        s_ref[...] = s0_ref[0]

    @pl.when(ci > 0)
    def _():
        hist_ref[0:8, :] = hist_ref[q:q + 8, :]

    hist_ref[8:8 + q, :] = xbc_ref[0]
    acc = jnp.zeros((q, c), F32) + bc_ref[...]
    for i in range(k):
        s = 8 - (k - 1) + i
        acc = acc + wc_ref[i:i + 1, :] * hist_ref[s:s + q, :]
    xc = _silu(acc)
    xs = xc[:, 0:bd]
    gw = SSM_STATE
    bmf = [xc[:, bd + g * gw:bd + (g + 1) * gw] for g in range(SSM_GROUPS)]
    bm = [t.astype(BF16) for t in bmf]
    cm = [xc[:, bd + (SSM_GROUPS + g) * gw:bd + (SSM_GROUPS + g + 1) * gw].astype(BF16) for g in range(SSM_GROUPS)]

    row = lax.broadcasted_iota(jnp.int32, (q, q), 0)
    col = lax.broadcasted_iota(jnp.int32, (q, q), 1)
    causal = col <= row
    dt = _softplus(dtm_ref[0][:, 0:SSM_HEADS] + dtb_ref[...])
    dtt = _softplus(dtt_ref[0] + dtbt_ref[...])
    da = dt * (-jnp.exp(alog_ref[...]) * LOG2E)
    dat = dtt * (-jnp.exp(alogt_ref[...]) * LOG2E)
    cum = jnp.dot(causal.astype(F32), da, preferred_element_type=F32, precision=HIGHEST)
    cumt = jnp.dot(dat, (row <= col).astype(F32), preferred_element_type=F32, precision=HIGHEST)
    last = cum[q - 1:q, :]

    lane = lax.broadcasted_iota(jnp.int32, (q, LANES), 1)
    srow = lax.broadcasted_iota(jnp.int32, (LANES, gw), 0)
    p_dim = bd // SSM_HEADS
    rep = SSM_HEADS // SSM_GROUPS
    cbs = [_dot_nt(cm[g], bm[g]) for g in range(SSM_GROUPS)]
    for pair in range(SSM_HEADS // 2):
        h0, h1 = 2 * pair, 2 * pair + 1
        g = h0 // rep
        first = lane < p_dim
        cum_pair = jnp.where(first, cum[:, h0:h0 + 1], cum[:, h1:h1 + 1])
        last_pair = jnp.where(first, last[:, h0:h0 + 1], last[:, h1:h1 + 1])
        dt_pair = jnp.where(first, dt[:, h0:h0 + 1], dt[:, h1:h1 + 1])
        xdt = xs[:, pair * LANES:(pair + 1) * LANES] * dt_pair
        xb = xdt.astype(BF16)
        ys = []
        for hh in (h0, h1):
            dec = jnp.where(causal, jnp.exp2(cum[:, hh:hh + 1] - cumt[hh:hh + 1, :]), 0.0)
            ys.append(_dot((cbs[g] * dec).astype(BF16), xb))
        s_in = s_ref[pair]
        y_pair = jnp.where(first, ys[0], ys[1]) + jnp.exp2(cum_pair) * _dot_nt(cm[g], s_in.astype(BF16))
        y_ref[0, :, pair * LANES:(pair + 1) * LANES] = y_pair
        xw = xdt * jnp.exp2(last_pair - cum_pair)
        bmg = bm[g]
        if q < LANES:
            zpad = jnp.zeros((LANES - q, LANES), F32)
            xw = jnp.concatenate([xw, zpad], axis=0)
            bmg = jnp.concatenate([bmf[g], zpad], axis=0).astype(BF16)
        cs = _dot(xw.T.astype(BF16), bmg)
        cd = jnp.where(srow < p_dim, jnp.exp2(last[:, h0:h0 + 1]), jnp.exp2(last[:, h1:h1 + 1]))
        s_ref[pair] = s_in * cd + cs

    y = (y_ref[0] + dsk_ref[...] * xs) * _silu(z_ref[0])
    gs = bd // SSM_GROUPS
    for g in range(SSM_GROUPS):
        y_ref[0, :, g * gs:(g + 1) * gs] = _rms(y[:, g * gs:(g + 1) * gs]) * gn_ref[:, g * gs:(g + 1) * gs]
    sout_ref[0] = s_ref[...]


def _ssm(xbc, z, dtm, dtt, buf, s0, wc, bc, dtb, dtbt, alog, alogt, dsk, gn):
    b, l, c = xbc.shape
    bd = z.shape[-1]
    q = min(l, SSM_CHUNK)
    assert SSM_STATE == LANES and l % q == 0 and q % 8 == 0 and SSM_HEADS % 2 == 0 and bd // SSM_HEADS * 2 == LANES and (SSM_HEADS // SSM_GROUPS) % 2 == 0
    npair = SSM_HEADS // 2
    tok = lambda w_: pl.BlockSpec((1, q, w_), lambda i, j: (i, j, 0))
    per_b = lambda shp: pl.BlockSpec((1,) + shp, lambda i, j: (i,) + (0,) * len(shp))
    small = [wc, bc, dtb, dtbt, alog, alogt, dsk, gn]
    return pl.pallas_call(
        _ssm_kernel,
        grid=(b, l // q),
        in_specs=[tok(c), tok(bd), tok(LANES), pl.BlockSpec((1, SSM_HEADS, q), lambda i, j: (i, 0, j)),
                  per_b(buf.shape[1:]), per_b(s0.shape[1:])] + [_resident(a.shape) for a in small],
        out_specs=[tok(bd), per_b(s0.shape[1:])],
        out_shape=[jax.ShapeDtypeStruct((b, l, bd), F32), jax.ShapeDtypeStruct(s0.shape, F32)],
        scratch_shapes=[pltpu.VMEM((8 + q, c), F32), pltpu.VMEM((npair, LANES, SSM_STATE), F32)],
        compiler_params=_cparams(("parallel", "arbitrary")),
        name="ssm",
    )(xbc, z, dtm, dtt, buf, s0, *small)


def _rope_swap(x, lane, sign):
    half = ROPE_DIM // 2
    return jnp.where(lane < NOPE_DIM + half, pltpu.roll(x, LANES - half, 1), sign * pltpu.roll(x, half, 1))


def _qk_kernel(*refs, decode):
    if decode:
        (ql_ref, cos_ref, sin_ref, wq_ref, gq_ref, gk_ref, wukt_ref, qt_ref, qa_ref, qb_ref) = refs
    else:
        (ql_ref, cbf_ref, krp_ref, ct_ref, cos_ref, sin_ref, wq_ref, wk_ref, wuvt_ref, gq_ref, gk_ref,
         q_ref, k_ref, vt_ref) = refs
    bb, tl, _ = ql_ref.shape
    m = bb * tl
    ql = ql_ref[...].reshape(m, ql_ref.shape[-1])
    lane = lax.broadcasted_iota(jnp.int32, (m, LANES), 1)
    cos = jnp.broadcast_to(cos_ref[...][None], (bb, tl, LANES)).reshape(m, LANES)
    sin = jnp.broadcast_to(sin_ref[...][None], (bb, tl, LANES)).reshape(m, LANES)

    def norm_rope(x, g):
        xn = x * lax.rsqrt(jnp.sum(x * x, axis=-1, keepdims=True) * (1.0 / QK_DIM) + EPS) * g
        return xn * cos + _rope_swap(xn, lane, 1.0) * sin

    if not decode:
        cbf = cbf_ref[...].reshape(m, cbf_ref.shape[-1])
        krp = krp_ref[...].reshape(m, LANES)
    for hh in range(MLA_HEADS):
        qf = norm_rope(_dot(ql, wq_ref[hh]), gq_ref[...]) * (QK_SCALE * LOG2E)
        if decode:
            gk = gk_ref[...]
            nope = lane < NOPE_DIM
            qn = jnp.where(nope, qf * gk, 0.0).astype(BF16)
            qt_ref[:, hh] = _dot(qn, wukt_ref[hh]).reshape(bb, tl, wukt_ref.shape[-1])
            qa = qf * gk
            qb = _rope_swap(qf, lane, -1.0) * gk
            qa_ref[:, hh] = qa[:, NOPE_DIM:QK_DIM].reshape(bb, tl, ROPE_DIM)
            qb_ref[:, hh] = qb[:, NOPE_DIM:QK_DIM].reshape(bb, tl, ROPE_DIM)
        else:
            q_ref[:, hh] = qf.reshape(bb, tl, LANES).astype(BF16)
            kf = norm_rope(_dot(cbf, wk_ref[hh]) + krp, gk_ref[...])
            k_ref[:, hh] = kf.reshape(bb, tl, LANES).astype(BF16)
            vt_ref[0, hh] = _dot(wuvt_ref[hh], ct_ref[0]).astype(BF16)


def _qk_prompt(ql, cbf, krp, ct, cos, sin, wq, wk, wuvt, gq, gk):
    b, l, _ = ql.shape
    bb, tl = _tok_blocks(b, l)
    assert bb == 1
    vd = wuvt.shape[1]
    tok = lambda w_: pl.BlockSpec((bb, tl, w_), lambda i, j: (i, j, 0))
    tab = pl.BlockSpec((tl, LANES), lambda i, j: (j, 0))
    head = pl.BlockSpec((bb, MLA_HEADS, tl, LANES), lambda i, j: (i, 0, j, 0))
    shp = jax.ShapeDtypeStruct((b, MLA_HEADS, l, LANES), BF16)
    return pl.pallas_call(
        functools.partial(_qk_kernel, decode=False),
        grid=(b // bb, l // tl),
        in_specs=[tok(ql.shape[-1]), tok(cbf.shape[-1]), tok(LANES),
                  pl.BlockSpec((1, ct.shape[1], tl), lambda i, j: (i, 0, j)), tab, tab,
                  _resident(wq.shape), _resident(wk.shape), _resident(wuvt.shape),
                  _resident(gq.shape), _resident(gk.shape)],
        out_specs=[head, head, pl.BlockSpec((1, MLA_HEADS, vd, tl), lambda i, j: (i, 0, 0, j))],
        out_shape=[shp, shp, jax.ShapeDtypeStruct((b, MLA_HEADS, vd, l), BF16)],
        compiler_params=_cparams(("parallel", "parallel")),
        name="qk_prompt",
    )(ql, cbf, krp, ct, cos, sin, wq, wk, wuvt, gq, gk)


def _q_decode(ql, cos, sin, wq, gq, gk, wukt):
    b, l, _ = ql.shape
    bb, tl = _tok_blocks(b, l)
    assert tl == l
    kv = wukt.shape[-1]
    tok = lambda w_: pl.BlockSpec((bb, tl, w_), lambda i: (i, 0, 0))
    head = lambda w_: pl.BlockSpec((bb, MLA_HEADS, tl, w_), lambda i: (i, 0, 0, 0))
    shp = lambda w_: jax.ShapeDtypeStruct((b, MLA_HEADS, l, w_), F32)
    return pl.pallas_call(
        functools.partial(_qk_kernel, decode=True),
        grid=(b // bb,),
        in_specs=[tok(ql.shape[-1]), _resident(cos.shape), _resident(sin.shape),
                  _resident(wq.shape), _resident(gq.shape), _resident(gk.shape), _resident(wukt.shape)],
        out_specs=[head(kv), head(ROPE_DIM), head(ROPE_DIM)],
        out_shape=[shp(kv), shp(ROPE_DIM), shp(ROPE_DIM)],
        compiler_params=_cparams(("parallel",)),
        name="q_decode",
    )(ql, cos, sin, wq, gq, gk, wukt)


def _flash_kernel(qi_ref, ki_ref, q_ref, k_ref, vt_ref, o_ref, m_ref, l_ref, acc_ref):
    tq = q_ref.shape[2]
    tk = k_ref.shape[2]
    step_i = pl.program_id(1)
    qi, ki = qi_ref[step_i], ki_ref[step_i]

    @pl.when(ki == 0)
    def _():
        m_ref[...] = jnp.full(m_ref.shape, -jnp.inf, F32)
        l_ref[...] = jnp.zeros(l_ref.shape, F32)
        acc_ref[...] = jnp.zeros(acc_ref.shape, F32)

    vd = vt_ref.shape[2]

    def step(masked):
        if masked:
            keep = (ki * tk + lax.broadcasted_iota(jnp.int32, (tk, tq), 0)
                    <= qi * tq + lax.broadcasted_iota(jnp.int32, (tk, tq), 1))
        def values(hh, alpha, pb):
            rows = slice(hh * vd, (hh + 1) * vd)
            acc_ref[rows, :] = alpha * acc_ref[rows, :] + _dot(vt_ref[0, hh], pb)

        ahead = [_dot_nt(k_ref[0, hh], q_ref[0, hh]) for hh in range(min(QK_AHEAD, MLA_HEADS))]
        pending = None
        for hh in range(MLA_HEADS):
            st = ahead[hh]
            if hh + QK_AHEAD < MLA_HEADS:
                ahead.append(_dot_nt(k_ref[0, hh + QK_AHEAD], q_ref[0, hh + QK_AHEAD]))
            if pending is not None:
                values(*pending)
            if masked:
                st = jnp.where(keep, st, -jnp.inf)
            m_prev = m_ref[hh:hh + 1, :]
            m_new = jnp.maximum(m_prev, jnp.max(st, axis=0, keepdims=True))
            alpha = jnp.exp2(m_prev - m_new)
            p = jnp.exp2(st - m_new)
            l_ref[hh:hh + 1, :] = alpha * l_ref[hh:hh + 1, :] + jnp.sum(p, axis=0, keepdims=True)
            m_ref[hh:hh + 1, :] = m_new
            pending = (hh, alpha, p.astype(BF16))
        values(*pending)

    straddles = (ki + 1) * tk - 1 > qi * tq

    @pl.when(jnp.logical_not(straddles))
    def _():
        step(False)

    @pl.when(straddles)
    def _():
        step(True)

    @pl.when((ki + 1) * tk >= (qi + 1) * tq)
    def _():
        for hh in range(MLA_HEADS):
            rows = slice(hh * vd, (hh + 1) * vd)
            acc_ref[rows, :] = acc_ref[rows, :] / l_ref[hh:hh + 1, :]
        o_ref[0] = acc_ref[...].T


def _flash(q, k, vt):
    b, h, l, _ = q.shape
    tq, tk = min(l, ATTN_Q_BLOCK), min(l, ATTN_K_BLOCK)
    assert l % tq == 0 and l % tk == 0
    vd = vt.shape[2]
    pairs = [(qi, ki) for qi in range(l // tq) for ki in range(l // tk) if ki * tk <= qi * tq + tq - 1]
    qi_of = jnp.asarray([p[0] for p in pairs], jnp.int32)
    ki_of = jnp.asarray([p[1] for p in pairs], jnp.int32)
    grid_spec = pltpu.PrefetchScalarGridSpec(
        num_scalar_prefetch=2,
        grid=(b, len(pairs)),
        in_specs=[pl.BlockSpec((1, h, tq, LANES), lambda i, s, qo, ko: (i, 0, qo[s], 0)),
                  pl.BlockSpec((1, h, tk, LANES), lambda i, s, qo, ko: (i, 0, ko[s], 0)),
                  pl.BlockSpec((1, h, vd, tk), lambda i, s, qo, ko: (i, 0, 0, ko[s]))],
        out_specs=pl.BlockSpec((1, tq, h * vd), lambda i, s, qo, ko: (i, qo[s], 0)),
        scratch_shapes=[pltpu.VMEM((h, tq), F32), pltpu.VMEM((h, tq), F32), pltpu.VMEM((h * vd, tq), F32)],
    )
    return pl.pallas_call(
        _flash_kernel,
        grid_spec=grid_spec,
        out_shape=jax.ShapeDtypeStruct((b, l, h * vd), F32),
        compiler_params=_cparams(("parallel", "arbitrary")),
        name="flash",
    )(qi_of, ki_of, q, k, vt)


def _decode_kernel(pt_ref, qt_ref, qab_ref, tab_ref, tabn_ref, new_ref, wukt_ref, wuv_ref, pool_ref,
                   o_ref, buf_ref, sem, lhs_ref, ctb_ref, a_ref, s_ref, m_ref, l_ref, acc_ref,
                   *, pages, group, layer):
    seq, j = pl.program_id(0), pl.program_id(1)
    nseq, nj = pl.num_programs(0), pl.num_programs(1)
    nrow = qt_ref.shape[1]
    lq = nrow // MLA_HEADS
    kvw = qt_ref.shape[2]
    nkt = wukt_ref.shape[0]
    psz = buf_ref.shape[3]
    step = seq * nj + j
    slot = step % 2

    def page_copy(sq, chunk, pi, sl):
        return pltpu.make_async_copy(pool_ref.at[layer, pt_ref[sq, chunk * pages + pi]], buf_ref.at[sl, pi], sem.at[sl])

    @pl.when(step == 0)
    def _():
        for pi in range(pages):
            page_copy(seq, j, pi, slot).start()

    @pl.when(step + 1 < nseq * nj)
    def _():
        wrap = j + 1 == nj
        nxt_seq = jnp.where(wrap, seq + 1, seq)
        nxt_j = jnp.where(wrap, 0, j + 1)
        for pi in range(pages):
            page_copy(nxt_seq, nxt_j, pi, 1 - slot).start()

    @pl.when(j == 0)
    def _():
        m_ref[...] = jnp.full(m_ref.shape, SCORE_FLOOR, F32)
        l_ref[...] = jnp.zeros(l_ref.shape, F32)
        acc_ref[...] = jnp.zeros(acc_ref.shape, F32)
        lhs_ref[0:nkt, :] = wukt_ref[...]
        lhs_ref[nkt:nkt + nrow, :] = qt_ref[0].astype(BF16)
        s_ref[1] = jnp.full(s_ref.shape[1:], -jnp.inf, F32)
        ctb_ref[1] = jnp.zeros(ctb_ref.shape[1:], BF16)

    qab = qab_ref[0].astype(BF16)
    cur = j % 2

    for pi in range(pages):
        page_copy(seq, j, pi, slot).wait()
    page_refs = [buf_ref.at[slot, pi] for pi in range(pages)]

    def scores(ct, krt, cct, snt):
        nk = ct.shape[1]
        ctb = ct.astype(BF16)
        a = _dot(lhs_ref[...], ctb)
        kt = a[0:nkt]
        ss = jnp.sum((kt * kt).reshape(MLA_HEADS, nkt // MLA_HEADS, nk), axis=1)
        ss = ss + jnp.sum(krt * krt, axis=0, keepdims=True)
        inv = lax.rsqrt(ss * (1.0 / QK_DIM) + EPS)
        feats = jnp.concatenate([krt * cct, krt * snt], axis=0).astype(BF16)
        s = a[nkt:nkt + nrow] + _dot(qab, feats)
        return (s.reshape(MLA_HEADS, lq, nk) * inv[:, None, :]).reshape(nrow, nk), ctb

    def softmax_stats(s):
        m_prev = m_ref[...]
        m_new = jnp.maximum(m_prev, jnp.max(s, axis=-1, keepdims=True))
        alpha = jnp.exp2(m_prev - m_new)
        p = jnp.exp2(s - m_new)
        l_ref[...] = alpha * l_ref[...] + jnp.sum(p, axis=-1, keepdims=True)
        m_ref[...] = m_new
        return alpha, p.astype(BF16)

    for pi in range(pages):
        ctb_ref[cur, :, pi * psz:(pi + 1) * psz] = page_refs[pi][0:kvw, :].astype(BF16)
    a_ref[...] = _dot(lhs_ref[...], ctb_ref[cur])
    alpha_prev, pb_prev = softmax_stats(s_ref[1 - cur])
    pv_prev = _dot_nt(pb_prev, ctb_ref[1 - cur])
    for g0 in range(0, pages, group):
        refs = page_refs[g0:g0 + group]
        lanes = slice(g0 * psz, (g0 + group) * psz)
        nk = group * psz
        kt = a_ref[0:nkt, lanes]
        krt = jnp.concatenate([r[kvw:kvw + ROPE_DIM, :] for r in refs], axis=1) if group > 1 else refs[0][kvw:kvw + ROPE_DIM, :]
        ss = jnp.sum((kt * kt).reshape(MLA_HEADS, nkt // MLA_HEADS, nk), axis=1)
        ss = ss + jnp.sum(krt * krt, axis=0, keepdims=True)
        inv = lax.rsqrt(ss * (1.0 / QK_DIM) + EPS)
        feats = jnp.concatenate([krt * tab_ref[0, :, lanes], krt * tab_ref[1, :, lanes]], axis=0).astype(BF16)
        s = a_ref[nkt:nkt + nrow, lanes] + _dot(qab, feats)
        s_ref[cur, :, lanes] = (s.reshape(MLA_HEADS, lq, nk) * inv[:, None, :]).reshape(nrow, nk)
    acc_ref[...] = alpha_prev * acc_ref[...] + pv_prev

    @pl.when(j == pl.num_programs(1) - 1)
    def _():
        nk = new_ref.shape[2]
        tq = lax.broadcasted_iota(jnp.int32, (nrow, nk), 0) % lq
        tk = lax.broadcasted_iota(jnp.int32, (nrow, nk), 1)
        s, ctb = scores(new_ref[0, 0:kvw, :], new_ref[0, kvw:kvw + ROPE_DIM, :], tabn_ref[0], tabn_ref[1])
        alpha_l, pb_l = softmax_stats(s_ref[cur])
        acc_ref[...] = alpha_l * acc_ref[...] + _dot_nt(pb_l, ctb_ref[cur])
        alpha_n, pb_n = softmax_stats(jnp.where(tk <= tq, s, -jnp.inf))
        acc_ref[...] = alpha_n * acc_ref[...] + _dot_nt(pb_n, ctb)
        ctx = (acc_ref[...] / l_ref[...]).astype(BF16)
        vd = wuv_ref.shape[-1]
        for hh in range(MLA_HEADS):
            o_ref[0, :, hh * vd:(hh + 1) * vd] = _dot(ctx[hh * lq:(hh + 1) * lq, :], wuv_ref[hh])


def _decode(page_table, qt, qab, tab, tabn, rows_new_t, wukt, wuv, pool_t, layer):
    bs, nrow, kvw = qt.shape
    n_pages = page_table.shape[1]
    cw, psz = pool_t.shape[2], pool_t.shape[3]
    pages = math.gcd(n_pages, DECODE_PAGES)
    group = math.gcd(pages, DECODE_GROUP)
    lq = nrow // MLA_HEADS
    vd = wuv.shape[-1]
    per_s = lambda shp: pl.BlockSpec((1,) + shp, lambda s, j, pt: (s,) + (0,) * len(shp))
    grid_spec = pltpu.PrefetchScalarGridSpec(
        num_scalar_prefetch=1,
        grid=(bs, n_pages // pages),
        in_specs=[per_s((nrow, kvw)), per_s((nrow, 2 * ROPE_DIM)),
                  pl.BlockSpec((2, ROPE_DIM, pages * psz), lambda s, j, pt: (0, 0, j)),
                  _resident(tabn.shape), per_s(rows_new_t.shape[1:]),
                  _resident(wukt.shape), _resident(wuv.shape), pl.BlockSpec(memory_space=pl.ANY)],
        out_specs=per_s((lq, MLA_HEADS * vd)),
        scratch_shapes=[pltpu.VMEM((2, pages, cw, psz), F32), pltpu.SemaphoreType.DMA((2,)),
                        pltpu.VMEM((wukt.shape[0] + nrow, kvw), BF16),
                        pltpu.VMEM((2, kvw, pages * psz), BF16),
                        pltpu.VMEM((wukt.shape[0] + nrow, pages * psz), F32), pltpu.VMEM((2, nrow, pages * psz), F32),
                        pltpu.VMEM((nrow, 1), F32), pltpu.VMEM((nrow, 1), F32), pltpu.VMEM((nrow, kvw), F32)],
    )
    return pl.pallas_call(
        functools.partial(_decode_kernel, pages=pages, group=group, layer=layer),
        grid_spec=grid_spec,
        out_shape=jax.ShapeDtypeStruct((bs, lq, MLA_HEADS * vd), F32),
        compiler_params=_cparams(("arbitrary", "arbitrary")),
        name="decode",
    )(page_table, qt, qab, tab, tabn, rows_new_t, wukt, wuv, pool_t)


def _merge_kernel(h_ref, sh_ref, sc_ref, gt_ref, g_ref, ya_ref, yb_ref, yc_ref, yd_ref,
                  wg_ref, bg_ref, wb_ref, wo_ref, o_ref):
    bb, tl, d = h_ref.shape
    m = bb * tl
    h = h_ref[...]
    n = (_rms(h) * g_ref[...]) * (1.0 + sc_ref[...]) + sh_ref[...]
    n = n.reshape(m, d).astype(BF16)
    merged = None
    for r, y_ref in enumerate((ya_ref, yb_ref, yc_ref, yd_ref)):
        gate = jax.nn.sigmoid(_dot(n, wg_ref[r]) + bg_ref[r])
        term = gate * _dot(y_ref[...].reshape(m, y_ref.shape[-1]).astype(BF16), wb_ref[r])
        merged = term if merged is None else merged + term
    out = _dot(merged.astype(BF16), wo_ref[...])
    o_ref[...] = h + gt_ref[...] * out.reshape(bb, tl, d)


def _merge(h, sh, sc, gt, g, ys, wg, bg, wb, wo):
    b, l, d = h.shape
    bb, tl = _tok_blocks(b, l)
    tok = lambda w_: pl.BlockSpec((bb, tl, w_), lambda i, j: (i, j, 0))
    mod = pl.BlockSpec((bb, 1, d), lambda i, j: (i, 0, 0))
    return pl.pallas_call(
        _merge_kernel,
        grid=(b // bb, l // tl),
        in_specs=[tok(d), mod, mod, mod, _resident((1, d))] + [tok(y.shape[-1]) for y in ys]
                 + [_resident(wg.shape), _resident(bg.shape), _resident(wb.shape), _resident(wo.shape)],
        out_specs=tok(d),
        out_shape=jax.ShapeDtypeStruct(h.shape, F32),
        compiler_params=_cparams(("parallel", "parallel")),
        name="merge",
    )(h, sh, sc, gt, g, *ys, wg, bg, wb, wo)


def _rope_tables(pos):
    half = ROPE_DIM // 2
    inv = ROPE_THETA ** (-jnp.arange(half, dtype=F32) / half)
    ang = pos.astype(F32)[:, None] * inv
    return jnp.cos(ang), jnp.sin(ang)


def _lane_tables(pos):
    cos, sin = _rope_tables(pos)
    n = pos.shape[0]
    cos_t = jnp.concatenate([jnp.ones((n, NOPE_DIM), F32), cos, cos, jnp.zeros((n, LANES - QK_DIM), F32)], axis=1)
    sin_t = jnp.concatenate([jnp.zeros((n, NOPE_DIM), F32), -sin, sin, jnp.zeros((n, LANES - QK_DIM), F32)], axis=1)
    return cos_t, sin_t


def _pad_lanes(x, before, total):
    pad = [(0, 0)] * (x.ndim - 1) + [(before, total - before - x.shape[-1])]
    return jnp.pad(x, pad)


def _layer_weights(lw):
    d = lw["w_in"].shape[0]
    bd = lw["g_gm_v"].shape[0]
    q_lora, kv_lora = lw["g_q_lat"].shape[0], lw["g_kv_lat"].shape[0]
    layout, _ = _proj_layout(bd, q_lora, kv_lora)
    w = {}
    w["layout"] = layout
    w["w_ada"] = lw["w_ada"].astype(BF16)
    w["b_ada"] = lw["b_ada"][None, :]
    w["g_norm"] = [lw["g_norm"][i][None, :] for i in range(3)]
    w["w_ffn_in"] = [lw["w_ffn_in"][i].astype(BF16) for i in range(2)]
    w["w_ffn_out"] = [lw["w_ffn_out"][i].astype(BF16) for i in range(2)]
    w["w_proj"] = _proj_weight(lw["w_in"], bd, q_lora, kv_lora)
    w["g_gm_v"] = lw["g_gm_v"][None, :]
    w["g_q_lat"] = lw["g_q_lat"][None, :]
    w["g_kv_lat"] = lw["g_kv_lat"][None, :]
    w["w_spatial"] = lw["w_spatial"]
    w["b_spatial_t"] = lw["b_spatial"].T
    w["w_ssm_conv"] = lw["w_ssm_conv"]
    w["b_ssm_conv"] = lw["b_ssm_conv"][None, :]
    w["dt_bias"] = lw["dt_bias"][None, :]
    w["dt_bias_t"] = lw["dt_bias"][:, None]
    w["a_log"] = lw["a_log"][None, :]
    w["a_log_t"] = lw["a_log"][:, None]
    w["d_skip"] = jnp.repeat(lw["d_skip"], bd // SSM_HEADS)[None, :]
    w["g_ssm_norm"] = lw["g_ssm_norm"][None, :]
    w["w_sc_conv"] = lw["w_sc_conv"]
    w["wq"] = _pad_lanes(jnp.moveaxis(lw["w_uq"], 1, 0), 0, LANES).astype(BF16)
    wuk = jnp.moveaxis(lw["w_uk"], 1, 0)
    w["wk"] = _pad_lanes(wuk, 0, LANES).astype(BF16)
    wukt = jnp.swapaxes(wuk, 1, 2)
    w["wukt_pad"] = jnp.pad(wukt, ((0, 0), (0, LANES - NOPE_DIM), (0, 0))).astype(BF16)
    w["wukt_flat"] = wukt.reshape(MLA_HEADS * NOPE_DIM, kv_lora).astype(BF16)
    w["wuv"] = jnp.moveaxis(lw["w_uv"], 1, 0).astype(BF16)
    w["wuvt"] = jnp.swapaxes(w["wuv"], 1, 2)
    w["gq"] = _pad_lanes(lw["g_qk"][0][None, :], 0, LANES)
    w["gk"] = _pad_lanes(lw["g_qk"][1][None, :], 0, LANES)
    w["w_gate"] = lw["w_gate"].astype(BF16)
    w["b_gate"] = lw["b_gate"][:, None, :]
    w["w_branch_out"] = lw["w_branch_out"].astype(BF16)
    w["w_out"] = lw["w_out"].astype(BF16)
    return w


def _trunk_layer(x, mod, w, ssm_buf, ssm_s0, sc_buf, attn_fn):
    b, l, d = x.shape
    sh1, sc1, gt1, sh2, sc2, gt2, sh3, sc3, gt3 = [m[:, None, :] for m in jnp.split(mod, N_MOD, axis=-1)]
    h = _ffn(x, sh1, sc1, gt1, w["g_norm"][0], w["w_ffn_in"][0], w["w_ffn_out"][0])
    (y_a, a_v, b_z, b_xbc, y_c, c_ptail, ql, rows, cbf, krp, dtm, *ct) = _proj(
        h, sh2, sc2, w["g_norm"][1], w["w_proj"], w["g_gm_v"], w["g_q_lat"], w["g_kv_lat"],
        w["w_spatial"], w["b_spatial_t"], sc_buf, w["w_sc_conv"], w["layout"])
    sc_buf_new = jnp.concatenate([sc_buf, c_ptail[:, -min(l, 8):]], axis=1)[:, -(SC_WIDTH - 1):]
    dtt = jnp.swapaxes(dtm[:, :, 0:SSM_HEADS], 1, 2)
    npair = SSM_HEADS // 2
    s0 = ssm_s0.reshape(b, npair, LANES, SSM_STATE)
    y_b, s_new = _ssm(b_xbc, b_z, dtm, dtt, ssm_buf, s0, w["w_ssm_conv"], w["b_ssm_conv"],
                      w["dt_bias"], w["dt_bias_t"], w["a_log"], w["a_log_t"], w["d_skip"], w["g_ssm_norm"])
    s_new = s_new.reshape(ssm_s0.shape)
    ssm_buf_new = jnp.concatenate([ssm_buf, b_xbc], axis=1)[:, -(SSM_CONV - 1):]
    y_d = attn_fn(ql, rows, cbf, krp, *ct)
    h = _merge(h, sh2, sc2, gt2, w["g_norm"][1], (y_a, y_b, y_c, y_d), w["w_gate"], w["b_gate"],
               w["w_branch_out"], w["w_out"])
    h = _ffn(h, sh3, sc3, gt3, w["g_norm"][2], w["w_ffn_in"][1], w["w_ffn_out"][1])
    return h, rows, ssm_buf_new, s_new, sc_buf_new, a_v


def kernel(x_prompt, x_sample, c_prompt, c_sample, cache_mla, page_table, state_ssm, state_ssm_conv,
           state_short_conv, w_ada, b_ada, g_norm, w_ffn_in, w_ffn_out, w_in, g_gm_v, w_spatial, b_spatial,
           w_ssm_conv, b_ssm_conv, dt_bias, a_log, d_skip, g_ssm_norm, w_sc_conv, g_q_lat, w_uq, g_kv_lat,
           w_uk, w_uv, g_qk, w_branch_out, w_gate, b_gate, w_out):
    params = dict(w_ada=w_ada, b_ada=b_ada, g_norm=g_norm, w_ffn_in=w_ffn_in, w_ffn_out=w_ffn_out, w_in=w_in,
                  g_gm_v=g_gm_v, w_spatial=w_spatial, b_spatial=b_spatial, w_ssm_conv=w_ssm_conv,
                  b_ssm_conv=b_ssm_conv, dt_bias=dt_bias, a_log=a_log, d_skip=d_skip, g_ssm_norm=g_ssm_norm,
                  w_sc_conv=w_sc_conv, g_q_lat=g_q_lat, w_uq=w_uq, g_kv_lat=g_kv_lat, w_uk=w_uk, w_uv=w_uv,
                  g_qk=g_qk, w_branch_out=w_branch_out, w_gate=w_gate, b_gate=b_gate, w_out=w_out)
    depth = w_ada.shape[0]
    bp, lp, d = x_prompt.shape
    bs, ls, _ = x_sample.shape
    page = cache_mla.shape[2]
    past = page_table.shape[1] * page
    bd = g_gm_v.shape[1]
    conv_ch = w_ssm_conv.shape[-1]

    pos_p = jnp.arange(lp, dtype=jnp.int32)
    pos_s = past + jnp.arange(ls, dtype=jnp.int32)
    cos_p, sin_p = _lane_tables(pos_p)
    cos_s, sin_s = _lane_tables(pos_s)
    def key_tables(pos, width):
        cos, sin = _rope_tables(pos)
        t = jnp.stack([jnp.concatenate([cos, cos], axis=1).T, jnp.concatenate([sin, sin], axis=1).T])
        return jnp.pad(t, ((0, 0), (0, 0), (0, width - pos.shape[0])))

    tab_k = key_tables(jnp.arange(past, dtype=jnp.int32), past)
    new_w = -(-ls // LANES) * LANES
    tab_n = key_tables(pos_s, new_w)
    pool_t = jnp.swapaxes(cache_mla, 2, 3)
    c_all = jnp.concatenate([c_prompt, c_sample], axis=0)

    yp, ys = x_prompt, x_sample
    outs = [[] for _ in range(9)]
    for layer in range(depth):
        w = _layer_weights({k: v[layer] for k, v in params.items()})
        mod = _ada(c_all, w["w_ada"], w["b_ada"])

        def attn_prompt(ql, rows, cbf, krp, ct, w=w):
            q, k, vt = _qk_prompt(ql, cbf, krp, ct, cos_p, sin_p, w["wq"], w["wk"], w["wuvt"], w["gq"], w["gk"])
            return _flash(q, k, vt)

        def attn_sample(ql, rows, cbf, krp, *unused, w=w, layer=layer):
            qt, qa, qb = _q_decode(ql, cos_s, sin_s, w["wq"], w["gq"], w["gk"], w["wukt_pad"])
            flat = lambda t: t.reshape(bs, MLA_HEADS * ls, t.shape[-1])
            rows_t = jnp.pad(jnp.swapaxes(rows, 1, 2), ((0, 0), (0, 0), (0, new_w - ls)))
            return _decode(page_table, flat(qt), flat(jnp.concatenate([qa, qb], axis=-1)), tab_k, tab_n, rows_t,
                           w["wukt_flat"], w["wuv"], pool_t, layer)

        yp, r, cb, s, scb, _ = _trunk_layer(
            yp, mod[:bp], w,
            jnp.zeros((bp, SSM_CONV - 1, conv_ch), F32),
            jnp.zeros((bp, SSM_HEADS, bd // SSM_HEADS, SSM_STATE), F32),
            jnp.zeros((bp, SC_WIDTH - 1, bd), F32), attn_prompt)
        for lst, v in zip((outs[0], outs[2], outs[4], outs[6]), (r, s, cb, scb)):
            lst.append(v)
        ys, r, cb, s, scb, v = _trunk_layer(
            ys, mod[bp:], w, state_ssm_conv[layer], state_ssm[layer], state_short_conv[layer], attn_sample)
        for lst, val in zip((outs[1], outs[3], outs[5], outs[7], outs[8]), (r, s, cb, scb, v)):
            lst.append(val)
    st = jnp.stack
    return (yp, ys, st(outs[0]), st(outs[1]), st(outs[2]), st(outs[3]), st(outs[4]), st(outs[5]),
            st(outs[6]), st(outs[7]), st(outs[8]))
```

```python
import functools
import math

import jax
import jax.numpy as jnp
from jax import lax
from jax.experimental import pallas as pl
from jax.experimental.pallas import tpu as pltpu

F32 = jnp.float32
BF16 = jnp.bfloat16
HIGHEST = lax.Precision.HIGHEST

EPS = 1e-6
LANES = 128
N_MOD = 9
GM_CHUNK = 128
GM_HEADS = 4
SSM_HEADS = 8
SSM_GROUPS = 2
SSM_STATE = 128
SSM_CHUNK = 128
SSM_CONV = 4
SC_WIDTH = 3
MLA_HEADS = 8
NOPE_DIM = 64
ROPE_DIM = 32
QK_DIM = NOPE_DIM + ROPE_DIM
ROPE_THETA = 10000.0
QK_SCALE = 1.0 / math.sqrt(QK_DIM)
LOG2E = math.log2(math.e)
SCORE_FLOOR = float(jnp.finfo(jnp.float32).min)
ROW_BLOCK = 512
FFN_ROW_BLOCK = 512
ATTN_Q_BLOCK = 512
ATTN_K_BLOCK = 512
QK_AHEAD = 1
DECODE_PAGES = 32
DECODE_GROUP = 4
VMEM_LIMIT = 56 * 1024 * 1024


def _cparams(sem):
    return pltpu.CompilerParams(dimension_semantics=sem, vmem_limit_bytes=VMEM_LIMIT)


def _resident(shape):
    nd = len(shape)
    return pl.BlockSpec(shape, lambda *_: (0,) * nd, pipeline_mode=pl.Buffered(1))


def _tok_blocks(b, l, rows=ROW_BLOCK):
    tl = min(l, rows)
    bb = min(b, max(1, rows // tl))
    assert l % tl == 0 and b % bb == 0 and tl % 8 == 0
    return bb, tl


def _silu(x):
    return x * jax.nn.sigmoid(x)


def _rms(x):
    return x * lax.rsqrt(jnp.mean(x * x, axis=-1, keepdims=True) + EPS)


def _dot(a, b):
    return jnp.dot(a, b, preferred_element_type=F32)


def _dot_nt(a, b, precision=None):
    return lax.dot_general(a, b, (((1,), (1,)), ((), ())), preferred_element_type=F32, precision=precision)


def _ada_kernel(c_ref, w_ref, b_ref, o_ref):
    o_ref[...] = _dot(_silu(c_ref[...]).astype(BF16), w_ref[...]) + b_ref[...]


def _ada(c, w, b):
    bt, d = c.shape
    n = w.shape[1]
    return pl.pallas_call(
        _ada_kernel,
        grid=(n // d,),
        in_specs=[pl.BlockSpec((bt, d), lambda j: (0, 0)),
                  pl.BlockSpec((d, d), lambda j: (0, j)),
                  pl.BlockSpec((1, d), lambda j: (0, j))],
        out_specs=pl.BlockSpec((bt, d), lambda j: (0, j)),
        out_shape=jax.ShapeDtypeStruct((bt, n), F32),
        compiler_params=_cparams(("arbitrary",)),
        name="ada",
    )(c, w, b)


def _ffn_kernel(x_ref, sh_ref, sc_ref, gt_ref, g_ref, win_ref, wout_ref, o_ref, acc_ref, *, tf):
    bb, tl, d = x_ref.shape
    f = wout_ref.shape[0]
    x = x_ref[...]
    xn = (_rms(x) * g_ref[...]) * (1.0 + sc_ref[...]) + sh_ref[...]
    xn = xn.reshape(bb * tl, d).astype(BF16)
    for i in range(f // tf):
        g = _dot(xn, win_ref[:, i * tf:(i + 1) * tf])
        u = _dot(xn, win_ref[:, f + i * tf:f + (i + 1) * tf])
        a = (_silu(g) * u).astype(BF16)
        part = _dot(a, wout_ref[i * tf:(i + 1) * tf, :])
        if i == 0:
            acc_ref[...] = part
        else:
            acc_ref[...] += part
    o_ref[...] = x + 0.5 * gt_ref[...] * acc_ref[...].reshape(bb, tl, d)


def _ffn(x, sh, sc, gt, g, w_in, w_out):
    b, l, d = x.shape
    bb, tl = _tok_blocks(b, l, FFN_ROW_BLOCK)
    f = w_out.shape[0]
    tf = 256 if f % 256 == 0 else LANES
    tok = pl.BlockSpec((bb, tl, d), lambda i, j: (i, j, 0))
    mod = pl.BlockSpec((bb, 1, d), lambda i, j: (i, 0, 0))
    return pl.pallas_call(
        functools.partial(_ffn_kernel, tf=tf),
        grid=(b // bb, l // tl),
        in_specs=[tok, mod, mod, mod, _resident((1, d)), _resident(w_in.shape), _resident(w_out.shape)],
        out_specs=tok,
        out_shape=jax.ShapeDtypeStruct(x.shape, F32),
        scratch_shapes=[pltpu.VMEM((bb * tl, d), F32)],
        compiler_params=_cparams(("parallel", "parallel")),
        name="ffn",
    )(x, sh, sc, gt, g, w_in, w_out)


_PROJ_GROUPS = ("a_u", "a_v", "b_z", "b_xbc", "c_h", "c_b", "c_c", "d_q", "d_kv", "kr0", "krp", "dt")


def _proj_layout(bd, q_lora, kv_lora):
    widths = dict(a_u=bd, a_v=bd, b_z=bd, b_xbc=2 * bd, c_h=bd, c_b=bd, c_c=bd, d_q=q_lora, d_kv=kv_lora,
                  kr0=LANES, krp=LANES, dt=LANES)
    off, layout = 0, {}
    for name in _PROJ_GROUPS:
        layout[name] = (off, widths[name])
        off += widths[name]
    return layout, off


def _proj_weight(w_in, bd, q_lora, kv_lora):
    d = w_in.shape[0]
    sizes = (bd, bd, bd, 2 * bd, SSM_HEADS, bd, bd, bd, q_lora, kv_lora, ROPE_DIM)
    a_u, a_v, b_z, b_xbc, b_dt, c_h, c_b, c_c, d_q, d_kv, d_kr = jnp.split(w_in, _cumsum(sizes)[:-1], axis=1)
    z = lambda n: jnp.zeros((d, n), w_in.dtype)
    kr0 = jnp.concatenate([d_kr, z(LANES - ROPE_DIM)], axis=1)
    krp = jnp.concatenate([z(NOPE_DIM), d_kr, z(LANES - QK_DIM)], axis=1)
    dt = jnp.concatenate([b_dt, z(LANES - SSM_HEADS)], axis=1)
    return jnp.concatenate([a_u, a_v, b_z, b_xbc, c_h, c_b, c_c, d_q, d_kv, kr0, krp, dt], axis=1).astype(BF16)


def _cumsum(sizes):
    out, s = [], 0
    for v in sizes:
        s += v
        out.append(s)
    return out


def _proj_kernel(h_ref, sh_ref, sc_ref, g_ref, w_ref, ggm_ref, gq_ref, gkv_ref, ws_ref, bst_ref, scbuf_ref, wsc_ref,
                 ya_ref, av_ref, z_ref, xbc_ref, yc_ref, ptail_ref, ql_ref, rows_ref, cbf_ref, krp_ref, dt_ref,
                 *rest, layout, chunk):
    *maybe_ct_ref, hist_ref = rest
    bb, tl, d = h_ref.shape
    n = (_rms(h_ref[...]) * g_ref[...]) * (1.0 + sc_ref[...]) + sh_ref[...]
    n = n.reshape(bb * tl, d).astype(BF16)

    order = ["a_u", "a_v", "c_c", "c_h", "c_b", "b_z", "b_xbc", "d_q", "d_kv", "kr0", "krp", "dt"]

    def issue(name):
        off, w = layout[name]
        return _dot(n, w_ref[:, off:off + w])

    ahead = [issue(order[0])]

    def grp(name):
        i = len(ahead) - 1
        assert order[i] == name
        ahead.append(issue(order[i + 1]) if i + 1 < len(order) else None)
        return ahead[i]

    def put(ref, val):
        ref[...] = val.reshape(bb, tl, val.shape[-1]).astype(ref.dtype)

    a_u = jax.nn.gelu(grp("a_u"))
    a_v = _rms(jax.nn.gelu(grp("a_v"))) * ggm_ref[...]
    put(av_ref, a_v)
    bd = a_u.shape[-1]
    hd = bd // GM_HEADS
    row = lax.broadcasted_iota(jnp.int32, (chunk, chunk), 0)
    col = lax.broadcasted_iota(jnp.int32, (chunk, chunk), 1)
    for hh in range(GM_HEADS):
        wm = jnp.where(col <= row, ws_ref[hh, 0:chunk, 0:chunk], 0.0)
        bias = bst_ref[0:chunk, hh:hh + 1]
        cols = slice(hh * hd, (hh + 1) * hd)
        if chunk == GM_CHUNK:
            wmb = wm.astype(BF16)
            for ci in range(tl // chunk):
                rows = slice(ci * chunk, (ci + 1) * chunk)
                mixed = _dot(wmb, a_v[rows, cols].astype(BF16)) + bias
                ya_ref[0, rows, cols] = a_u[rows, cols] * mixed
        else:
            u3 = a_u[:, cols].reshape(bb, tl, hd)
            v3 = a_v[:, cols].reshape(bb, tl, hd)
            mixed = jnp.zeros((bb, chunk, hd), F32) + bias
            for j in range(chunk):
                mixed = mixed + wm[:, j:j + 1] * v3[:, j:j + 1, :]
            ya_ref[:, :, cols] = u3 * mixed

    kc = wsc_ref.shape[0]

    @pl.when(pl.program_id(1) == 0)
    def _():
        hist_ref[:, 0:8, :] = jnp.zeros((bb, 8, bd), F32)
        hist_ref[:, 8 - (kc - 1):8, :] = scbuf_ref[...]

    @pl.when(pl.program_id(1) > 0)
    def _():
        hist_ref[:, 0:8, :] = hist_ref[:, tl:tl + 8, :]

    hist_ref[:, 8:8 + tl, :] = (grp("c_c") * grp("c_h")).reshape(bb, tl, bd)
    acc = jnp.zeros((bb, tl, bd), F32)
    for i in range(kc):
        s = 8 - (kc - 1) + i
        acc = acc + wsc_ref[i:i + 1, :] * hist_ref[:, s:s + tl, :]
    yc_ref[...] = grp("c_b").reshape(bb, tl, bd) * acc
    ptail_ref[...] = hist_ref[:, tl:tl + 8, :]

    put(z_ref, grp("b_z"))
    put(xbc_ref, grp("b_xbc"))
    put(ql_ref, _rms(grp("d_q")) * gq_ref[...])
    kv = _rms(grp("d_kv")) * gkv_ref[...]
    put(cbf_ref, kv)
    if maybe_ct_ref:
        maybe_ct_ref[0][0] = kv.T.astype(BF16)
    kvw = kv.shape[-1]
    rows_ref[:, :, 0:kvw] = kv.reshape(bb, tl, kvw)
    rows_ref[:, :, kvw:kvw + ROPE_DIM] = grp("kr0")[:, 0:ROPE_DIM].reshape(bb, tl, ROPE_DIM)
    put(krp_ref, grp("krp"))
    put(dt_ref, grp("dt"))


def _proj(h, sh, sc, g, w, ggm, gq, gkv, w_s, b_st, sc_buf, w_sc, layout):
    b, l, d = h.shape
    bb, tl = _tok_blocks(b, l)
    bd, q_lora, kv_lora = ggm.shape[1], gq.shape[1], gkv.shape[1]
    chunk = min(l, GM_CHUNK)
    assert (chunk == GM_CHUNK and bb == 1 and tl % chunk == 0) or chunk == tl
    tok = lambda w_: pl.BlockSpec((bb, tl, w_), lambda i, j: (i, j, 0))
    mod = pl.BlockSpec((bb, 1, d), lambda i, j: (i, 0, 0))
    per_b = lambda r: pl.BlockSpec((bb, r, bd), lambda i, j: (i, 0, 0))
    outs = [(bd, F32), (bd, F32), (bd, F32), (2 * bd, F32), (bd, F32), None, (q_lora, BF16),
            (kv_lora + ROPE_DIM, F32), (kv_lora, BF16), (LANES, F32), (LANES, F32)]
    out_specs = [per_b(8) if o is None else tok(o[0]) for o in outs]
    out_shape = [jax.ShapeDtypeStruct((b, 8, bd), F32) if o is None else jax.ShapeDtypeStruct((b, l, o[0]), o[1])
                 for o in outs]
    if bb == 1 and tl % LANES == 0:
        out_specs.append(pl.BlockSpec((1, kv_lora, tl), lambda i, j: (i, 0, j)))
        out_shape.append(jax.ShapeDtypeStruct((b, kv_lora, l), BF16))
    return pl.pallas_call(
        functools.partial(_proj_kernel, layout=layout, chunk=chunk),
        grid=(b // bb, l // tl),
        in_specs=[tok(d), mod, mod, _resident((1, d)), _resident(w.shape),
                  _resident(ggm.shape), _resident(gq.shape), _resident(gkv.shape),
                  _resident(w_s.shape), _resident(b_st.shape), per_b(sc_buf.shape[1]), _resident(w_sc.shape)],
        out_specs=out_specs,
        out_shape=out_shape,
        scratch_shapes=[pltpu.VMEM((bb, 8 + tl, bd), F32)],
        compiler_params=_cparams(("parallel", "arbitrary")),
        name="proj",
    )(h, sh, sc, g, w, ggm, gq, gkv, w_s, b_st, sc_buf, w_sc)


def _softplus(x):
    return jnp.maximum(x, 0.0) + jnp.log1p(jnp.exp(-jnp.abs(x)))


def _ssm_kernel(xbc_ref, z_ref, dtm_ref, dtt_ref, buf_ref, s0_ref, wc_ref, bc_ref, dtb_ref, dtbt_ref,
                alog_ref, alogt_ref, dsk_ref, gn_ref, y_ref, sout_ref, hist_ref, s_ref):
    q = xbc_ref.shape[1]
    c = xbc_ref.shape[-1]
    bd = z_ref.shape[-1]
    k = wc_ref.shape[0]
    ci = pl.program_id(1)

    @pl.when(ci == 0)
    def _():
        hist_ref[0:8, :] = jnp.zeros((8, c), F32)
        hist_ref[8 - (k - 1):8, :] = buf_ref[0]
        s_ref[...] = s0_ref[0]

    @pl.when(ci > 0)
    def _():
        hist_ref[0:8, :] = hist_ref[q:q + 8, :]

    hist_ref[8:8 + q, :] = xbc_ref[0]
    acc = jnp.zeros((q, c), F32) + bc_ref[...]
    for i in range(k):
        s = 8 - (k - 1) + i
        acc = acc + wc_ref[i:i + 1, :] * hist_ref[s:s + q, :]
    xc = _silu(acc)
    xs = xc[:, 0:bd]
    gw = SSM_STATE
    bmf = [xc[:, bd + g * gw:bd + (g + 1) * gw] for g in range(SSM_GROUPS)]
    bm = [t.astype(BF16) for t in bmf]
    cm = [xc[:, bd + (SSM_GROUPS + g) * gw:bd + (SSM_GROUPS + g + 1) * gw].astype(BF16) for g in range(SSM_GROUPS)]

    row = lax.broadcasted_iota(jnp.int32, (q, q), 0)
    col = lax.broadcasted_iota(jnp.int32, (q, q), 1)
    causal = col <= row
    dt = _softplus(dtm_ref[0][:, 0:SSM_HEADS] + dtb_ref[...])
    dtt = _softplus(dtt_ref[0] + dtbt_ref[...])
    da = dt * (-jnp.exp(alog_ref[...]) * LOG2E)
    dat = dtt * (-jnp.exp(alogt_ref[...]) * LOG2E)
    cum = jnp.dot(causal.astype(F32), da, preferred_element_type=F32, precision=HIGHEST)
    cumt = jnp.dot(dat, (row <= col).astype(F32), preferred_element_type=F32, precision=HIGHEST)
    last = cum[q - 1:q, :]

    lane = lax.broadcasted_iota(jnp.int32, (q, LANES), 1)
    srow = lax.broadcasted_iota(jnp.int32, (LANES, gw), 0)
    p_dim = bd // SSM_HEADS
    rep = SSM_HEADS // SSM_GROUPS
    cbs = [_dot_nt(cm[g], bm[g]) for g in range(SSM_GROUPS)]
    for pair in range(SSM_HEADS // 2):
        h0, h1 = 2 * pair, 2 * pair + 1
        g = h0 // rep
        first = lane < p_dim
        cum_pair = jnp.where(first, cum[:, h0:h0 + 1], cum[:, h1:h1 + 1])
        last_pair = jnp.where(first, last[:, h0:h0 + 1], last[:, h1:h1 + 1])
        dt_pair = jnp.where(first, dt[:, h0:h0 + 1], dt[:, h1:h1 + 1])
        xdt = xs[:, pair * LANES:(pair + 1) * LANES] * dt_pair
        xb = xdt.astype(BF16)
        ys = []
        for hh in (h0, h1):
            dec = jnp.where(causal, jnp.exp2(cum[:, hh:hh + 1] - cumt[hh:hh + 1, :]), 0.0)
            ys.append(_dot((cbs[g] * dec).astype(BF16), xb))
        s_in = s_ref[pair]
        y_pair = jnp.where(first, ys[0], ys[1]) + jnp.exp2(cum_pair) * _dot_nt(cm[g], s_in.astype(BF16))
        y_ref[0, :, pair * LANES:(pair + 1) * LANES] = y_pair
        xw = xdt * jnp.exp2(last_pair - cum_pair)
        bmg = bm[g]
        if q < LANES:
            zpad = jnp.zeros((LANES - q, LANES), F32)
            xw = jnp.concatenate([xw, zpad], axis=0)
            bmg = jnp.concatenate([bmf[g], zpad], axis=0).astype(BF16)
        cs = _dot(xw.T.astype(BF16), bmg)
        cd = jnp.where(srow < p_dim, jnp.exp2(last[:, h0:h0 + 1]), jnp.exp2(last[:, h1:h1 + 1]))
        s_ref[pair] = s_in * cd + cs

    y = (y_ref[0] + dsk_ref[...] * xs) * _silu(z_ref[0])
    gs = bd // SSM_GROUPS
    for g in range(SSM_GROUPS):
        y_ref[0, :, g * gs:(g + 1) * gs] = _rms(y[:, g * gs:(g + 1) * gs]) * gn_ref[:, g * gs:(g + 1) * gs]
    sout_ref[0] = s_ref[...]


def _ssm(xbc, z, dtm, dtt, buf, s0, wc, bc, dtb, dtbt, alog, alogt, dsk, gn):
    b, l, c = xbc.shape
    bd = z.shape[-1]
    q = min(l, SSM_CHUNK)
    assert SSM_STATE == LANES and l % q == 0 and q % 8 == 0 and SSM_HEADS % 2 == 0 and bd // SSM_HEADS * 2 == LANES and (SSM_HEADS // SSM_GROUPS) % 2 == 0
    npair = SSM_HEADS // 2
    tok = lambda w_: pl.BlockSpec((1, q, w_), lambda i, j: (i, j, 0))
    per_b = lambda shp: pl.BlockSpec((1,) + shp, lambda i, j: (i,) + (0,) * len(shp))
    small = [wc, bc, dtb, dtbt, alog, alogt, dsk, gn]
    return pl.pallas_call(
        _ssm_kernel,
        grid=(b, l // q),
        in_specs=[tok(c), tok(bd), tok(LANES), pl.BlockSpec((1, SSM_HEADS, q), lambda i, j: (i, 0, j)),
                  per_b(buf.shape[1:]), per_b(s0.shape[1:])] + [_resident(a.shape) for a in small],
        out_specs=[tok(bd), per_b(s0.shape[1:])],
        out_shape=[jax.ShapeDtypeStruct((b, l, bd), F32), jax.ShapeDtypeStruct(s0.shape, F32)],
        scratch_shapes=[pltpu.VMEM((8 + q, c), F32), pltpu.VMEM((npair, LANES, SSM_STATE), F32)],
        compiler_params=_cparams(("parallel", "arbitrary")),
        name="ssm",
    )(xbc, z, dtm, dtt, buf, s0, *small)


def _rope_swap(x, lane, sign):
    half = ROPE_DIM // 2
    return jnp.where(lane < NOPE_DIM + half, pltpu.roll(x, LANES - half, 1), sign * pltpu.roll(x, half, 1))


def _qk_kernel(*refs, decode):
    if decode:
        (ql_ref, cos_ref, sin_ref, wq_ref, gq_ref, gk_ref, wukt_ref, qt_ref, qa_ref, qb_ref) = refs
    else:
        (ql_ref, cbf_ref, krp_ref, ct_ref, cos_ref, sin_ref, wq_ref, wk_ref, wuvt_ref, gq_ref, gk_ref,
         q_ref, k_ref, vt_ref) = refs
    bb, tl, _ = ql_ref.shape
    m = bb * tl
    ql = ql_ref[...].reshape(m, ql_ref.shape[-1])
    lane = lax.broadcasted_iota(jnp.int32, (m, LANES), 1)
    cos = jnp.broadcast_to(cos_ref[...][None], (bb, tl, LANES)).reshape(m, LANES)
    sin = jnp.broadcast_to(sin_ref[...][None], (bb, tl, LANES)).reshape(m, LANES)

    def norm_rope(x, g):
        x2 = x * x
        hi = x2.astype(BF16)
        lo = (x2 - hi.astype(F32)).astype(BF16)
        ones = jnp.ones((LANES, LANES), BF16)
        ss = _dot(hi, ones) + _dot(lo, ones)
        xn = x * lax.rsqrt(ss * (1.0 / QK_DIM) + EPS) * g
        return xn * cos + _rope_swap(xn, lane, 1.0) * sin

    if not decode:
        cbf = cbf_ref[...].reshape(m, cbf_ref.shape[-1])
        krp = krp_ref[...].reshape(m, LANES)
    for hh in range(MLA_HEADS):
        qf = norm_rope(_dot(ql, wq_ref[hh]), gq_ref[...]) * (QK_SCALE * LOG2E)
        if decode:
            gk = gk_ref[...]
            nope = lane < NOPE_DIM
            qn = jnp.where(nope, qf * gk, 0.0).astype(BF16)
            qt_ref[:, hh] = _dot(qn, wukt_ref[hh]).reshape(bb, tl, wukt_ref.shape[-1])
            qa = qf * gk
            qb = _rope_swap(qf, lane, -1.0) * gk
            qa_ref[:, hh] = qa[:, NOPE_DIM:QK_DIM].reshape(bb, tl, ROPE_DIM)
            qb_ref[:, hh] = qb[:, NOPE_DIM:QK_DIM].reshape(bb, tl, ROPE_DIM)
        else:
            q_ref[:, hh] = qf.reshape(bb, tl, LANES).astype(BF16)
            kf = norm_rope(_dot(cbf, wk_ref[hh]) + krp, gk_ref[...])
            k_ref[:, hh] = kf.reshape(bb, tl, LANES).astype(BF16)
            vt_ref[0, hh] = _dot(wuvt_ref[hh], ct_ref[0]).astype(BF16)


def _qk_prompt(ql, cbf, krp, ct, cos, sin, wq, wk, wuvt, gq, gk):
    b, l, _ = ql.shape
    bb, tl = _tok_blocks(b, l)
    assert bb == 1
    vd = wuvt.shape[1]
    tok = lambda w_: pl.BlockSpec((bb, tl, w_), lambda i, j: (i, j, 0))
    tab = pl.BlockSpec((tl, LANES), lambda i, j: (j, 0))
    head = pl.BlockSpec((bb, MLA_HEADS, tl, LANES), lambda i, j: (i, 0, j, 0))
    shp = jax.ShapeDtypeStruct((b, MLA_HEADS, l, LANES), BF16)
    return pl.pallas_call(
        functools.partial(_qk_kernel, decode=False),
        grid=(b // bb, l // tl),
        in_specs=[tok(ql.shape[-1]), tok(cbf.shape[-1]), tok(LANES),
                  pl.BlockSpec((1, ct.shape[1], tl), lambda i, j: (i, 0, j)), tab, tab,
                  _resident(wq.shape), _resident(wk.shape), _resident(wuvt.shape),
                  _resident(gq.shape), _resident(gk.shape)],
        out_specs=[head, head, pl.BlockSpec((1, MLA_HEADS, vd, tl), lambda i, j: (i, 0, 0, j))],
        out_shape=[shp, shp, jax.ShapeDtypeStruct((b, MLA_HEADS, vd, l), BF16)],
        compiler_params=_cparams(("parallel", "parallel")),
        name="qk_prompt",
    )(ql, cbf, krp, ct, cos, sin, wq, wk, wuvt, gq, gk)


def _q_decode(ql, cos, sin, wq, gq, gk, wukt):
    b, l, _ = ql.shape
    bb, tl = _tok_blocks(b, l)
    assert tl == l
    kv = wukt.shape[-1]
    tok = lambda w_: pl.BlockSpec((bb, tl, w_), lambda i: (i, 0, 0))
    head = lambda w_: pl.BlockSpec((bb, MLA_HEADS, tl, w_), lambda i: (i, 0, 0, 0))
    shp = lambda w_: jax.ShapeDtypeStruct((b, MLA_HEADS, l, w_), F32)
    return pl.pallas_call(
        functools.partial(_qk_kernel, decode=True),
        grid=(b // bb,),
        in_specs=[tok(ql.shape[-1]), _resident(cos.shape), _resident(sin.shape),
                  _resident(wq.shape), _resident(gq.shape), _resident(gk.shape), _resident(wukt.shape)],
        out_specs=[head(kv), head(ROPE_DIM), head(ROPE_DIM)],
        out_shape=[shp(kv), shp(ROPE_DIM), shp(ROPE_DIM)],
        compiler_params=_cparams(("parallel",)),
        name="q_decode",
    )(ql, cos, sin, wq, gq, gk, wukt)


def _flash_kernel(qi_ref, ki_ref, q_ref, k_ref, vt_ref, o_ref, m_ref, l_ref, acc_ref):
    tq = q_ref.shape[2]
    tk = k_ref.shape[2]
    step_i = pl.program_id(1)
    qi, ki = qi_ref[step_i], ki_ref[step_i]

    @pl.when(ki == 0)
    def _():
        m_ref[...] = jnp.full(m_ref.shape, -jnp.inf, F32)
        l_ref[...] = jnp.zeros(l_ref.shape, F32)
        acc_ref[...] = jnp.zeros(acc_ref.shape, F32)

    vd = vt_ref.shape[2]

    def step(masked):
        if masked:
            keep = (ki * tk + lax.broadcasted_iota(jnp.int32, (tk, tq), 0)
                    <= qi * tq + lax.broadcasted_iota(jnp.int32, (tk, tq), 1))
        def values(hh, alpha, pb):
            rows = slice(hh * vd, (hh + 1) * vd)
            acc_ref[rows, :] = alpha * acc_ref[rows, :] + _dot(vt_ref[0, hh], pb)

        ahead = [_dot_nt(k_ref[0, hh], q_ref[0, hh]) for hh in range(min(QK_AHEAD, MLA_HEADS))]
        pending = None
        for hh in range(MLA_HEADS):
            st = ahead[hh]
            if hh + QK_AHEAD < MLA_HEADS:
                ahead.append(_dot_nt(k_ref[0, hh + QK_AHEAD], q_ref[0, hh + QK_AHEAD]))
            if pending is not None:
                values(*pending)
            if masked:
                st = jnp.where(keep, st, -jnp.inf)
            m_prev = m_ref[hh:hh + 1, :]
            m_new = jnp.maximum(m_prev, jnp.max(st, axis=0, keepdims=True))
            alpha = jnp.exp2(m_prev - m_new)
            p = jnp.exp2(st - m_new)
            l_ref[hh:hh + 1, :] = alpha * l_ref[hh:hh + 1, :] + jnp.sum(p, axis=0, keepdims=True)
            m_ref[hh:hh + 1, :] = m_new
            pending = (hh, alpha, p.astype(BF16))
        values(*pending)

    straddles = (ki + 1) * tk - 1 > qi * tq

    @pl.when(jnp.logical_not(straddles))
    def _():
        step(False)

    @pl.when(straddles)
    def _():
        step(True)

    @pl.when((ki + 1) * tk >= (qi + 1) * tq)
    def _():
        for hh in range(MLA_HEADS):
            rows = slice(hh * vd, (hh + 1) * vd)
            acc_ref[rows, :] = acc_ref[rows, :] / l_ref[hh:hh + 1, :]
        o_ref[0] = acc_ref[...].T


def _flash(q, k, vt):
    b, h, l, _ = q.shape
    tq, tk = min(l, ATTN_Q_BLOCK), min(l, ATTN_K_BLOCK)
    assert l % tq == 0 and l % tk == 0
    vd = vt.shape[2]
    pairs = [(qi, ki) for qi in range(l // tq) for ki in range(l // tk) if ki * tk <= qi * tq + tq - 1]
    qi_of = jnp.asarray([p[0] for p in pairs], jnp.int32)
    ki_of = jnp.asarray([p[1] for p in pairs], jnp.int32)
    grid_spec = pltpu.PrefetchScalarGridSpec(
        num_scalar_prefetch=2,
        grid=(b, len(pairs)),
        in_specs=[pl.BlockSpec((1, h, tq, LANES), lambda i, s, qo, ko: (i, 0, qo[s], 0)),
                  pl.BlockSpec((1, h, tk, LANES), lambda i, s, qo, ko: (i, 0, ko[s], 0)),
                  pl.BlockSpec((1, h, vd, tk), lambda i, s, qo, ko: (i, 0, 0, ko[s]))],
        out_specs=pl.BlockSpec((1, tq, h * vd), lambda i, s, qo, ko: (i, qo[s], 0)),
        scratch_shapes=[pltpu.VMEM((h, tq), F32), pltpu.VMEM((h, tq), F32), pltpu.VMEM((h * vd, tq), F32)],
    )
    return pl.pallas_call(
        _flash_kernel,
        grid_spec=grid_spec,
        out_shape=jax.ShapeDtypeStruct((b, l, h * vd), F32),
        compiler_params=_cparams(("parallel", "arbitrary")),
        name="flash",
    )(qi_of, ki_of, q, k, vt)


def _decode_kernel(pt_ref, qt_ref, qab_ref, tab_ref, tabn_ref, new_ref, wukt_ref, wuv_ref, pool_ref,
                   o_ref, buf_ref, sem, lhs_ref, ctb_ref, a_ref, s_ref, m_ref, l_ref, acc_ref,
                   *, pages, group, layer):
    seq, j = pl.program_id(0), pl.program_id(1)
    nseq, nj = pl.num_programs(0), pl.num_programs(1)
    nrow = qt_ref.shape[1]
    lq = nrow // MLA_HEADS
    kvw = qt_ref.shape[2]
    nkt = wukt_ref.shape[0]
    psz = buf_ref.shape[3]
    step = seq * nj + j
    slot = step % 2

    def page_copy(sq, chunk, pi, sl):
        return pltpu.make_async_copy(pool_ref.at[layer, pt_ref[sq, chunk * pages + pi]], buf_ref.at[sl, pi], sem.at[sl])

    @pl.when(step == 0)
    def _():
        for pi in range(pages):
            page_copy(seq, j, pi, slot).start()

    @pl.when(step + 1 < nseq * nj)
    def _():
        wrap = j + 1 == nj
        nxt_seq = jnp.where(wrap, seq + 1, seq)
        nxt_j = jnp.where(wrap, 0, j + 1)
        for pi in range(pages):
            page_copy(nxt_seq, nxt_j, pi, 1 - slot).start()

    @pl.when(j == 0)
    def _():
        m_ref[...] = jnp.full(m_ref.shape, SCORE_FLOOR, F32)
        l_ref[...] = jnp.zeros(l_ref.shape, F32)
        acc_ref[...] = jnp.zeros(acc_ref.shape, F32)
        lhs_ref[0:nkt, :] = wukt_ref[...]
        lhs_ref[nkt:nkt + nrow, :] = qt_ref[0].astype(BF16)
        s_ref[1] = jnp.full(s_ref.shape[1:], -jnp.inf, F32)
        ctb_ref[1] = jnp.zeros(ctb_ref.shape[1:], BF16)

    qab = qab_ref[0].astype(BF16)
    cur = j % 2

    for pi in range(pages):
        page_copy(seq, j, pi, slot).wait()
    page_refs = [buf_ref.at[slot, pi] for pi in range(pages)]

    def scores(ct, krt, cct, snt):
        nk = ct.shape[1]
        ctb = ct.astype(BF16)
        a = _dot(lhs_ref[...], ctb)
        kt = a[0:nkt]
        ss = jnp.sum((kt * kt).reshape(MLA_HEADS, nkt // MLA_HEADS, nk), axis=1)
        ss = ss + jnp.sum(krt * krt, axis=0, keepdims=True)
        inv = lax.rsqrt(ss * (1.0 / QK_DIM) + EPS)
        feats = jnp.concatenate([krt * cct, krt * snt], axis=0).astype(BF16)
        s = a[nkt:nkt + nrow] + _dot(qab, feats)
        return (s.reshape(MLA_HEADS, lq, nk) * inv[:, None, :]).reshape(nrow, nk), ctb

    def softmax_stats(s):
        m_prev = m_ref[...]
        m_new = jnp.maximum(m_prev, jnp.max(s, axis=-1, keepdims=True))
        alpha = jnp.exp2(m_prev - m_new)
        p = jnp.exp2(s - m_new)
        l_ref[...] = alpha * l_ref[...] + jnp.sum(p, axis=-1, keepdims=True)
        m_ref[...] = m_new
        return alpha, p.astype(BF16)

    for pi in range(pages):
        ctb_ref[cur, :, pi * psz:(pi + 1) * psz] = page_refs[pi][0:kvw, :].astype(BF16)
    a_ref[...] = _dot(lhs_ref[...], ctb_ref[cur])
    alpha_prev, pb_prev = softmax_stats(s_ref[1 - cur])
    pv_prev = _dot_nt(pb_prev, ctb_ref[1 - cur])
    for g0 in range(0, pages, group):
        refs = page_refs[g0:g0 + group]
        lanes = slice(g0 * psz, (g0 + group) * psz)
        nk = group * psz
        kt = a_ref[0:nkt, lanes]
        krt = jnp.concatenate([r[kvw:kvw + ROPE_DIM, :] for r in refs], axis=1) if group > 1 else refs[0][kvw:kvw + ROPE_DIM, :]
        ss = jnp.sum((kt * kt).reshape(MLA_HEADS, nkt // MLA_HEADS, nk), axis=1)
        ss = ss + jnp.sum(krt * krt, axis=0, keepdims=True)
        inv = lax.rsqrt(ss * (1.0 / QK_DIM) + EPS)
        feats = jnp.concatenate([krt * tab_ref[0, :, lanes], krt * tab_ref[1, :, lanes]], axis=0).astype(BF16)
        s = a_ref[nkt:nkt + nrow, lanes] + _dot(qab, feats)
        s_ref[cur, :, lanes] = (s.reshape(MLA_HEADS, lq, nk) * inv[:, None, :]).reshape(nrow, nk)
    acc_ref[...] = alpha_prev * acc_ref[...] + pv_prev

    @pl.when(j == pl.num_programs(1) - 1)
    def _():
        nk = new_ref.shape[2]
        tq = lax.broadcasted_iota(jnp.int32, (nrow, nk), 0) % lq
        tk = lax.broadcasted_iota(jnp.int32, (nrow, nk), 1)
        s, ctb = scores(new_ref[0, 0:kvw, :], new_ref[0, kvw:kvw + ROPE_DIM, :], tabn_ref[0], tabn_ref[1])
        alpha_l, pb_l = softmax_stats(s_ref[cur])
        acc_ref[...] = alpha_l * acc_ref[...] + _dot_nt(pb_l, ctb_ref[cur])
        alpha_n, pb_n = softmax_stats(jnp.where(tk <= tq, s, -jnp.inf))
        acc_ref[...] = alpha_n * acc_ref[...] + _dot_nt(pb_n, ctb)
        ctx = (acc_ref[...] / l_ref[...]).astype(BF16)
        vd = wuv_ref.shape[-1]
        for hh in range(MLA_HEADS):
            o_ref[0, :, hh * vd:(hh + 1) * vd] = _dot(ctx[hh * lq:(hh + 1) * lq, :], wuv_ref[hh])


def _decode(page_table, qt, qab, tab, tabn, rows_new_t, wukt, wuv, pool_t, layer):
    bs, nrow, kvw = qt.shape
    n_pages = page_table.shape[1]
    cw, psz = pool_t.shape[2], pool_t.shape[3]
    pages = math.gcd(n_pages, DECODE_PAGES)
    group = math.gcd(pages, DECODE_GROUP)
    lq = nrow // MLA_HEADS
    vd = wuv.shape[-1]
    per_s = lambda shp: pl.BlockSpec((1,) + shp, lambda s, j, pt: (s,) + (0,) * len(shp))
    grid_spec = pltpu.PrefetchScalarGridSpec(
        num_scalar_prefetch=1,
        grid=(bs, n_pages // pages),
        in_specs=[per_s((nrow, kvw)), per_s((nrow, 2 * ROPE_DIM)),
                  pl.BlockSpec((2, ROPE_DIM, pages * psz), lambda s, j, pt: (0, 0, j)),
                  _resident(tabn.shape), per_s(rows_new_t.shape[1:]),
                  _resident(wukt.shape), _resident(wuv.shape), pl.BlockSpec(memory_space=pl.ANY)],
        out_specs=per_s((lq, MLA_HEADS * vd)),
        scratch_shapes=[pltpu.VMEM((2, pages, cw, psz), F32), pltpu.SemaphoreType.DMA((2,)),
                        pltpu.VMEM((wukt.shape[0] + nrow, kvw), BF16),
                        pltpu.VMEM((2, kvw, pages * psz), BF16),
                        pltpu.VMEM((wukt.shape[0] + nrow, pages * psz), F32), pltpu.VMEM((2, nrow, pages * psz), F32),
                        pltpu.VMEM((nrow, 1), F32), pltpu.VMEM((nrow, 1), F32), pltpu.VMEM((nrow, kvw), F32)],
    )
    return pl.pallas_call(
        functools.partial(_decode_kernel, pages=pages, group=group, layer=layer),
        grid_spec=grid_spec,
        out_shape=jax.ShapeDtypeStruct((bs, lq, MLA_HEADS * vd), F32),
        compiler_params=_cparams(("arbitrary", "arbitrary")),
        name="decode",
    )(page_table, qt, qab, tab, tabn, rows_new_t, wukt, wuv, pool_t)


def _merge_kernel(h_ref, sh_ref, sc_ref, gt_ref, g_ref, ya_ref, yb_ref, yc_ref, yd_ref,
                  wg_ref, bg_ref, wb_ref, wo_ref, o_ref):
    bb, tl, d = h_ref.shape
    m = bb * tl
    h = h_ref[...]
    n = (_rms(h) * g_ref[...]) * (1.0 + sc_ref[...]) + sh_ref[...]
    n = n.reshape(m, d).astype(BF16)
    merged = None
    for r, y_ref in enumerate((ya_ref, yb_ref, yc_ref, yd_ref)):
        gate = jax.nn.sigmoid(_dot(n, wg_ref[r]) + bg_ref[r])
        term = gate * _dot(y_ref[...].reshape(m, y_ref.shape[-1]).astype(BF16), wb_ref[r])
        merged = term if merged is None else merged + term
    out = _dot(merged.astype(BF16), wo_ref[...])
    o_ref[...] = h + gt_ref[...] * out.reshape(bb, tl, d)


def _merge(h, sh, sc, gt, g, ys, wg, bg, wb, wo):
    b, l, d = h.shape
    bb, tl = _tok_blocks(b, l)
    tok = lambda w_: pl.BlockSpec((bb, tl, w_), lambda i, j: (i, j, 0))
    mod = pl.BlockSpec((bb, 1, d), lambda i, j: (i, 0, 0))
    return pl.pallas_call(
        _merge_kernel,
        grid=(b // bb, l // tl),
        in_specs=[tok(d), mod, mod, mod, _resident((1, d))] + [tok(y.shape[-1]) for y in ys]
                 + [_resident(wg.shape), _resident(bg.shape), _resident(wb.shape), _resident(wo.shape)],
        out_specs=tok(d),
        out_shape=jax.ShapeDtypeStruct(h.shape, F32),
        compiler_params=_cparams(("parallel", "parallel")),
        name="merge",
    )(h, sh, sc, gt, g, *ys, wg, bg, wb, wo)


def _rope_tables(pos):
    half = ROPE_DIM // 2
    inv = ROPE_THETA ** (-jnp.arange(half, dtype=F32) / half)
    ang = pos.astype(F32)[:, None] * inv
    return jnp.cos(ang), jnp.sin(ang)


def _lane_tables(pos):
    cos, sin = _rope_tables(pos)
    n = pos.shape[0]
    cos_t = jnp.concatenate([jnp.ones((n, NOPE_DIM), F32), cos, cos, jnp.zeros((n, LANES - QK_DIM), F32)], axis=1)
    sin_t = jnp.concatenate([jnp.zeros((n, NOPE_DIM), F32), -sin, sin, jnp.zeros((n, LANES - QK_DIM), F32)], axis=1)
    return cos_t, sin_t


def _pad_lanes(x, before, total):
    pad = [(0, 0)] * (x.ndim - 1) + [(before, total - before - x.shape[-1])]
    return jnp.pad(x, pad)


def _layer_weights(lw):
    d = lw["w_in"].shape[0]
    bd = lw["g_gm_v"].shape[0]
    q_lora, kv_lora = lw["g_q_lat"].shape[0], lw["g_kv_lat"].shape[0]
    layout, _ = _proj_layout(bd, q_lora, kv_lora)
    w = {}
    w["layout"] = layout
    w["w_ada"] = lw["w_ada"].astype(BF16)
    w["b_ada"] = lw["b_ada"][None, :]
    w["g_norm"] = [lw["g_norm"][i][None, :] for i in range(3)]
    w["w_ffn_in"] = [lw["w_ffn_in"][i].astype(BF16) for i in range(2)]
    w["w_ffn_out"] = [lw["w_ffn_out"][i].astype(BF16) for i in range(2)]
    w["w_proj"] = _proj_weight(lw["w_in"], bd, q_lora, kv_lora)
    w["g_gm_v"] = lw["g_gm_v"][None, :]
    w["g_q_lat"] = lw["g_q_lat"][None, :]
    w["g_kv_lat"] = lw["g_kv_lat"][None, :]
    w["w_spatial"] = lw["w_spatial"]
    w["b_spatial_t"] = lw["b_spatial"].T
    w["w_ssm_conv"] = lw["w_ssm_conv"]
    w["b_ssm_conv"] = lw["b_ssm_conv"][None, :]
    w["dt_bias"] = lw["dt_bias"][None, :]
    w["dt_bias_t"] = lw["dt_bias"][:, None]
    w["a_log"] = lw["a_log"][None, :]
    w["a_log_t"] = lw["a_log"][:, None]
    w["d_skip"] = jnp.repeat(lw["d_skip"], bd // SSM_HEADS)[None, :]
    w["g_ssm_norm"] = lw["g_ssm_norm"][None, :]
    w["w_sc_conv"] = lw["w_sc_conv"]
    w["wq"] = _pad_lanes(jnp.moveaxis(lw["w_uq"], 1, 0), 0, LANES).astype(BF16)
    wuk = jnp.moveaxis(lw["w_uk"], 1, 0)
    w["wk"] = _pad_lanes(wuk, 0, LANES).astype(BF16)
    wukt = jnp.swapaxes(wuk, 1, 2)
    w["wukt_pad"] = jnp.pad(wukt, ((0, 0), (0, LANES - NOPE_DIM), (0, 0))).astype(BF16)
    w["wukt_flat"] = wukt.reshape(MLA_HEADS * NOPE_DIM, kv_lora).astype(BF16)
    w["wuv"] = jnp.moveaxis(lw["w_uv"], 1, 0).astype(BF16)
    w["wuvt"] = jnp.swapaxes(w["wuv"], 1, 2)
    w["gq"] = _pad_lanes(lw["g_qk"][0][None, :], 0, LANES)
    w["gk"] = _pad_lanes(lw["g_qk"][1][None, :], 0, LANES)
    w["w_gate"] = lw["w_gate"].astype(BF16)
    w["b_gate"] = lw["b_gate"][:, None, :]
    w["w_branch_out"] = lw["w_branch_out"].astype(BF16)
    w["w_out"] = lw["w_out"].astype(BF16)
    return w


def _trunk_layer(x, mod, w, ssm_buf, ssm_s0, sc_buf, attn_fn):
    b, l, d = x.shape
    sh1, sc1, gt1, sh2, sc2, gt2, sh3, sc3, gt3 = [m[:, None, :] for m in jnp.split(mod, N_MOD, axis=-1)]
    h = _ffn(x, sh1, sc1, gt1, w["g_norm"][0], w["w_ffn_in"][0], w["w_ffn_out"][0])
    (y_a, a_v, b_z, b_xbc, y_c, c_ptail, ql, rows, cbf, krp, dtm, *ct) = _proj(
        h, sh2, sc2, w["g_norm"][1], w["w_proj"], w["g_gm_v"], w["g_q_lat"], w["g_kv_lat"],
        w["w_spatial"], w["b_spatial_t"], sc_buf, w["w_sc_conv"], w["layout"])
    sc_buf_new = jnp.concatenate([sc_buf, c_ptail[:, -min(l, 8):]], axis=1)[:, -(SC_WIDTH - 1):]
    dtt = jnp.swapaxes(dtm[:, :, 0:SSM_HEADS], 1, 2)
    npair = SSM_HEADS // 2
    s0 = ssm_s0.reshape(b, npair, LANES, SSM_STATE)
    y_b, s_new = _ssm(b_xbc, b_z, dtm, dtt, ssm_buf, s0, w["w_ssm_conv"], w["b_ssm_conv"],
                      w["dt_bias"], w["dt_bias_t"], w["a_log"], w["a_log_t"], w["d_skip"], w["g_ssm_norm"])
    s_new = s_new.reshape(ssm_s0.shape)
    ssm_buf_new = jnp.concatenate([ssm_buf, b_xbc], axis=1)[:, -(SSM_CONV - 1):]
    y_d = attn_fn(ql, rows, cbf, krp, *ct)
    h = _merge(h, sh2, sc2, gt2, w["g_norm"][1], (y_a, y_b, y_c, y_d), w["w_gate"], w["b_gate"],
               w["w_branch_out"], w["w_out"])
    h = _ffn(h, sh3, sc3, gt3, w["g_norm"][2], w["w_ffn_in"][1], w["w_ffn_out"][1])
    return h, rows, ssm_buf_new, s_new, sc_buf_new, a_v


def kernel(x_prompt, x_sample, c_prompt, c_sample, cache_mla, page_table, state_ssm, state_ssm_conv,
           state_short_conv, w_ada, b_ada, g_norm, w_ffn_in, w_ffn_out, w_in, g_gm_v, w_spatial, b_spatial,
           w_ssm_conv, b_ssm_conv, dt_bias, a_log, d_skip, g_ssm_norm, w_sc_conv, g_q_lat, w_uq, g_kv_lat,
           w_uk, w_uv, g_qk, w_branch_out, w_gate, b_gate, w_out):
    params = dict(w_ada=w_ada, b_ada=b_ada, g_norm=g_norm, w_ffn_in=w_ffn_in, w_ffn_out=w_ffn_out, w_in=w_in,
                  g_gm_v=g_gm_v, w_spatial=w_spatial, b_spatial=b_spatial, w_ssm_conv=w_ssm_conv,
                  b_ssm_conv=b_ssm_conv, dt_bias=dt_bias, a_log=a_log, d_skip=d_skip, g_ssm_norm=g_ssm_norm,
                  w_sc_conv=w_sc_conv, g_q_lat=g_q_lat, w_uq=w_uq, g_kv_lat=g_kv_lat, w_uk=w_uk, w_uv=w_uv,
                  g_qk=g_qk, w_branch_out=w_branch_out, w_gate=w_gate, b_gate=b_gate, w_out=w_out)
    depth = w_ada.shape[0]
    bp, lp, d = x_prompt.shape
    bs, ls, _ = x_sample.shape
    page = cache_mla.shape[2]
    past = page_table.shape[1] * page
    bd = g_gm_v.shape[1]
    conv_ch = w_ssm_conv.shape[-1]

    pos_p = jnp.arange(lp, dtype=jnp.int32)
    pos_s = past + jnp.arange(ls, dtype=jnp.int32)
    cos_p, sin_p = _lane_tables(pos_p)
    cos_s, sin_s = _lane_tables(pos_s)
    def key_tables(pos, width):
        cos, sin = _rope_tables(pos)
        t = jnp.stack([jnp.concatenate([cos, cos], axis=1).T, jnp.concatenate([sin, sin], axis=1).T])
        return jnp.pad(t, ((0, 0), (0, 0), (0, width - pos.shape[0])))

    tab_k = key_tables(jnp.arange(past, dtype=jnp.int32), past)
    new_w = -(-ls // LANES) * LANES
    tab_n = key_tables(pos_s, new_w)
    pool_t = jnp.swapaxes(cache_mla, 2, 3)
    c_all = jnp.concatenate([c_prompt, c_sample], axis=0)

    yp, ys = x_prompt, x_sample
    outs = [[] for _ in range(9)]
    for layer in range(depth):
        w = _layer_weights({k: v[layer] for k, v in params.items()})
        mod = _ada(c_all, w["w_ada"], w["b_ada"])

        def attn_prompt(ql, rows, cbf, krp, ct, w=w):
            q, k, vt = _qk_prompt(ql, cbf, krp, ct, cos_p, sin_p, w["wq"], w["wk"], w["wuvt"], w["gq"], w["gk"])
            return _flash(q, k, vt)

        def attn_sample(ql, rows, cbf, krp, *unused, w=w, layer=layer):
            qt, qa, qb = _q_decode(ql, cos_s, sin_s, w["wq"], w["gq"], w["gk"], w["wukt_pad"])
            flat = lambda t: t.reshape(bs, MLA_HEADS * ls, t.shape[-1])
            rows_t = jnp.pad(jnp.swapaxes(rows, 1, 2), ((0, 0), (0, 0), (0, new_w - ls)))
            return _decode(page_table, flat(qt), flat(jnp.concatenate([qa, qb], axis=-1)), tab_k, tab_n, rows_t,
                           w["wukt_flat"], w["wuv"], pool_t, layer)

        yp, r, cb, s, scb, _ = _trunk_layer(
            yp, mod[:bp], w,
            jnp.zeros((bp, SSM_CONV - 1, conv_ch), F32),
            jnp.zeros((bp, SSM_HEADS, bd // SSM_HEADS, SSM_STATE), F32),
            jnp.zeros((bp, SC_WIDTH - 1, bd), F32), attn_prompt)
        for lst, v in zip((outs[0], outs[2], outs[4], outs[6]), (r, s, cb, scb)):
            lst.append(v)
        ys, r, cb, s, scb, v = _trunk_layer(
            ys, mod[bp:], w, state_ssm_conv[layer], state_ssm[layer], state_short_conv[layer], attn_sample)
        for lst, val in zip((outs[1], outs[3], outs[5], outs[7], outs[8]), (r, s, cb, scb, v)):
            lst.append(val)
    st = jnp.stack
    return (yp, ys, st(outs[0]), st(outs[1]), st(outs[2]), st(outs[3]), st(outs[4]), st(outs[5]),
            st(outs[6]), st(outs[7]), st(outs[8]))
```
